```python
import math
import jax, jax.numpy as jnp
from jax import lax
import numpy as np

D_MODEL = 2048
BATCH = 4
SEQ = 4096
DEPTH = 1

DA_HEADS = 8
DA_HEAD_DIM = 64
DA_V_DIM = 2 * DA_HEAD_DIM
ROT_DIM = DA_HEAD_DIM // 4
ROPE_THETA = 500000.0
Q_BLOCK = 128
SUBLN_EPS = 1e-5
RW_HEADS = 16
RW_HEAD_DIM = 64
RW_WIDTH = RW_HEADS * RW_HEAD_DIM
DECAY_LORA = 96
AAA_LORA = 96
GATE_LORA = 256
GN_EPS = 64e-5
N_GROUPS = 4
EXPERTS_PER_GROUP = 8
N_EXPERTS = N_GROUPS * EXPERTS_PER_GROUP
TOP_K = 2
D_EXPERT = 1024
ROW_BLOCK = 128
NORM_EPS = 1e-6

DA_QK_WIDTH = DA_HEADS * 2 * DA_HEAD_DIM
DA_V_WIDTH = DA_HEADS * DA_V_DIM
RW_SHIFT_WIDTH = 3 * RW_WIDTH + DECAY_LORA + AAA_LORA + GATE_LORA
IN_SPLITS = (DA_QK_WIDTH, DA_QK_WIDTH, DA_V_WIDTH, RW_SHIFT_WIDTH, D_MODEL, D_MODEL)
IN_WIDTH = sum(IN_SPLITS)
IN_SPLIT_POINTS = [int(v) for v in np.cumsum(IN_SPLITS)[:-1]]
RW_SPLITS = (RW_WIDTH, RW_WIDTH, RW_WIDTH, DECAY_LORA, AAA_LORA, GATE_LORA)
RW_SPLIT_POINTS = [int(v) for v in np.cumsum(RW_SPLITS)[:-1]]

kernel_name = "hybrid_diffattn_rwkv7_hiermoe"


def rms_norm(x, g, eps):
    xf = x.astype(jnp.float32)
    y = xf * lax.rsqrt(jnp.mean(xf * xf, axis=-1, keepdims=True) + eps)
    return (y * g.astype(jnp.float32)).astype(x.dtype)


def rope_tables(seq):
    inv_freq = ROPE_THETA ** (-jnp.arange(0, ROT_DIM, 2, dtype=jnp.float32) / ROT_DIM)
    ang = jnp.arange(seq, dtype=jnp.float32)[:, None] * inv_freq[None, :]
    return jnp.cos(ang), jnp.sin(ang)


def partial_rope(x, cos, sin):
    half = ROT_DIM // 2
    c = cos[None, :, None, None, :].astype(x.dtype)
    s = sin[None, :, None, None, :].astype(x.dtype)
    x1, x2, xp = x[..., :half], x[..., half:ROT_DIM], x[..., ROT_DIM:]
    return jnp.concatenate([x1 * c - x2 * s, x2 * c + x1 * s, xp], axis=-1)


def diff_attention(q, k, v, q_norm_g, k_norm_g, lam_q1, lam_k1, lam_q2, lam_k2,
                   subln_g, lam_init, cos, sin):
    B, S = q.shape[0], q.shape[1]
    nb = S // Q_BLOCK
    q = partial_rope(rms_norm(q, q_norm_g, NORM_EPS), cos, sin) * (DA_HEAD_DIM ** -0.5)
    k = partial_rope(rms_norm(k, k_norm_g, NORM_EPS), cos, sin)
    lam = (jnp.exp(jnp.sum(lam_q1.astype(jnp.float32) * lam_k1.astype(jnp.float32)))
           - jnp.exp(jnp.sum(lam_q2.astype(jnp.float32) * lam_k2.astype(jnp.float32)))
           + lam_init)
    qb = q.reshape(B, nb, Q_BLOCK, DA_HEADS, 2, DA_HEAD_DIM).transpose(1, 0, 3, 4, 2, 5)
    kt = k.transpose(0, 2, 3, 1, 4)
    vt = v.transpose(0, 2, 1, 3)
    kpos = jnp.arange(S)

    def block(args):
        q_blk, i = args
        s = jnp.einsum('bhcqd,bhckd->bhcqk', q_blk, kt).astype(jnp.float32)
        qpos = i * Q_BLOCK + jnp.arange(Q_BLOCK)
        s = jnp.where(kpos[None, :] <= qpos[:, None], s, -jnp.inf)
        p = jax.nn.softmax(s, axis=-1)
        attn = p[:, :, 0] - lam * p[:, :, 1]
        return jnp.einsum('bhqk,bhkd->bhqd', attn.astype(vt.dtype), vt)

    o = lax.map(block, (qb, jnp.arange(nb)))
    o = o.transpose(1, 0, 3, 2, 4).reshape(B, S, DA_HEADS, DA_V_DIM)
    o = rms_norm(o, subln_g, SUBLN_EPS) * (1.0 - lam_init)
    return o.reshape(B, S, DA_V_WIDTH)


def token_shift(z, mu):
    prev = jnp.pad(z, ((0, 0), (1, 0), (0, 0)))[:, :-1]
    return z + (prev - z) * mu


def rwkv7_step(state, inp):
    r_t, w_t, k_t, v_t, a_t, b_t = inp
    sa = jnp.einsum('bhij,bhj->bhi', state, a_t)
    state = (state * w_t[:, :, None, :] + sa[..., None] * b_t[:, :, None, :]
             + v_t[..., None] * k_t[:, :, None, :])
    y = jnp.einsum('bhij,bhj->bhi', state, r_t)
    return state, y


def rwkv7_time_mix(z, w0, w_up, a0, a_up, g_up, k_k, k_a, r_k, lnx_g, lnx_b):
    B, S = z.shape[0], z.shape[1]
    f32 = jnp.float32
    r, k, v, dw, da, dg = jnp.split(z, RW_SPLIT_POINTS, axis=-1)
    w = -jax.nn.softplus(-(w0 + jnp.tanh(dw) @ w_up)) - 0.5
    decay = jnp.exp(-jnp.exp(w.astype(f32)))
    a = jax.nn.sigmoid(a0 + da @ a_up)
    g = jax.nn.sigmoid(dg) @ g_up
    hs = (B, S, RW_HEADS, RW_HEAD_DIM)
    r, k, v, a, decay = (t.reshape(hs).astype(f32) for t in (r, k, v, a, decay))
    kk = k * k_k.reshape(RW_HEADS, RW_HEAD_DIM).astype(f32)
    kk = kk / jnp.maximum(jnp.sqrt(jnp.sum(kk * kk, axis=-1, keepdims=True)), 1e-12)
    k = k * (1.0 + (a - 1.0) * k_a.reshape(RW_HEADS, RW_HEAD_DIM).astype(f32))
    xs = tuple(t.transpose(1, 0, 2, 3) for t in (r, decay, k, v, -kk, kk * a))
    state0 = jnp.zeros((B, RW_HEADS, RW_HEAD_DIM, RW_HEAD_DIM), f32)
    _, ys = lax.scan(rwkv7_step, state0, xs)
    y = ys.transpose(1, 0, 2, 3)
    mu = jnp.mean(y, axis=-1, keepdims=True)
    var = jnp.mean(jnp.square(y - mu), axis=-1, keepdims=True)
    yn = ((y - mu) * lax.rsqrt(var + GN_EPS)).reshape(B, S, RW_WIDTH)
    yn = yn * lnx_g.astype(f32) + lnx_b.astype(f32)
    bonus = jnp.sum(r * k * r_k.astype(f32), axis=-1, keepdims=True) * v
    out = (yn + bonus.reshape(B, S, RW_WIDTH)) * g.astype(f32)
    return out.astype(z.dtype)


def hier_moe(h, router_g, router_g_b, router_e, router_e_b, w_gate_e, w_up_e, w_down_e):
    T, D = h.shape
    f32 = jnp.float32
    p_group = jax.nn.softmax((h @ router_g).astype(f32) + router_g_b.astype(f32), axis=-1)
    p_g_top, g_idx = lax.top_k(p_group, 1)
    e_logits = ((h @ router_e).astype(f32) + router_e_b.astype(f32)).reshape(T, N_GROUPS, EXPERTS_PER_GROUP)
    e_sel = jnp.take_along_axis(e_logits, g_idx[:, :, None], axis=1)[:, 0]
    p_e_top, e_local = lax.top_k(jax.nn.softmax(e_sel, axis=-1), TOP_K)
    gates = p_g_top * p_e_top / jnp.sum(p_e_top, axis=-1, keepdims=True)
    expert = g_idx * EXPERTS_PER_GROUP + e_local

    A = T * TOP_K
    flat_e = expert.reshape(A)
    flat_g = gates.reshape(A)
    flat_t = jnp.repeat(jnp.arange(T, dtype=jnp.int32), TOP_K)
    order = jnp.argsort(flat_e)
    se = flat_e[order]
    counts = jnp.zeros((N_EXPERTS,), jnp.int32).at[flat_e].add(1)
    padded = (counts + ROW_BLOCK - 1) // ROW_BLOCK * ROW_BLOCK
    start = jnp.cumsum(counts) - counts
    pend = jnp.cumsum(padded)
    pstart = pend - padded
    dest = pstart[se] + jnp.arange(A, dtype=jnp.int32) - start[se]
    n_rows = A + N_EXPERTS * ROW_BLOCK
    n_blocks = n_rows // ROW_BLOCK
    row_tok = jnp.zeros((n_rows,), jnp.int32).at[dest].set(flat_t[order])
    row_gate = jnp.zeros((n_rows,), f32).at[dest].set(flat_g[order])
    block_e = jnp.minimum(jnp.searchsorted(pend, jnp.arange(n_blocks, dtype=jnp.int32) * ROW_BLOCK,
                                           side='right'), N_EXPERTS - 1).astype(jnp.int32)

    def expert_block(args):
        tok, e = args
        xb = h[tok]
        return (jax.nn.silu(xb @ w_gate_e[e]) * (xb @ w_up_e[e])) @ w_down_e[e]

    out = lax.map(expert_block, (row_tok.reshape(n_blocks, ROW_BLOCK), block_e))
    contrib = (out.reshape(n_rows, D).astype(f32) * row_gate[:, None]).astype(h.dtype)
    return jnp.zeros_like(h).at[row_tok].add(contrib)


def setup_inputs(seed: int = 0) -> dict:
    key = jax.random.key(seed)
    ks = iter(jax.random.split(key, 48))
    L, D = DEPTH, D_MODEL
    f32 = jnp.float32

    def nrm(shape, scale):
        return jax.random.normal(next(ks), shape, f32) * scale

    def gain(shape):
        return 1.0 + nrm(shape, 0.02)

    def unif(shape, lo, hi):
        return jax.random.uniform(next(ks), shape, f32, lo, hi)

    return {
        "x": nrm((BATCH, SEQ, D), 1.0),
        "norm1_g": gain((L, D)),
        "w_in": nrm((L, D, IN_WIDTH), D ** -0.5),
        "q_norm_g": gain((L, DA_HEAD_DIM)),
        "k_norm_g": gain((L, DA_HEAD_DIM)),
        "lam_q1": nrm((L, DA_HEAD_DIM), 0.1),
        "lam_k1": nrm((L, DA_HEAD_DIM), 0.1),
        "lam_q2": nrm((L, DA_HEAD_DIM), 0.1),
        "lam_k2": nrm((L, DA_HEAD_DIM), 0.1),
        "subln_g": gain((L, DA_V_DIM)),
        "shift_mu": unif((L, RW_SHIFT_WIDTH), 0.0, 1.0),
        "w0": unif((L, RW_WIDTH), -6.0, 1.0),
        "w_up": nrm((L, DECAY_LORA, RW_WIDTH), 0.5 * DECAY_LORA ** -0.5),
        "a0": nrm((L, RW_WIDTH), 0.5),
        "a_up": nrm((L, AAA_LORA, RW_WIDTH), AAA_LORA ** -0.5),
        "g_up": nrm((L, GATE_LORA, RW_WIDTH), GATE_LORA ** -0.5),
        "k_k": 0.85 + nrm((L, RW_WIDTH), 0.05),
        "k_a": 1.0 + nrm((L, RW_WIDTH), 0.05),
        "r_k": nrm((L, RW_HEADS, RW_HEAD_DIM), 0.1),
        "lnx_g": gain((L, RW_WIDTH)),
        "lnx_b": nrm((L, RW_WIDTH), 0.02),
        "proj_a": nrm((L, DA_V_WIDTH, D), DA_V_WIDTH ** -0.5),
        "proj_b": nrm((L, RW_WIDTH, D), RW_WIDTH ** -0.5),
        "w_out": nrm((L, D, D), D ** -0.5),
        "norm2_g": gain((L, D)),
        "router_g": nrm((L, D, N_GROUPS), D ** -0.5),
        "router_g_b": nrm((L, N_GROUPS), 0.01),
        "router_e": nrm((L, D, N_EXPERTS), D ** -0.5),
        "router_e_b": nrm((L, N_EXPERTS), 0.01),
        "w_gate_e": nrm((L, N_EXPERTS, D, D_EXPERT), D ** -0.5),
        "w_up_e": nrm((L, N_EXPERTS, D, D_EXPERT), D ** -0.5),
        "w_down_e": nrm((L, N_EXPERTS, D_EXPERT, D), D_EXPERT ** -0.5),
    }


def reference(x, norm1_g, w_in, q_norm_g, k_norm_g, lam_q1, lam_k1, lam_q2, lam_k2, subln_g,
              shift_mu, w0, w_up, a0, a_up, g_up, k_k, k_a, r_k, lnx_g, lnx_b,
              proj_a, proj_b, w_out, norm2_g, router_g, router_g_b, router_e, router_e_b,
              w_gate_e, w_up_e, w_down_e):
    B, S, D = x.shape
    cos, sin = rope_tables(S)
    for l in range(DEPTH):
        lam_init = 0.8 - 0.6 * math.exp(-0.3 * l)
        h = rms_norm(x, norm1_g[l], NORM_EPS)
        p = h @ w_in[l]
        qa, ka, va, rw, ga, gb = jnp.split(p, IN_SPLIT_POINTS, axis=-1)
        ya = diff_attention(qa.reshape(B, S, DA_HEADS, 2, DA_HEAD_DIM),
                            ka.reshape(B, S, DA_HEADS, 2, DA_HEAD_DIM),
                            va.reshape(B, S, DA_HEADS, DA_V_DIM),
                            q_norm_g[l], k_norm_g[l], lam_q1[l], lam_k1[l], lam_q2[l], lam_k2[l],
                            subln_g[l], lam_init, cos, sin)
        yb = rwkv7_time_mix(token_shift(rw, shift_mu[l]), w0[l], w_up[l], a0[l], a_up[l], g_up[l],
                            k_k[l], k_a[l], r_k[l], lnx_g[l], lnx_b[l])
        merged = jax.nn.sigmoid(ga) * (ya @ proj_a[l]) + jax.nn.sigmoid(gb) * (yb @ proj_b[l])
        x = x + merged @ w_out[l]
        h2 = rms_norm(x, norm2_g[l], NORM_EPS).reshape(B * S, D)
        x = x + hier_moe(h2, router_g[l], router_g_b[l], router_e[l], router_e_b[l],
                         w_gate_e[l], w_up_e[l], w_down_e[l]).reshape(B, S, D)
    return x
```

```python
import functools
import math

import jax
import jax.numpy as jnp
from jax import lax
from jax.experimental import pallas as pl
from jax.experimental.pallas import tpu as pltpu

DA_HEADS = 8
DA_HEAD_DIM = 64
DA_V_DIM = 2 * DA_HEAD_DIM
ROT_DIM = DA_HEAD_DIM // 4
ROPE_THETA = 500000.0
SUBLN_EPS = 1e-5
RW_HEADS = 16
RW_HEAD_DIM = 64
RW_WIDTH = RW_HEADS * RW_HEAD_DIM
DECAY_LORA = 96
AAA_LORA = 96
GATE_LORA = 256
GN_EPS = 64e-5
N_GROUPS = 4
EXPERTS_PER_GROUP = 8
N_EXPERTS = N_GROUPS * EXPERTS_PER_GROUP
TOP_K = 2
NORM_EPS = 1e-6

LANES = 128
SUBLANES = 8
VMEM_LIMIT_BYTES = 56 * 1024 * 1024

CHUNK = 64

F32 = jnp.float32
BF16 = jnp.bfloat16

NT_DIMS = (((1,), (1,)), ((), ()))
TN_DIMS = (((0,), (0,)), ((), ()))


def _cparams(*sem):
    return pltpu.CompilerParams(dimension_semantics=tuple(sem), vmem_limit_bytes=VMEM_LIMIT_BYTES)


def _dot(a, b, dims=None, precision=None):
    if dims is None:
        return jnp.dot(a, b, preferred_element_type=F32, precision=precision)
    return lax.dot_general(a, b, dims, preferred_element_type=F32, precision=precision)


def _split3(x):
    h = x.astype(BF16)
    r = x - h.astype(F32)
    m = r.astype(BF16)
    l = (r - m.astype(F32)).astype(BF16)
    return h, m, l


def _dot_exact_rhs(sel_bf16, x_f32):
    h, m, l = _split3(x_f32)
    return _dot(sel_bf16, h) + _dot(sel_bf16, m) + _dot(sel_bf16, l)


def _group_sum64(x, bd):
    h, m, l = _split3(x)
    outs = []
    for s in range(x.shape[1] // LANES):
        sl = slice(s * LANES, (s + 1) * LANES)
        outs.append(_dot(h[:, sl], bd) + _dot(m[:, sl], bd) + _dot(l[:, sl], bd))
    return outs[0] if len(outs) == 1 else jnp.concatenate(outs, axis=1)


def _block_diag_ones(n, blk, dtype=BF16):
    r = lax.broadcasted_iota(jnp.int32, (n, n), 0) // blk
    c = lax.broadcasted_iota(jnp.int32, (n, n), 1) // blk
    return jnp.where(r == c, 1.0, 0.0).astype(dtype)


def _rmsnorm_kernel(x_ref, g_ref, o_ref, *, eps):
    x = x_ref[...]
    ms = jnp.mean(x * x, axis=-1, keepdims=True)
    o_ref[...] = (x * lax.rsqrt(ms + eps) * g_ref[...]).astype(o_ref.dtype)


def _rmsnorm(x, g, eps, tm):
    t, d = x.shape
    return pl.pallas_call(
        functools.partial(_rmsnorm_kernel, eps=eps),
        out_shape=jax.ShapeDtypeStruct((t, d), BF16),
        grid=(t // tm,),
        in_specs=[pl.BlockSpec((tm, d), lambda i: (i, 0)),
                  pl.BlockSpec((1, d), lambda i: (0, 0))],
        out_specs=pl.BlockSpec((tm, d), lambda i: (i, 0)),
        compiler_params=_cparams("parallel"),
        name="rmsnorm",
    )(x, g.reshape(1, d))


def _matmul_kernel(a_ref, w_ref, o_ref):
    o_ref[...] = _dot(a_ref[...], w_ref[...]).astype(o_ref.dtype)


def _matmul(a, w, out_dtype, tm, tn, name):
    t, k = a.shape
    n = w.shape[1]
    return pl.pallas_call(
        _matmul_kernel,
        out_shape=jax.ShapeDtypeStruct((t, n), out_dtype),
        grid=(t // tm, n // tn),
        in_specs=[pl.BlockSpec((tm, k), lambda i, j: (i, 0)),
                  pl.BlockSpec((k, tn), lambda i, j: (0, j))],
        out_specs=pl.BlockSpec((tm, tn), lambda i, j: (i, j)),
        compiler_params=_cparams("parallel", "parallel"),
        name=name,
    )(a, w)


def _qk_proj_kernel(a_ref, w_ref, gain_ref, cos_ref, sinm_ref, sinp_ref, o_ref, *, tn):
    acc = _dot(a_ref[...], w_ref[...])
    bd = _block_diag_ones(LANES, DA_HEAD_DIM)
    ms = _group_sum64(acc * acc, bd) * (1.0 / DA_HEAD_DIM)
    xn = acc * lax.rsqrt(ms + NORM_EPS) * gain_ref[...]
    reps = tn // LANES
    cos = jnp.tile(cos_ref[...], (1, reps))
    sinm = jnp.tile(sinm_ref[...], (1, reps))
    sinp = jnp.tile(sinp_ref[...], (1, reps))
    half = ROT_DIM // 2
    hi = pltpu.roll(xn, tn - half, 1)
    lo = pltpu.roll(xn, half, 1)
    o_ref[...] = (xn * cos + hi * sinm + lo * sinp).astype(o_ref.dtype)


def _qk_proj(h, w_qk, gain_row, cos_t, sinm_t, sinp_t, seq, tm, tn):
    t, k = h.shape
    n = w_qk.shape[1]
    nseq = seq // tm
    return pl.pallas_call(
        functools.partial(_qk_proj_kernel, tn=tn),
        out_shape=jax.ShapeDtypeStruct((t, n), BF16),
        grid=(t // tm, n // tn),
        in_specs=[pl.BlockSpec((tm, k), lambda i, j: (i, 0)),
                  pl.BlockSpec((k, tn), lambda i, j: (0, j)),
                  pl.BlockSpec((1, tn), lambda i, j: (0, j)),
                  pl.BlockSpec((tm, LANES), lambda i, j: (i % nseq, 0)),
                  pl.BlockSpec((tm, LANES), lambda i, j: (i % nseq, 0)),
                  pl.BlockSpec((tm, LANES), lambda i, j: (i % nseq, 0))],
        out_specs=pl.BlockSpec((tm, tn), lambda i, j: (i, j)),
        compiler_params=_cparams("parallel", "parallel"),
        name="qk_proj",
    )(h, w_qk, gain_row, cos_t, sinm_t, sinp_t)


def _rope_tables(seq):
    half = ROT_DIM // 2
    inv_freq = ROPE_THETA ** (-jnp.arange(0, ROT_DIM, 2, dtype=F32) / ROT_DIM)
    ang = jnp.arange(seq, dtype=F32)[:, None] * inv_freq[None, :]
    cos, sin = jnp.cos(ang), jnp.sin(ang)
    ones = jnp.ones((seq, DA_HEAD_DIM - ROT_DIM), F32)
    zeros = jnp.zeros((seq, DA_HEAD_DIM - ROT_DIM), F32)
    zh = jnp.zeros((seq, half), F32)
    cos64 = jnp.concatenate([cos, cos, ones], axis=1)
    sinm64 = jnp.concatenate([-sin, zh, zeros], axis=1)
    sinp64 = jnp.concatenate([zh, sin, zeros], axis=1)
    return tuple(jnp.concatenate([a, a], axis=1) for a in (cos64, sinm64, sinp64))


def _diff_attn_kernel(lam_ref, q_ref, k_ref, v_ref, g_ref, o_ref,
                      m1_ref, l1_ref, a1_ref, m2_ref, l2_ref, a2_ref, *, tq, tk, lam_init):
    i = pl.program_id(2)
    lane = lax.broadcasted_iota(jnp.int32, (1, DA_V_DIM), 1)
    q = q_ref[...]
    zero = jnp.zeros_like(q)
    q1 = jnp.where(lane < DA_HEAD_DIM, q, zero)
    q2 = jnp.where(lane >= DA_HEAD_DIM, q, zero)

    for m_ref, l_ref, a_ref in ((m1_ref, l1_ref, a1_ref), (m2_ref, l2_ref, a2_ref)):
        m_ref[...] = jnp.full(m_ref.shape, -jnp.inf, F32)
        l_ref[...] = jnp.zeros(l_ref.shape, F32)
        a_ref[...] = jnp.zeros(a_ref.shape, F32)

    def update(qc, kj, vj, mask, m_ref, l_ref, a_ref):
        s = _dot(qc, kj, NT_DIMS)
        if mask is not None:
            s = jnp.where(mask, s, -jnp.inf)
        m_old = m_ref[...]
        m_new = jnp.maximum(m_old, jnp.max(s, axis=-1, keepdims=True))
        alpha = jnp.exp(m_old - m_new)
        p = jnp.exp(s - m_new)
        l_ref[...] = alpha * l_ref[...] + jnp.sum(p, axis=-1, keepdims=True)
        a_ref[...] = alpha * a_ref[...] + _dot(p.astype(BF16), vj)
        m_ref[...] = m_new

    def step(j, mask):
        off = pl.multiple_of(j * tk, tk)
        kj = k_ref[pl.ds(off, tk), :]
        vj = v_ref[pl.ds(off, tk), :]
        update(q1, kj, vj, mask, m1_ref, l1_ref, a1_ref)
        update(q2, kj, vj, mask, m2_ref, l2_ref, a2_ref)

    n_full = (i * tq) // tk

    def full_body(j, c):
        step(j, None)
        return c

    lax.fori_loop(0, n_full, full_body, 0)

    row = i * tq + lax.broadcasted_iota(jnp.int32, (tq, tk), 0)
    for d in range(tq // tk):
        j = n_full + d
        col = j * tk + lax.broadcasted_iota(jnp.int32, (tq, tk), 1)
        step(j, col <= row)

    lq1, lk1, lq2, lk2 = (lam_ref[r:r + 1, :] for r in range(4))
    lam = (jnp.exp(jnp.sum(lq1 * lk1, axis=-1, keepdims=True))
           - jnp.exp(jnp.sum(lq2 * lk2, axis=-1, keepdims=True)) + lam_init)
    o = a1_ref[...] / l1_ref[...] - lam * (a2_ref[...] / l2_ref[...])
    ms = jnp.mean(o * o, axis=-1, keepdims=True)
    o_ref[...] = (o * lax.rsqrt(ms + SUBLN_EPS) * (g_ref[...] * (1.0 - lam_init))).astype(o_ref.dtype)


def _diff_attention(qk, v, lam_params, subln_g, batch, seq, lam_init, tq, tk):
    t = qk.shape[0]
    nq = seq // tq
    kern = functools.partial(_diff_attn_kernel, tq=tq, tk=tk, lam_init=lam_init)
    return pl.pallas_call(
        kern,
        out_shape=jax.ShapeDtypeStruct((t, DA_HEADS * DA_V_DIM), BF16),
        grid=(batch, DA_HEADS, nq),
        in_specs=[pl.BlockSpec((4, DA_HEAD_DIM), lambda b, h, i: (0, 0)),
                  pl.BlockSpec((tq, DA_V_DIM), lambda b, h, i: (b * nq + i, h)),
                  pl.BlockSpec((seq, DA_V_DIM), lambda b, h, i: (b, DA_HEADS + h)),
                  pl.BlockSpec((seq, DA_V_DIM), lambda b, h, i: (b, h)),
                  pl.BlockSpec((1, DA_V_DIM), lambda b, h, i: (0, 0))],
        out_specs=pl.BlockSpec((tq, DA_V_DIM), lambda b, h, i: (b * nq + i, h)),
        scratch_shapes=[pltpu.VMEM((tq, 1), F32), pltpu.VMEM((tq, 1), F32), pltpu.VMEM((tq, DA_V_DIM), F32),
                        pltpu.VMEM((tq, 1), F32), pltpu.VMEM((tq, 1), F32), pltpu.VMEM((tq, DA_V_DIM), F32)],
        compiler_params=_cparams("parallel", "parallel", "parallel"),
        name="diff_attention",
    )(lam_params, qk, qk, v, subln_g.reshape(1, DA_V_DIM))


DECAY_SCALE = math.exp(-0.5)


def _rwkv_prep_kernel(r_ref, k_ref, v_ref, lo_ref, pr_ref, pk_ref, pv_ref, plo_ref,
                      mur_ref, muk_ref, muv_ref, mulo_ref, par_ref, wup_ref, aup_ref, gup_ref,
                      at_ref, rt_ref, bt_ref, kt_ref, bg_ref, kg_ref, vb_ref, bonus_ref, g_ref, gam_ref,
                      *, tm, nseq):
    i = pl.program_id(0)
    seq_start = (i % nseq) == 0
    row0 = lax.broadcasted_iota(jnp.int32, (tm, 1), 0) == 0

    def shifted(x_ref, p_ref, mu_ref):
        x = x_ref[...]
        last = jnp.where(seq_start, 0.0, p_ref[SUBLANES - 1:SUBLANES, :])
        prev = jnp.where(row0, last, pltpu.roll(x, 1, 0))
        return x + (prev - x) * mu_ref[...]

    r = shifted(r_ref, pr_ref, mur_ref)
    k = shifted(k_ref, pk_ref, muk_ref)
    v = shifted(v_ref, pv_ref, muv_ref)
    lo = shifted(lo_ref, plo_ref, mulo_ref)
    dw, da, dg = lo[:, :LANES], lo[:, LANES:2 * LANES], lo[:, 2 * LANES:]
    w0, a0, k_k, k_a, r_k = (par_ref[j:j + 1, :] for j in range(5))

    u = w0 + _dot(jnp.tanh(dw).astype(BF16), wup_ref[...])
    ld = -DECAY_SCALE * jax.nn.sigmoid(u)
    a = jax.nn.sigmoid(a0 + _dot(da.astype(BF16), aup_ref[...]))
    g = _dot(jax.nn.sigmoid(dg).astype(BF16), gup_ref[...])

    bd = _block_diag_ones(LANES, RW_HEAD_DIM)
    kk = k * k_k
    kk = kk / jnp.maximum(jnp.sqrt(_group_sum64(kk * kk, bd)), 1e-12)
    k2 = k * (1.0 + (a - 1.0) * k_a)
    bonus = _group_sum64(r * k2 * r_k, bd) * v

    t_i = lax.broadcasted_iota(jnp.int32, (tm, tm), 0)
    s_i = lax.broadcasted_iota(jnp.int32, (tm, tm), 1)
    same = (t_i // CHUNK) == (s_i // CHUNK)
    tri = jnp.where(same & (s_i <= t_i), 1.0, 0.0).astype(BF16)
    rest = jnp.where(same & (s_i > t_i), 1.0, 0.0).astype(BF16)
    c_i = lax.broadcasted_iota(jnp.int32, (tm // CHUNK, tm), 0)
    cs_i = lax.broadcasted_iota(jnp.int32, (tm // CHUNK, tm), 1)
    whole = jnp.where(cs_i // CHUNK == c_i, 1.0, 0.0).astype(BF16)

    ldh, ldm, ldl = _split3(ld)

    def sel(m):
        return _dot(m, ldh) + _dot(m, ldm) + _dot(m, ldl)

    cum = sel(tri)
    e_neg = jnp.exp(-cum)
    e_rem = jnp.exp(sel(rest))
    b = kk * a
    at_ref[...] = (-kk * jnp.exp(cum - ld)).astype(BF16)
    rt_ref[...] = (r * jnp.exp(cum)).astype(BF16)
    bt_ref[...] = (b * e_neg).astype(BF16)
    kt_ref[...] = (k2 * e_neg).astype(BF16)
    bg_ref[...] = (b * e_rem).astype(BF16)
    kg_ref[...] = (k2 * e_rem).astype(BF16)
    vb_ref[...] = v.astype(BF16)
    bonus_ref[...] = bonus.astype(BF16)
    g_ref[...] = g.astype(BF16)
    gam_ref[0] = jnp.exp(sel(whole))


def _rwkv_prep(rw, mu, params, wup, aup, gup, seq, tm, tn):
    t = rw.shape[0]
    nj = RW_WIDTH // tn
    lo_w = 4 * LANES
    lo_blk = (3 * RW_WIDTH) // lo_w
    nseq = seq // tm
    rpb = tm // SUBLANES

    def cur(off):
        return pl.BlockSpec((tm, tn), lambda i, j: (i, off * nj + j))

    def prv(off):
        return pl.BlockSpec((SUBLANES, tn), lambda i, j: (jnp.maximum(i * rpb - 1, 0), off * nj + j))

    def row(off):
        return pl.BlockSpec((1, tn), lambda i, j: (0, off * nj + j))

    in_specs = [cur(0), cur(1), cur(2), pl.BlockSpec((tm, lo_w), lambda i, j: (i, lo_blk)),
                prv(0), prv(1), prv(2),
                pl.BlockSpec((SUBLANES, lo_w), lambda i, j: (jnp.maximum(i * rpb - 1, 0), lo_blk)),
                row(0), row(1), row(2), pl.BlockSpec((1, lo_w), lambda i, j: (0, lo_blk)),
                pl.BlockSpec((SUBLANES, tn), lambda i, j: (0, j)),
                pl.BlockSpec((LANES, tn), lambda i, j: (0, j)),
                pl.BlockSpec((LANES, tn), lambda i, j: (0, j)),
                pl.BlockSpec((2 * LANES, tn), lambda i, j: (0, j))]
    out_blk = pl.BlockSpec((tm, tn), lambda i, j: (i, j))
    outs = [jax.ShapeDtypeStruct((t, RW_WIDTH), BF16)] * 9
    outs.append(jax.ShapeDtypeStruct((t // tm, tm // CHUNK, RW_WIDTH), F32))
    out_specs = [out_blk] * 9 + [pl.BlockSpec((1, tm // CHUNK, tn), lambda i, j: (i, 0, j))]
    return pl.pallas_call(
        functools.partial(_rwkv_prep_kernel, tm=tm, nseq=nseq),
        out_shape=outs,
        grid=(t // tm, nj),
        in_specs=in_specs,
        out_specs=out_specs,
        compiler_params=_cparams("parallel", "parallel"),
        name="rwkv_prep",
    )(rw, rw, rw, rw, rw, rw, rw, rw, mu, mu, mu, mu, params, wup, aup, gup)


PAIR = 2 * RW_HEAD_DIM
HI = lax.Precision.HIGHEST


def _rwkv_chunk_kernel(at_ref, rt_ref, bt_ref, kt_ref, bg_ref, kg_ref, v_ref, bonus_ref, g_ref,
                       gam_ref, lng_ref, lnb_ref, o_ref, s_ref):
    c = pl.program_id(1)

    @pl.when(c == 0)
    def _():
        s_ref[...] = jnp.zeros(s_ref.shape, F32)

    lane = lax.broadcasted_iota(jnp.int32, (1, PAIR), 1)
    first = lane < RW_HEAD_DIM
    rho = lax.broadcasted_iota(jnp.int32, (PAIR, PAIR), 0)
    sig = lax.broadcasted_iota(jnp.int32, (PAIR, PAIR), 1)
    strict, incl, eye = sig < rho, sig <= rho, sig == rho
    own = (rho // RW_HEAD_DIM) == (sig // RW_HEAD_DIM)

    def stacked(x):
        z = jnp.zeros_like(x)
        return jnp.concatenate([jnp.where(first, x, z), jnp.where(first, z, x)], axis=0)

    for p in range(RW_HEADS // 2):
        sl = slice(p * PAIR, (p + 1) * PAIR)
        pa, pr, pb, pk, pbg, pkg, pv = (stacked(ref[:, sl]) for ref in
                                        (at_ref, rt_ref, bt_ref, kt_ref, bg_ref, kg_ref, v_ref))
        prods = _dot(jnp.concatenate([pa, pr], axis=0), jnp.concatenate([pb, pk], axis=0), NT_DIMS)
        zero = jnp.zeros((PAIR, PAIR), F32)
        lmat = jnp.where(strict, prods[:PAIR, :PAIR], zero)
        sak = jnp.where(strict, prods[:PAIR, PAIR:], zero)
        lrb = jnp.where(incl, prods[PAIR:, :PAIR], zero)
        lrk = jnp.where(incl, prods[PAIR:, PAIR:], zero)

        x = jnp.concatenate([pa.astype(F32), _dot(sak.astype(BF16), pv)], axis=1)
        li = lmat
        n_steps = CHUNK.bit_length() - 1
        for it in range(n_steps):
            if it + 1 < n_steps:
                res = _dot(li, jnp.concatenate([li, x], axis=1), precision=HI)
                x = x + res[:, PAIR:]
                li = res[:, :PAIR]
            else:
                x = x + _dot(li, x, precision=HI)
        gmat = jnp.concatenate([x, jnp.concatenate([zero, pv.astype(F32)], axis=1)], axis=0).astype(BF16)
        out1 = _dot(jnp.concatenate([lrb, lrk], axis=1).astype(BF16), gmat)
        out2 = _dot(gmat, jnp.concatenate([pbg, pkg], axis=0), TN_DIMS)
        qe = out1[:, :PAIR] + pr.astype(F32)
        mmat = out2[:PAIR] + jnp.where(eye, gam_ref[0][:, sl], 0.0)
        nmat = out2[PAIR:]

        s0 = s_ref[p]
        y = _dot(qe, s0, NT_DIMS, precision=HI) + out1[:, PAIR:]
        s_ref[p] = _dot(s0, mmat, precision=HI) + nmat

        mean = jnp.sum(y, axis=-1, keepdims=True) * (1.0 / RW_HEAD_DIM)
        d = jnp.where(own, y - mean, 0.0)
        var = jnp.sum(d * d, axis=-1, keepdims=True) * (1.0 / RW_HEAD_DIM)
        yn = d * lax.rsqrt(var + GN_EPS)
        yn = yn[:CHUNK] + yn[CHUNK:]
        out = (yn * lng_ref[:, sl] + lnb_ref[:, sl] + bonus_ref[:, sl].astype(F32)) * g_ref[:, sl].astype(F32)
        o_ref[:, sl] = out.astype(o_ref.dtype)


def _rwkv_chunks(prep, lnx_g, lnx_b, batch, seq):
    at, rt, bt, kt, bg, kg, vb, bonus, g, gam = prep
    t = at.shape[0]
    nc = seq // CHUNK
    gam = gam.reshape(t // CHUNK, 1, RW_WIDTH)
    blk = pl.BlockSpec((CHUNK, RW_WIDTH), lambda b, c: (b * nc + c, 0))
    rowspec = pl.BlockSpec((1, RW_WIDTH), lambda b, c: (0, 0))
    return pl.pallas_call(
        _rwkv_chunk_kernel,
        out_shape=jax.ShapeDtypeStruct((t, RW_WIDTH), BF16),
        grid=(batch, nc),
        in_specs=[blk] * 9 + [pl.BlockSpec((1, 1, RW_WIDTH), lambda b, c: (b * nc + c, 0, 0)), rowspec, rowspec],
        out_specs=blk,
        scratch_shapes=[pltpu.VMEM((RW_HEADS // 2, PAIR, PAIR), F32)],
        compiler_params=_cparams("arbitrary", "arbitrary"),
        name="rwkv_chunks",
    )(at, rt, bt, kt, bg, kg, vb, bonus, g, gam, lnx_g.reshape(1, RW_WIDTH), lnx_b.reshape(1, RW_WIDTH))


def _merge_kernel(h_ref, ya_ref, yb_ref, wga_ref, wgb_ref, pa_ref, pb_ref, o_ref):
    h = h_ref[...]
    ga = jax.nn.sigmoid(_dot(h, wga_ref[...]))
    gb = jax.nn.sigmoid(_dot(h, wgb_ref[...]))
    o_ref[...] = (ga * _dot(ya_ref[...], pa_ref[...]) + gb * _dot(yb_ref[...], pb_ref[...])).astype(o_ref.dtype)


def _merge(h, ya, yb, wga, wgb, pa, pb, tm, tn):
    t, d = h.shape
    n = wga.shape[1]
    ka, kb = ya.shape[1], yb.shape[1]
    return pl.pallas_call(
        _merge_kernel,
        out_shape=jax.ShapeDtypeStruct((t, n), BF16),
        grid=(t // tm, n // tn),
        in_specs=[pl.BlockSpec((tm, d), lambda i, j: (i, 0)),
                  pl.BlockSpec((tm, ka), lambda i, j: (i, 0)),
                  pl.BlockSpec((tm, kb), lambda i, j: (i, 0)),
                  pl.BlockSpec((d, tn), lambda i, j: (0, j)),
                  pl.BlockSpec((d, tn), lambda i, j: (0, j)),
                  pl.BlockSpec((ka, tn), lambda i, j: (0, j)),
                  pl.BlockSpec((kb, tn), lambda i, j: (0, j))],
        out_specs=pl.BlockSpec((tm, tn), lambda i, j: (i, j)),
        compiler_params=_cparams("parallel", "parallel"),
        name="gated_merge",
    )(h, ya, yb, wga, wgb, pa, pb)


def _outproj_kernel(m_ref, x_ref, wo_ref, g2_ref, wrh_ref, wrl_ref, x1_ref, h2_ref, lg_ref):
    x1 = x_ref[...] + _dot(m_ref[...], wo_ref[...])
    x1_ref[...] = x1
    ms = jnp.mean(x1 * x1, axis=-1, keepdims=True)
    h2 = x1 * lax.rsqrt(ms + NORM_EPS) * g2_ref[...]
    h2_ref[...] = h2
    hh = h2.astype(BF16)
    hl = (h2 - hh.astype(F32)).astype(BF16)
    wrh = wrh_ref[...]
    lg_ref[...] = _dot(hh, wrh) + _dot(hl, wrh) + _dot(hh, wrl_ref[...])


def _outproj(merged, x, wo, g2, wr_hi, wr_lo, tm):
    t, d = x.shape
    nr = wr_hi.shape[1]
    return pl.pallas_call(
        _outproj_kernel,
        out_shape=[jax.ShapeDtypeStruct((t, d), F32), jax.ShapeDtypeStruct((t, d), F32),
                   jax.ShapeDtypeStruct((t, nr), F32)],
        grid=(t // tm,),
        in_specs=[pl.BlockSpec((tm, d), lambda i: (i, 0)),
                  pl.BlockSpec((tm, d), lambda i: (i, 0)),
                  pl.BlockSpec((d, d), lambda i: (0, 0)),
                  pl.BlockSpec((1, d), lambda i: (0, 0)),
                  pl.BlockSpec((d, nr), lambda i: (0, 0)),
                  pl.BlockSpec((d, nr), lambda i: (0, 0))],
        out_specs=[pl.BlockSpec((tm, d), lambda i: (i, 0)),
                   pl.BlockSpec((tm, d), lambda i: (i, 0)),
                   pl.BlockSpec((tm, nr), lambda i: (i, 0))],
        compiler_params=_cparams("parallel"),
        name="outproj_norm_router",
    )(merged, x, wo, g2.reshape(1, d), wr_hi, wr_lo)


def _moe_kernel(be_ref, nused_ref, tok_hbm, dst_hbm, h_hbm, wg_ref, wu_ref, wd_ref, y_hbm,
                xbuf, obuf, tok_s, dst_s, idx_sem, g_sem, s_sem, *, rows, n_blocks):
    b = pl.program_id(0)
    slot = b % 2
    n_used = nused_ref[0]

    def idx_copies(blk, sl):
        return (pltpu.make_async_copy(tok_hbm.at[blk], tok_s.at[sl], idx_sem.at[0]),
                pltpu.make_async_copy(dst_hbm.at[blk], dst_s.at[sl], idx_sem.at[1]))

    def gather_copy(r, sl):
        return pltpu.make_async_copy(h_hbm.at[tok_s[sl, r]], xbuf.at[sl, r], g_sem.at[sl])

    def scatter_copy(r, sl):
        return pltpu.make_async_copy(obuf.at[r], y_hbm.at[dst_s[sl, r]], s_sem.at[0])

    def fetch(blk, sl):
        for cp in idx_copies(blk, sl):
            cp.start()
        for cp in idx_copies(blk, sl):
            cp.wait()

        def issue(r, c):
            gather_copy(r, sl).start()
            return c

        lax.fori_loop(0, rows, issue, 0, unroll=8)

    @pl.when((b == 0) & (n_used > 0))
    def _():
        fetch(0, 0)

    @pl.when(b + 1 < n_used)
    def _():
        fetch(b + 1, 1 - slot)

    @pl.when(b < n_used)
    def _():
        def wait_g(r, c):
            gather_copy(r, slot).wait()
            return c

        lax.fori_loop(0, rows, wait_g, 0, unroll=8)
        x = xbuf[slot].astype(BF16)
        gate = _dot(x, wg_ref[...])
        up = _dot(x, wu_ref[...])
        mid = (gate * jax.nn.sigmoid(gate) * up).astype(BF16)
        obuf[...] = _dot(mid, wd_ref[...]).astype(obuf.dtype)

        def issue_s(r, c):
            @pl.when(dst_s[slot, r] >= 0)
            def _():
                scatter_copy(r, slot).start()
            return c

        def wait_s(r, c):
            @pl.when(dst_s[slot, r] >= 0)
            def _():
                scatter_copy(r, slot).wait()
            return c

        lax.fori_loop(0, rows, issue_s, 0, unroll=8)
        lax.fori_loop(0, rows, wait_s, 0, unroll=8)


def _moe(h2, row_tok, row_dst, block_e, n_used, wg, wu, wd, rows):
    t, d = h2.shape
    n_blocks = row_tok.shape[0]
    a = t * TOP_K
    f = wg.shape[2]
    kern = functools.partial(_moe_kernel, rows=rows, n_blocks=n_blocks)
    grid_spec = pltpu.PrefetchScalarGridSpec(
        num_scalar_prefetch=2,
        grid=(n_blocks,),
        in_specs=[pl.BlockSpec(memory_space=pl.ANY),
                  pl.BlockSpec(memory_space=pl.ANY),
                  pl.BlockSpec(memory_space=pl.ANY),
                  pl.BlockSpec((None, d, f), lambda b, be, nu: (be[b], 0, 0)),
                  pl.BlockSpec((None, d, f), lambda b, be, nu: (be[b], 0, 0)),
                  pl.BlockSpec((None, f, d), lambda b, be, nu: (be[b], 0, 0))],
        out_specs=pl.BlockSpec(memory_space=pl.ANY),
        scratch_shapes=[pltpu.VMEM((2, rows, d), F32),
                        pltpu.VMEM((rows, d), F32),
                        pltpu.SMEM((2, rows), jnp.int32),
                        pltpu.SMEM((2, rows), jnp.int32),
                        pltpu.SemaphoreType.DMA((2,)),
                        pltpu.SemaphoreType.DMA((2,)),
                        pltpu.SemaphoreType.DMA((1,))],
    )
    return pl.pallas_call(
        kern,
        out_shape=jax.ShapeDtypeStruct((a, d), F32),
        grid_spec=grid_spec,
        compiler_params=_cparams("arbitrary"),
        name="moe_experts",
    )(block_e, n_used, row_tok, row_dst, h2, wg, wu, wd)


def _combine_kernel(x1_ref, y_ref, gates_ref, o_ref, *, d):
    y = y_ref[...]
    gts = gates_ref[...]
    o_ref[...] = (x1_ref[...] + gts[:, 0:1] * y[:, :d].astype(F32) + gts[:, 1:2] * y[:, d:].astype(F32))


def _combine(x1, y_pairs, gates, tm):
    t, d = x1.shape
    return pl.pallas_call(
        functools.partial(_combine_kernel, d=d),
        out_shape=jax.ShapeDtypeStruct((t, d), F32),
        grid=(t // tm,),
        in_specs=[pl.BlockSpec((tm, d), lambda i: (i, 0)),
                  pl.BlockSpec((tm, TOP_K * d), lambda i: (i, 0)),
                  pl.BlockSpec((tm, TOP_K), lambda i: (i, 0))],
        out_specs=pl.BlockSpec((tm, d), lambda i: (i, 0)),
        compiler_params=_cparams("parallel"),
        name="moe_combine",
    )(x1, y_pairs, gates)


def _routing(logits, router_g_b, router_e_b, rows):
    t = logits.shape[0]
    a = t * TOP_K
    lg = logits[:, :N_GROUPS] + router_g_b
    le = logits[:, N_GROUPS:N_GROUPS + N_EXPERTS] + router_e_b
    p_group = jax.nn.softmax(lg, axis=-1)
    p_g_top, g_idx = lax.top_k(p_group, 1)
    e_sel = jnp.take_along_axis(le.reshape(t, N_GROUPS, EXPERTS_PER_GROUP), g_idx[:, :, None], axis=1)[:, 0]
    p_e_top, e_local = lax.top_k(jax.nn.softmax(e_sel, axis=-1), TOP_K)
    gates = p_g_top * p_e_top / jnp.sum(p_e_top, axis=-1, keepdims=True)
    expert = g_idx * EXPERTS_PER_GROUP + e_local

    flat_e = expert.reshape(a).astype(jnp.int32)
    order = jnp.argsort(flat_e).astype(jnp.int32)
    se = flat_e[order]
    counts = jnp.zeros((N_EXPERTS,), jnp.int32).at[flat_e].add(1)
    padded = (counts + rows - 1) // rows * rows
    start = jnp.cumsum(counts) - counts
    pend = jnp.cumsum(padded)
    pstart = pend - padded
    dest = pstart[se] + jnp.arange(a, dtype=jnp.int32) - start[se]
    n_rows = a + N_EXPERTS * rows
    n_blocks = n_rows // rows
    row_tok = jnp.zeros((n_rows,), jnp.int32).at[dest].set(order // TOP_K)
    row_dst = jnp.full((n_rows,), -1, jnp.int32).at[dest].set(order)
    block_e = jnp.minimum(jnp.searchsorted(pend, jnp.arange(n_blocks, dtype=jnp.int32) * rows, side='right'),
                          N_EXPERTS - 1).astype(jnp.int32)
    n_used = (pend[-1] // rows).astype(jnp.int32).reshape(1)
    return (gates.astype(F32), row_tok.reshape(n_blocks, rows), row_dst.reshape(n_blocks, rows), block_e, n_used)


def _tiles(t, seq):
    return dict(
        norm_tm=min(512, t),
        proj_tm=min(1024, seq), proj_tn=512,
        attn_tq=min(512, seq), attn_tk=min(512, seq),
        prep_tm=min(256, seq), prep_tn=256,
        merge_tm=min(1024, t), merge_tn=512,
        out_tm=min(256, t),
        moe_rows=256,
        comb_tm=min(512, t),
    )


def _pad_rows(w, n):
    return jnp.pad(w, ((0, n - w.shape[0]), (0, 0)))


def _pad_cols(w, n):
    return jnp.pad(w, ((0, 0), (0, n - w.shape[1])))


def kernel(x, norm1_g, w_in, q_norm_g, k_norm_g, lam_q1, lam_k1, lam_q2, lam_k2, subln_g, shift_mu, w0, w_up, a0, a_up, g_up, k_k, k_a, r_k, lnx_g, lnx_b, proj_a, proj_b, w_out, norm2_g, router_g, router_g_b, router_e, router_e_b, w_gate_e, w_up_e, w_down_e):
    batch, seq, d = x.shape
    t = batch * seq
    depth = norm1_g.shape[0]
    tl = _tiles(t, seq)
    qkw = DA_HEADS * 2 * DA_HEAD_DIM
    vw = DA_HEADS * DA_V_DIM
    c_q, c_k, c_v = 0, qkw, 2 * qkw
    c_rw = c_v + vw
    c_dw = c_rw + 3 * RW_WIDTH
    c_da = c_dw + DECAY_LORA
    c_dg = c_da + AAA_LORA
    c_ga = c_dg + GATE_LORA
    c_gb = c_ga + d
    cos_t, sinm_t, sinp_t = _rope_tables(seq)
    xf = x.reshape(t, d)

    for l in range(depth):
        lam_init = 0.8 - 0.6 * math.exp(-0.3 * l)
        wl = w_in[l]
        w_qk = wl[:, c_q:c_v].astype(BF16)
        w_v = wl[:, c_v:c_rw].astype(BF16)
        w_rw = jnp.concatenate([wl[:, c_rw:c_dw], _pad_cols(wl[:, c_dw:c_da], LANES),
                                _pad_cols(wl[:, c_da:c_dg], LANES), wl[:, c_dg:c_ga]], axis=1).astype(BF16)
        w_ga = wl[:, c_ga:c_gb].astype(BF16)
        w_gb = wl[:, c_gb:].astype(BF16)
        mu = shift_mu[l]
        o_dw = 3 * RW_WIDTH
        mu_l = jnp.concatenate([mu[:o_dw], jnp.pad(mu[o_dw:o_dw + DECAY_LORA], (0, LANES - DECAY_LORA)),
                                jnp.pad(mu[o_dw + DECAY_LORA:o_dw + DECAY_LORA + AAA_LORA], (0, LANES - AAA_LORA)),
                                mu[o_dw + DECAY_LORA + AAA_LORA:]]).reshape(1, -1)
        gain_row = jnp.concatenate([jnp.tile(q_norm_g[l], 2 * DA_HEADS) * (DA_HEAD_DIM ** -0.5),
                                    jnp.tile(k_norm_g[l], 2 * DA_HEADS)]).reshape(1, 2 * qkw)
        lam_params = jnp.stack([lam_q1[l], lam_k1[l], lam_q2[l], lam_k2[l]])
        zrow = jnp.zeros((RW_WIDTH,), F32)
        rw_params = jnp.stack([w0[l], a0[l], k_k[l], k_a[l], r_k[l].reshape(-1), zrow, zrow, zrow])

        h = _rmsnorm(xf, norm1_g[l], NORM_EPS, tl["norm_tm"])
        qk = _qk_proj(h, w_qk, gain_row, cos_t, sinm_t, sinp_t, seq, tl["proj_tm"], tl["proj_tn"])
        v = _matmul(h, w_v, BF16, tl["proj_tm"], tl["proj_tn"], "v_proj")
        ya = _diff_attention(qk, v, lam_params, subln_g[l], batch, seq, lam_init, tl["attn_tq"], tl["attn_tk"])

        rw = _matmul(h, w_rw, F32, tl["proj_tm"], tl["proj_tn"], "rw_proj")
        prep = _rwkv_prep(rw, mu_l, rw_params, _pad_rows(w_up[l], LANES).astype(BF16),
                          _pad_rows(a_up[l], LANES).astype(BF16), g_up[l].astype(BF16),
                          seq, tl["prep_tm"], tl["prep_tn"])
        yb = _rwkv_chunks(prep, lnx_g[l], lnx_b[l], batch, seq)

        merged = _merge(h, ya, yb, w_ga, w_gb, proj_a[l].astype(BF16), proj_b[l].astype(BF16),
                        tl["merge_tm"], tl["merge_tn"])
        wr = _pad_cols(jnp.concatenate([router_g[l], router_e[l]], axis=1), LANES)
        wr_hi = wr.astype(BF16)
        wr_lo = (wr - wr_hi.astype(F32)).astype(BF16)
        x1, h2, logits = _outproj(merged, xf, w_out[l].astype(BF16), norm2_g[l], wr_hi, wr_lo, tl["out_tm"])

        rows = tl["moe_rows"]
        gates, row_tok, row_dst, block_e, n_used = _routing(logits, router_g_b[l], router_e_b[l], rows)
        y = _moe(h2, row_tok, row_dst, block_e, n_used, w_gate_e[l].astype(BF16), w_up_e[l].astype(BF16),
                 w_down_e[l].astype(BF16), rows)
        xf = _combine(x1, y.reshape(t, TOP_K * d), gates, tl["comb_tm"])
    return xf.reshape(batch, seq, d)
```

```python
import functools
import math

import jax
import jax.numpy as jnp
from jax import lax
from jax.experimental import pallas as pl
from jax.experimental.pallas import tpu as pltpu

DA_HEADS = 8
DA_HEAD_DIM = 64
DA_V_DIM = 2 * DA_HEAD_DIM
ROT_DIM = DA_HEAD_DIM // 4
ROPE_THETA = 500000.0
SUBLN_EPS = 1e-5
RW_HEADS = 16
RW_HEAD_DIM = 64
RW_WIDTH = RW_HEADS * RW_HEAD_DIM
DECAY_LORA = 96
AAA_LORA = 96
GATE_LORA = 256
GN_EPS = 64e-5
N_GROUPS = 4
EXPERTS_PER_GROUP = 8
N_EXPERTS = N_GROUPS * EXPERTS_PER_GROUP
TOP_K = 2
NORM_EPS = 1e-6

LANES = 128
SUBLANES = 8
VMEM_LIMIT_BYTES = 56 * 1024 * 1024

CHUNK = 64

F32 = jnp.float32
BF16 = jnp.bfloat16

NT_DIMS = (((1,), (1,)), ((), ()))
TN_DIMS = (((0,), (0,)), ((), ()))


def _cparams(*sem):
    return pltpu.CompilerParams(dimension_semantics=tuple(sem), vmem_limit_bytes=VMEM_LIMIT_BYTES)


def _dot(a, b, dims=None, precision=None):
    if dims is None:
        return jnp.dot(a, b, preferred_element_type=F32, precision=precision)
    return lax.dot_general(a, b, dims, preferred_element_type=F32, precision=precision)


def _split3(x):
    h = x.astype(BF16)
    r = x - h.astype(F32)
    m = r.astype(BF16)
    l = (r - m.astype(F32)).astype(BF16)
    return h, m, l


def _dot_exact_rhs(sel_bf16, x_f32):
    h, m, l = _split3(x_f32)
    return _dot(sel_bf16, h) + _dot(sel_bf16, m) + _dot(sel_bf16, l)


def _group_sum64(x, bd):
    h, m, l = _split3(x)
    outs = []
    for s in range(x.shape[1] // LANES):
        sl = slice(s * LANES, (s + 1) * LANES)
        outs.append(_dot(h[:, sl], bd) + _dot(m[:, sl], bd) + _dot(l[:, sl], bd))
    return outs[0] if len(outs) == 1 else jnp.concatenate(outs, axis=1)


def _block_diag_ones(n, blk, dtype=BF16):
    r = lax.broadcasted_iota(jnp.int32, (n, n), 0) // blk
    c = lax.broadcasted_iota(jnp.int32, (n, n), 1) // blk
    return jnp.where(r == c, 1.0, 0.0).astype(dtype)


def _rmsnorm_kernel(x_ref, g_ref, o_ref, *, eps):
    x = x_ref[...]
    ms = jnp.mean(x * x, axis=-1, keepdims=True)
    o_ref[...] = (x * lax.rsqrt(ms + eps) * g_ref[...]).astype(o_ref.dtype)


def _rmsnorm(x, g, eps, tm):
    t, d = x.shape
    return pl.pallas_call(
        functools.partial(_rmsnorm_kernel, eps=eps),
        out_shape=jax.ShapeDtypeStruct((t, d), BF16),
        grid=(t // tm,),
        in_specs=[pl.BlockSpec((tm, d), lambda i: (i, 0)),
                  pl.BlockSpec((1, d), lambda i: (0, 0))],
        out_specs=pl.BlockSpec((tm, d), lambda i: (i, 0)),
        compiler_params=_cparams("parallel"),
        name="rmsnorm",
    )(x, g.reshape(1, d))


def _matmul_kernel(a_ref, w_ref, o_ref):
    o_ref[...] = _dot(a_ref[...], w_ref[...]).astype(o_ref.dtype)


def _matmul(a, w, out_dtype, tm, tn, name):
    t, k = a.shape
    n = w.shape[1]
    return pl.pallas_call(
        _matmul_kernel,
        out_shape=jax.ShapeDtypeStruct((t, n), out_dtype),
        grid=(t // tm, n // tn),
        in_specs=[pl.BlockSpec((tm, k), lambda i, j: (i, 0)),
                  pl.BlockSpec((k, tn), lambda i, j: (0, j))],
        out_specs=pl.BlockSpec((tm, tn), lambda i, j: (i, j)),
        compiler_params=_cparams("parallel", "parallel"),
        name=name,
    )(a, w)


def _qk_proj_kernel(a_ref, w_ref, gain_ref, cos_ref, sinm_ref, sinp_ref, o_ref, *, tn):
    acc = _dot(a_ref[...], w_ref[...])
    bd = _block_diag_ones(LANES, DA_HEAD_DIM)
    ms = _group_sum64(acc * acc, bd) * (1.0 / DA_HEAD_DIM)
    xn = acc * lax.rsqrt(ms + NORM_EPS) * gain_ref[...]
    reps = tn // LANES
    cos = jnp.tile(cos_ref[...], (1, reps))
    sinm = jnp.tile(sinm_ref[...], (1, reps))
    sinp = jnp.tile(sinp_ref[...], (1, reps))
    half = ROT_DIM // 2
    hi = pltpu.roll(xn, tn - half, 1)
    lo = pltpu.roll(xn, half, 1)
    o_ref[...] = (xn * cos + hi * sinm + lo * sinp).astype(o_ref.dtype)


def _qk_proj(h, w_qk, gain_row, cos_t, sinm_t, sinp_t, seq, tm, tn):
    t, k = h.shape
    n = w_qk.shape[1]
    nseq = seq // tm
    return pl.pallas_call(
        functools.partial(_qk_proj_kernel, tn=tn),
        out_shape=jax.ShapeDtypeStruct((t, n), BF16),
        grid=(t // tm, n // tn),
        in_specs=[pl.BlockSpec((tm, k), lambda i, j: (i, 0)),
                  pl.BlockSpec((k, tn), lambda i, j: (0, j)),
                  pl.BlockSpec((1, tn), lambda i, j: (0, j)),
                  pl.BlockSpec((tm, LANES), lambda i, j: (i % nseq, 0)),
                  pl.BlockSpec((tm, LANES), lambda i, j: (i % nseq, 0)),
                  pl.BlockSpec((tm, LANES), lambda i, j: (i % nseq, 0))],
        out_specs=pl.BlockSpec((tm, tn), lambda i, j: (i, j)),
        compiler_params=_cparams("parallel", "parallel"),
        name="qk_proj",
    )(h, w_qk, gain_row, cos_t, sinm_t, sinp_t)


def _rope_tables(seq):
    half = ROT_DIM // 2
    inv_freq = ROPE_THETA ** (-jnp.arange(0, ROT_DIM, 2, dtype=F32) / ROT_DIM)
    ang = jnp.arange(seq, dtype=F32)[:, None] * inv_freq[None, :]
    cos, sin = jnp.cos(ang), jnp.sin(ang)
    ones = jnp.ones((seq, DA_HEAD_DIM - ROT_DIM), F32)
    zeros = jnp.zeros((seq, DA_HEAD_DIM - ROT_DIM), F32)
    zh = jnp.zeros((seq, half), F32)
    cos64 = jnp.concatenate([cos, cos, ones], axis=1)
    sinm64 = jnp.concatenate([-sin, zh, zeros], axis=1)
    sinp64 = jnp.concatenate([zh, sin, zeros], axis=1)
    return tuple(jnp.concatenate([a, a], axis=1) for a in (cos64, sinm64, sinp64))


def _diff_attn_kernel(lam_ref, q_ref, k_ref, v_ref, g_ref, o_ref,
                      m1_ref, l1_ref, a1_ref, m2_ref, l2_ref, a2_ref, *, tq, tk, lam_init):
    i = pl.program_id(2)
    lane = lax.broadcasted_iota(jnp.int32, (1, DA_V_DIM), 1)
    q = q_ref[...]
    zero = jnp.zeros_like(q)
    q1 = jnp.where(lane < DA_HEAD_DIM, q, zero)
    q2 = jnp.where(lane >= DA_HEAD_DIM, q, zero)

    for m_ref, l_ref, a_ref in ((m1_ref, l1_ref, a1_ref), (m2_ref, l2_ref, a2_ref)):
        m_ref[...] = jnp.full(m_ref.shape, -jnp.inf, F32)
        l_ref[...] = jnp.zeros(l_ref.shape, F32)
        a_ref[...] = jnp.zeros(a_ref.shape, F32)

    def update(qc, kj, vj, mask, m_ref, l_ref, a_ref):
        s = _dot(qc, kj, NT_DIMS)
        if mask is not None:
            s = jnp.where(mask, s, -jnp.inf)
        m_old = m_ref[...]
        m_new = jnp.maximum(m_old, jnp.max(s, axis=-1, keepdims=True))
        alpha = jnp.exp(m_old - m_new)
        p = jnp.exp(s - m_new)
        l_ref[...] = alpha * l_ref[...] + jnp.sum(p, axis=-1, keepdims=True)
        a_ref[...] = alpha * a_ref[...] + _dot(p.astype(BF16), vj)
        m_ref[...] = m_new

    def step(j, mask):
        off = pl.multiple_of(j * tk, tk)
        kj = k_ref[pl.ds(off, tk), :]
        vj = v_ref[pl.ds(off, tk), :]
        update(q1, kj, vj, mask, m1_ref, l1_ref, a1_ref)
        update(q2, kj, vj, mask, m2_ref, l2_ref, a2_ref)

    n_full = (i * tq) // tk

    def full_body(j, c):
        step(j, None)
        return c

    lax.fori_loop(0, n_full, full_body, 0)

    row = i * tq + lax.broadcasted_iota(jnp.int32, (tq, tk), 0)
    for d in range(max(1, tq // tk)):
        j = n_full + d
        col = j * tk + lax.broadcasted_iota(jnp.int32, (tq, tk), 1)
        step(j, col <= row)

    lq1, lk1, lq2, lk2 = (lam_ref[r:r + 1, :] for r in range(4))
    lam = (jnp.exp(jnp.sum(lq1 * lk1, axis=-1, keepdims=True))
           - jnp.exp(jnp.sum(lq2 * lk2, axis=-1, keepdims=True)) + lam_init)
    o = a1_ref[...] / l1_ref[...] - lam * (a2_ref[...] / l2_ref[...])
    ms = jnp.mean(o * o, axis=-1, keepdims=True)
    o_ref[...] = (o * lax.rsqrt(ms + SUBLN_EPS) * (g_ref[...] * (1.0 - lam_init))).astype(o_ref.dtype)


def _diff_attention(qk, v, lam_params, subln_g, batch, seq, lam_init, tq, tk):
    t = qk.shape[0]
    nq = seq // tq
    kern = functools.partial(_diff_attn_kernel, tq=tq, tk=tk, lam_init=lam_init)
    return pl.pallas_call(
        kern,
        out_shape=jax.ShapeDtypeStruct((t, DA_HEADS * DA_V_DIM), BF16),
        grid=(batch, DA_HEADS, nq),
        in_specs=[pl.BlockSpec((4, DA_HEAD_DIM), lambda b, h, i: (0, 0)),
                  pl.BlockSpec((tq, DA_V_DIM), lambda b, h, i: (b * nq + i, h)),
                  pl.BlockSpec((seq, DA_V_DIM), lambda b, h, i: (b, DA_HEADS + h)),
                  pl.BlockSpec((seq, DA_V_DIM), lambda b, h, i: (b, h)),
                  pl.BlockSpec((1, DA_V_DIM), lambda b, h, i: (0, 0))],
        out_specs=pl.BlockSpec((tq, DA_V_DIM), lambda b, h, i: (b * nq + i, h)),
        scratch_shapes=[pltpu.VMEM((tq, 1), F32), pltpu.VMEM((tq, 1), F32), pltpu.VMEM((tq, DA_V_DIM), F32),
                        pltpu.VMEM((tq, 1), F32), pltpu.VMEM((tq, 1), F32), pltpu.VMEM((tq, DA_V_DIM), F32)],
        compiler_params=_cparams("parallel", "parallel", "parallel"),
        name="diff_attention",
    )(lam_params, qk, qk, v, subln_g.reshape(1, DA_V_DIM))


DECAY_SCALE = math.exp(-0.5)


def _rwkv_prep_kernel(r_ref, k_ref, v_ref, lo_ref, pr_ref, pk_ref, pv_ref, plo_ref,
                      mur_ref, muk_ref, muv_ref, mulo_ref, par_ref, wup_ref, aup_ref, gup_ref,
                      at_ref, rt_ref, bt_ref, kt_ref, bg_ref, kg_ref, vb_ref, bonus_ref, g_ref, gam_ref,
                      *, tm, nseq):
    i = pl.program_id(0)
    seq_start = (i % nseq) == 0
    row0 = lax.broadcasted_iota(jnp.int32, (tm, 1), 0) == 0

    def shifted(x_ref, p_ref, mu_ref):
        x = x_ref[...]
        last = jnp.where(seq_start, 0.0, p_ref[SUBLANES - 1:SUBLANES, :])
        prev = jnp.where(row0, last, pltpu.roll(x, 1, 0))
        return x + (prev - x) * mu_ref[...]

    r = shifted(r_ref, pr_ref, mur_ref)
    k = shifted(k_ref, pk_ref, muk_ref)
    v = shifted(v_ref, pv_ref, muv_ref)
    lo = shifted(lo_ref, plo_ref, mulo_ref)
    dw, da, dg = lo[:, :LANES], lo[:, LANES:2 * LANES], lo[:, 2 * LANES:]
    w0, a0, k_k, k_a, r_k = (par_ref[j:j + 1, :] for j in range(5))

    u = w0 + _dot(jnp.tanh(dw).astype(BF16), wup_ref[...])
    ld = -DECAY_SCALE * jax.nn.sigmoid(u)
    a = jax.nn.sigmoid(a0 + _dot(da.astype(BF16), aup_ref[...]))
    g = _dot(jax.nn.sigmoid(dg).astype(BF16), gup_ref[...])

    bd = _block_diag_ones(LANES, RW_HEAD_DIM)
    kk = k * k_k
    kk = kk / jnp.maximum(jnp.sqrt(_group_sum64(kk * kk, bd)), 1e-12)
    k2 = k * (1.0 + (a - 1.0) * k_a)
    bonus = _group_sum64(r * k2 * r_k, bd) * v

    t_i = lax.broadcasted_iota(jnp.int32, (tm, tm), 0)
    s_i = lax.broadcasted_iota(jnp.int32, (tm, tm), 1)
    same = (t_i // CHUNK) == (s_i // CHUNK)
    tri = jnp.where(same & (s_i <= t_i), 1.0, 0.0).astype(BF16)
    rest = jnp.where(same & (s_i > t_i), 1.0, 0.0).astype(BF16)
    c_i = lax.broadcasted_iota(jnp.int32, (tm // CHUNK, tm), 0)
    cs_i = lax.broadcasted_iota(jnp.int32, (tm // CHUNK, tm), 1)
    whole = jnp.where(cs_i // CHUNK == c_i, 1.0, 0.0).astype(BF16)

    ldh, ldm, ldl = _split3(ld)

    def sel(m):
        return _dot(m, ldh) + _dot(m, ldm) + _dot(m, ldl)

    cum = sel(tri)
    e_neg = jnp.exp(-cum)
    e_rem = jnp.exp(sel(rest))
    b = kk * a
    at_ref[...] = (-kk * jnp.exp(cum - ld)).astype(BF16)
    rt_ref[...] = (r * jnp.exp(cum)).astype(BF16)
    bt_ref[...] = (b * e_neg).astype(BF16)
    kt_ref[...] = (k2 * e_neg).astype(BF16)
    bg_ref[...] = (b * e_rem).astype(BF16)
    kg_ref[...] = (k2 * e_rem).astype(BF16)
    vb_ref[...] = v.astype(BF16)
    bonus_ref[...] = bonus.astype(BF16)
    g_ref[...] = g.astype(BF16)
    gam_ref[0] = jnp.exp(sel(whole))


def _rwkv_prep(rw, mu, params, wup, aup, gup, seq, tm, tn):
    t = rw.shape[0]
    nj = RW_WIDTH // tn
    lo_w = 4 * LANES
    lo_blk = (3 * RW_WIDTH) // lo_w
    nseq = seq // tm
    rpb = tm // SUBLANES

    def cur(off):
        return pl.BlockSpec((tm, tn), lambda i, j: (i, off * nj + j))

    def prv(off):
        return pl.BlockSpec((SUBLANES, tn), lambda i, j: (jnp.maximum(i * rpb - 1, 0), off * nj + j))

    def row(off):
        return pl.BlockSpec((1, tn), lambda i, j: (0, off * nj + j))

    in_specs = [cur(0), cur(1), cur(2), pl.BlockSpec((tm, lo_w), lambda i, j: (i, lo_blk)),
                prv(0), prv(1), prv(2),
                pl.BlockSpec((SUBLANES, lo_w), lambda i, j: (jnp.maximum(i * rpb - 1, 0), lo_blk)),
                row(0), row(1), row(2), pl.BlockSpec((1, lo_w), lambda i, j: (0, lo_blk)),
                pl.BlockSpec((SUBLANES, tn), lambda i, j: (0, j)),
                pl.BlockSpec((LANES, tn), lambda i, j: (0, j)),
                pl.BlockSpec((LANES, tn), lambda i, j: (0, j)),
                pl.BlockSpec((2 * LANES, tn), lambda i, j: (0, j))]
    out_blk = pl.BlockSpec((tm, tn), lambda i, j: (i, j))
    outs = [jax.ShapeDtypeStruct((t, RW_WIDTH), BF16)] * 9
    outs.append(jax.ShapeDtypeStruct((t // tm, tm // CHUNK, RW_WIDTH), F32))
    out_specs = [out_blk] * 9 + [pl.BlockSpec((1, tm // CHUNK, tn), lambda i, j: (i, 0, j))]
    return pl.pallas_call(
        functools.partial(_rwkv_prep_kernel, tm=tm, nseq=nseq),
        out_shape=outs,
        grid=(t // tm, nj),
        in_specs=in_specs,
        out_specs=out_specs,
        compiler_params=_cparams("parallel", "parallel"),
        name="rwkv_prep",
    )(rw, rw, rw, rw, rw, rw, rw, rw, mu, mu, mu, mu, params, wup, aup, gup)


PAIR = 2 * RW_HEAD_DIM


def _dot_bf16(a, b, dims=None):
    return _dot(a.astype(BF16), b.astype(BF16), dims)


def _dot_split(a, b, dims=None):
    ah = a.astype(BF16)
    al = (a - ah.astype(F32)).astype(BF16)
    bh = b.astype(BF16)
    bl = (b - bh.astype(F32)).astype(BF16)
    return _dot(ah, bh, dims) + _dot(al, bh, dims) + _dot(ah, bl, dims)


def _rwkv_chunk_kernel(at_ref, rt_ref, bt_ref, kt_ref, bg_ref, kg_ref, v_ref, bonus_ref, g_ref,
                       gam_ref, lng_ref, lnb_ref, o_ref, s_ref):
    c = pl.program_id(1)

    @pl.when(c == 0)
    def _():
        s_ref[...] = jnp.zeros(s_ref.shape, F32)

    lane = lax.broadcasted_iota(jnp.int32, (1, PAIR), 1)
    first = lane < RW_HEAD_DIM
    rho = lax.broadcasted_iota(jnp.int32, (PAIR, PAIR), 0)
    sig = lax.broadcasted_iota(jnp.int32, (PAIR, PAIR), 1)
    strict, incl, eye = sig < rho, sig <= rho, sig == rho
    own = (rho // RW_HEAD_DIM) == (sig // RW_HEAD_DIM)

    def stacked(x):
        z = jnp.zeros_like(x)
        return jnp.concatenate([jnp.where(first, x, z), jnp.where(first, z, x)], axis=0)

    for p in range(RW_HEADS // 2):
        sl = slice(p * PAIR, (p + 1) * PAIR)
        pa, pr, pb, pk, pbg, pkg, pv = (stacked(ref[:, sl]) for ref in
                                        (at_ref, rt_ref, bt_ref, kt_ref, bg_ref, kg_ref, v_ref))
        prods = _dot(jnp.concatenate([pa, pr], axis=0), jnp.concatenate([pb, pk], axis=0), NT_DIMS)
        zero = jnp.zeros((PAIR, PAIR), F32)
        lmat = jnp.where(strict, prods[:PAIR, :PAIR], zero)
        sak = jnp.where(strict, prods[:PAIR, PAIR:], zero)
        lrb = jnp.where(incl, prods[PAIR:, :PAIR], zero)
        lrk = jnp.where(incl, prods[PAIR:, PAIR:], zero)

        x = jnp.concatenate([pa.astype(F32), _dot(sak.astype(BF16), pv)], axis=1)
        li = lmat
        n_steps = CHUNK.bit_length() - 1
        for it in range(n_steps):
            if it + 1 < n_steps:
                res = _dot_bf16(li, jnp.concatenate([li, x], axis=1))
                x = x + res[:, PAIR:]
                li = res[:, :PAIR]
            else:
                x = x + _dot_bf16(li, x)
        gmat = jnp.concatenate([x, jnp.concatenate([zero, pv.astype(F32)], axis=1)], axis=0).astype(BF16)
        out1 = _dot(jnp.concatenate([lrb, lrk], axis=1).astype(BF16), gmat)
        out2 = _dot(gmat, jnp.concatenate([pbg, pkg], axis=0), TN_DIMS)
        qe = out1[:, :PAIR] + pr.astype(F32)
        mmat = out2[:PAIR] + jnp.where(eye, gam_ref[0][:, sl], 0.0)
        nmat = out2[PAIR:]

        s0 = s_ref[p]
        y = _dot_split(qe, s0, NT_DIMS) + out1[:, PAIR:]
        s_ref[p] = _dot_split(s0, mmat) + nmat

        mean = jnp.sum(y, axis=-1, keepdims=True) * (1.0 / RW_HEAD_DIM)
        d = jnp.where(own, y - mean, 0.0)
        var = jnp.sum(d * d, axis=-1, keepdims=True) * (1.0 / RW_HEAD_DIM)
        yn = d * lax.rsqrt(var + GN_EPS)
        yn = yn[:CHUNK] + yn[CHUNK:]
        out = (yn * lng_ref[:, sl] + lnb_ref[:, sl] + bonus_ref[:, sl].astype(F32)) * g_ref[:, sl].astype(F32)
        o_ref[:, sl] = out.astype(o_ref.dtype)


def _rwkv_chunks(prep, lnx_g, lnx_b, batch, seq):
    at, rt, bt, kt, bg, kg, vb, bonus, g, gam = prep
    t = at.shape[0]
    nc = seq // CHUNK
    gam = gam.reshape(t // CHUNK, 1, RW_WIDTH)
    blk = pl.BlockSpec((CHUNK, RW_WIDTH), lambda b, c: (b * nc + c, 0))
    rowspec = pl.BlockSpec((1, RW_WIDTH), lambda b, c: (0, 0))
    return pl.pallas_call(
        _rwkv_chunk_kernel,
        out_shape=jax.ShapeDtypeStruct((t, RW_WIDTH), BF16),
        grid=(batch, nc),
        in_specs=[blk] * 9 + [pl.BlockSpec((1, 1, RW_WIDTH), lambda b, c: (b * nc + c, 0, 0)), rowspec, rowspec],
        out_specs=blk,
        scratch_shapes=[pltpu.VMEM((RW_HEADS // 2, PAIR, PAIR), F32)],
        compiler_params=_cparams("arbitrary", "arbitrary"),
        name="rwkv_chunks",
    )(at, rt, bt, kt, bg, kg, vb, bonus, g, gam, lnx_g.reshape(1, RW_WIDTH), lnx_b.reshape(1, RW_WIDTH))


def _merge_kernel(h_ref, ya_ref, yb_ref, wga_ref, wgb_ref, pa_ref, pb_ref, o_ref):
    h = h_ref[...]
    ga = jax.nn.sigmoid(_dot(h, wga_ref[...]))
    gb = jax.nn.sigmoid(_dot(h, wgb_ref[...]))
    o_ref[...] = (ga * _dot(ya_ref[...], pa_ref[...]) + gb * _dot(yb_ref[...], pb_ref[...])).astype(o_ref.dtype)


def _merge(h, ya, yb, wga, wgb, pa, pb, tm, tn):
    t, d = h.shape
    n = wga.shape[1]
    ka, kb = ya.shape[1], yb.shape[1]
    return pl.pallas_call(
        _merge_kernel,
        out_shape=jax.ShapeDtypeStruct((t, n), BF16),
        grid=(t // tm, n // tn),
        in_specs=[pl.BlockSpec((tm, d), lambda i, j: (i, 0)),
                  pl.BlockSpec((tm, ka), lambda i, j: (i, 0)),
                  pl.BlockSpec((tm, kb), lambda i, j: (i, 0)),
                  pl.BlockSpec((d, tn), lambda i, j: (0, j)),
                  pl.BlockSpec((d, tn), lambda i, j: (0, j)),
                  pl.BlockSpec((ka, tn), lambda i, j: (0, j)),
                  pl.BlockSpec((kb, tn), lambda i, j: (0, j))],
        out_specs=pl.BlockSpec((tm, tn), lambda i, j: (i, j)),
        compiler_params=_cparams("parallel", "parallel"),
        name="gated_merge",
    )(h, ya, yb, wga, wgb, pa, pb)


def _first_lane_where(cond, lane):
    return jnp.min(jnp.where(cond, lane, LANES), axis=-1, keepdims=True)


def _outproj_kernel(m_ref, x_ref, wo_ref, g2_ref, wrh_ref, wrl_ref, rb_ref, x1_ref, h2_ref, route_ref):
    x1 = x_ref[...] + _dot(m_ref[...], wo_ref[...])
    x1_ref[...] = x1
    ms = jnp.mean(x1 * x1, axis=-1, keepdims=True)
    h2 = x1 * lax.rsqrt(ms + NORM_EPS) * g2_ref[...]
    h2_ref[...] = h2
    hh = h2.astype(BF16)
    hl = (h2 - hh.astype(F32)).astype(BF16)
    wrh = wrh_ref[...]
    lg = _dot(hh, wrh) + _dot(hl, wrh) + _dot(hh, wrl_ref[...]) + rb_ref[...]

    lane = lax.broadcasted_iota(jnp.int32, lg.shape, 1)
    neg = -jnp.inf
    is_g = lane < N_GROUPS
    mg = jnp.max(jnp.where(is_g, lg, neg), axis=-1, keepdims=True)
    eg = jnp.where(is_g, jnp.exp(lg - mg), 0.0)
    pg = eg / jnp.sum(eg, axis=-1, keepdims=True)
    p_g_top = jnp.max(pg, axis=-1, keepdims=True)
    g_idx = _first_lane_where(is_g & (pg == p_g_top), lane)
    lo = N_GROUPS + g_idx * EXPERTS_PER_GROUP
    sel = (lane >= lo) & (lane < lo + EXPERTS_PER_GROUP)
    me = jnp.max(jnp.where(sel, lg, neg), axis=-1, keepdims=True)
    ee = jnp.where(sel, jnp.exp(lg - me), 0.0)
    pe = ee / jnp.sum(ee, axis=-1, keepdims=True)
    pe = jnp.where(sel, pe, -1.0)
    v1 = jnp.max(pe, axis=-1, keepdims=True)
    i1 = _first_lane_where(pe == v1, lane)
    pe2 = jnp.where(lane == i1, -1.0, pe)
    v2 = jnp.max(pe2, axis=-1, keepdims=True)
    i2 = _first_lane_where(pe2 == v2, lane)
    den = v1 + v2
    route = jnp.where(lane == 0, p_g_top * v1 / den,
                      jnp.where(lane == 1, p_g_top * v2 / den,
                                jnp.where(lane == 2, (i1 - N_GROUPS).astype(F32),
                                          jnp.where(lane == 3, (i2 - N_GROUPS).astype(F32), 0.0))))
    route_ref[...] = route


def _outproj(merged, x, wo, g2, wr_hi, wr_lo, rbias, tm):
    t, d = x.shape
    nr = wr_hi.shape[1]
    return pl.pallas_call(
        _outproj_kernel,
        out_shape=[jax.ShapeDtypeStruct((t, d), F32), jax.ShapeDtypeStruct((t, d), F32),
                   jax.ShapeDtypeStruct((t, nr), F32)],
        grid=(t // tm,),
        in_specs=[pl.BlockSpec((tm, d), lambda i: (i, 0)),
                  pl.BlockSpec((tm, d), lambda i: (i, 0)),
                  pl.BlockSpec((d, d), lambda i: (0, 0)),
                  pl.BlockSpec((1, d), lambda i: (0, 0)),
                  pl.BlockSpec((d, nr), lambda i: (0, 0)),
                  pl.BlockSpec((d, nr), lambda i: (0, 0)),
                  pl.BlockSpec((1, nr), lambda i: (0, 0))],
        out_specs=[pl.BlockSpec((tm, d), lambda i: (i, 0)),
                   pl.BlockSpec((tm, d), lambda i: (i, 0)),
                   pl.BlockSpec((tm, nr), lambda i: (i, 0))],
        compiler_params=_cparams("parallel"),
        name="outproj_norm_router",
    )(merged, x, wo, g2.reshape(1, d), wr_hi, wr_lo, rbias)


def _moe_kernel(be_ref, nused_ref, tok_hbm, dst_hbm, h_hbm, wg_ref, wu_ref, wd_ref, y_hbm,
                xbuf, obuf, tok_s, dst_s, idx_sem, g_sem, s_sem, *, rows, n_blocks):
    b = pl.program_id(0)
    slot = b % 2
    n_used = nused_ref[0]

    def idx_copies(blk, sl):
        return (pltpu.make_async_copy(tok_hbm.at[blk], tok_s.at[sl], idx_sem.at[0]),
                pltpu.make_async_copy(dst_hbm.at[blk], dst_s.at[sl], idx_sem.at[1]))

    def gather_copy(r, sl):
        return pltpu.make_async_copy(h_hbm.at[tok_s[sl, r]], xbuf.at[sl, r], g_sem.at[sl])

    def scatter_copy(r, sl):
        return pltpu.make_async_copy(obuf.at[r], y_hbm.at[dst_s[sl, r]], s_sem.at[0])

    def fetch(blk, sl):
        for cp in idx_copies(blk, sl):
            cp.start()
        for cp in idx_copies(blk, sl):
            cp.wait()

        def issue(r, c):
            gather_copy(r, sl).start()
            return c

        lax.fori_loop(0, rows, issue, 0, unroll=8)

    @pl.when((b == 0) & (n_used > 0))
    def _():
        fetch(0, 0)

    @pl.when(b + 1 < n_used)
    def _():
        fetch(b + 1, 1 - slot)

    @pl.when(b < n_used)
    def _():
        def wait_g(r, c):
            gather_copy(r, slot).wait()
            return c

        lax.fori_loop(0, rows, wait_g, 0, unroll=8)
        x = xbuf[slot].astype(BF16)
        gate = _dot(x, wg_ref[...])
        up = _dot(x, wu_ref[...])
        mid = (gate * jax.nn.sigmoid(gate) * up).astype(BF16)
        obuf[...] = _dot(mid, wd_ref[...]).astype(obuf.dtype)

        def issue_s(r, c):
            @pl.when(dst_s[slot, r] >= 0)
            def _():
                scatter_copy(r, slot).start()
            return c

        def wait_s(r, c):
            @pl.when(dst_s[slot, r] >= 0)
            def _():
                scatter_copy(r, slot).wait()
            return c

        lax.fori_loop(0, rows, issue_s, 0, unroll=8)
        lax.fori_loop(0, rows, wait_s, 0, unroll=8)


def _moe(h2, row_tok, row_dst, block_e, n_used, wg, wu, wd, rows):
    t, d = h2.shape
    n_blocks = row_tok.shape[0]
    a = t * TOP_K
    f = wg.shape[2]
    kern = functools.partial(_moe_kernel, rows=rows, n_blocks=n_blocks)
    grid_spec = pltpu.PrefetchScalarGridSpec(
        num_scalar_prefetch=2,
        grid=(n_blocks,),
        in_specs=[pl.BlockSpec(memory_space=pl.ANY),
                  pl.BlockSpec(memory_space=pl.ANY),
                  pl.BlockSpec(memory_space=pl.ANY),
                  pl.BlockSpec((None, d, f), lambda b, be, nu: (be[b], 0, 0)),
                  pl.BlockSpec((None, d, f), lambda b, be, nu: (be[b], 0, 0)),
                  pl.BlockSpec((None, f, d), lambda b, be, nu: (be[b], 0, 0))],
        out_specs=pl.BlockSpec(memory_space=pl.ANY),
        scratch_shapes=[pltpu.VMEM((2, rows, d), F32),
                        pltpu.VMEM((rows, d), F32),
                        pltpu.SMEM((2, rows), jnp.int32),
                        pltpu.SMEM((2, rows), jnp.int32),
                        pltpu.SemaphoreType.DMA((2,)),
                        pltpu.SemaphoreType.DMA((2,)),
                        pltpu.SemaphoreType.DMA((1,))],
    )
    return pl.pallas_call(
        kern,
        out_shape=jax.ShapeDtypeStruct((a, d), F32),
        grid_spec=grid_spec,
        compiler_params=_cparams("arbitrary"),
        name="moe_experts",
    )(block_e, n_used, row_tok, row_dst, h2, wg, wu, wd)


def _combine_kernel(x1_ref, y0_ref, y1_ref, route_ref, o_ref):
    gts = route_ref[...]
    o_ref[...] = x1_ref[...] + gts[:, 0:1] * y0_ref[...] + gts[:, 1:2] * y1_ref[...]


def _combine(x1, y, route, tm):
    t, d = x1.shape
    nb = t // tm
    return pl.pallas_call(
        _combine_kernel,
        out_shape=jax.ShapeDtypeStruct((t, d), F32),
        grid=(nb,),
        in_specs=[pl.BlockSpec((tm, d), lambda i: (i, 0)),
                  pl.BlockSpec((tm, d), lambda i: (i, 0)),
                  pl.BlockSpec((tm, d), lambda i: (nb + i, 0)),
                  pl.BlockSpec((tm, LANES), lambda i: (i, 0))],
        out_specs=pl.BlockSpec((tm, d), lambda i: (i, 0)),
        compiler_params=_cparams("parallel"),
        name="moe_combine",
    )(x1, y, y, route)


def _routing_tables(route, rows):
    t = route.shape[0]
    a = t * TOP_K
    expert = route[:, 2:2 + TOP_K].astype(jnp.int32)
    flat_e = expert.reshape(a)
    ids = jnp.arange(a, dtype=jnp.int32)
    order = jnp.sort(flat_e * a + ids) % a
    counts = jnp.sum((flat_e[:, None] == jnp.arange(N_EXPERTS, dtype=jnp.int32)[None, :]).astype(jnp.int32), axis=0)
    padded = (counts + rows - 1) // rows * rows
    start = jnp.cumsum(counts) - counts
    pend = jnp.cumsum(padded)
    pstart = pend - padded
    n_rows = a + N_EXPERTS * rows
    n_blocks = n_rows // rows
    blk_row0 = jnp.arange(n_blocks, dtype=jnp.int32) * rows
    block_e = jnp.minimum(jnp.sum((pend[None, :] <= blk_row0[:, None]).astype(jnp.int32), axis=1), N_EXPERTS - 1)
    rho = jnp.arange(n_rows, dtype=jnp.int32)
    be = jnp.repeat(block_e, rows)
    off = rho - pstart[be]
    valid = (off >= 0) & (off < counts[be])
    src = order[jnp.clip(start[be] + off, 0, a - 1)]
    row_tok = jnp.where(valid, src // TOP_K, 0)
    row_dst = jnp.where(valid, (src % TOP_K) * t + src // TOP_K, -1)
    n_used = (pend[-1] // rows).astype(jnp.int32).reshape(1)
    return row_tok.reshape(n_blocks, rows), row_dst.reshape(n_blocks, rows), block_e.astype(jnp.int32), n_used


def _tiles(t, seq):
    return dict(
        norm_tm=min(512, t),
        proj_tm=min(1024, seq), proj_tn=512,
        attn_tq=min(256, seq), attn_tk=min(512, seq),
        prep_tm=min(256, seq), prep_tn=256,
        merge_tm=min(1024, t), merge_tn=512,
        out_tm=min(256, t),
        moe_rows=256,
        comb_tm=min(512, t),
    )


def _pad_rows(w, n):
    return jnp.pad(w, ((0, n - w.shape[0]), (0, 0)))


def _pad_cols(w, n):
    return jnp.pad(w, ((0, 0), (0, n - w.shape[1])))


def kernel(x, norm1_g, w_in, q_norm_g, k_norm_g, lam_q1, lam_k1, lam_q2, lam_k2, subln_g, shift_mu, w0, w_up, a0, a_up, g_up, k_k, k_a, r_k, lnx_g, lnx_b, proj_a, proj_b, w_out, norm2_g, router_g, router_g_b, router_e, router_e_b, w_gate_e, w_up_e, w_down_e):
    batch, seq, d = x.shape
    t = batch * seq
    depth = norm1_g.shape[0]
    tl = _tiles(t, seq)
    qkw = DA_HEADS * 2 * DA_HEAD_DIM
    vw = DA_HEADS * DA_V_DIM
    c_q, c_k, c_v = 0, qkw, 2 * qkw
    c_rw = c_v + vw
    c_dw = c_rw + 3 * RW_WIDTH
    c_da = c_dw + DECAY_LORA
    c_dg = c_da + AAA_LORA
    c_ga = c_dg + GATE_LORA
    c_gb = c_ga + d
    cos_t, sinm_t, sinp_t = _rope_tables(seq)
    xf = x.reshape(t, d)

    for l in range(depth):
        lam_init = 0.8 - 0.6 * math.exp(-0.3 * l)
        wl = w_in[l]
        w_qk = wl[:, c_q:c_v].astype(BF16)
        w_v = wl[:, c_v:c_rw].astype(BF16)
        w_rw = jnp.concatenate([wl[:, c_rw:c_dw], _pad_cols(wl[:, c_dw:c_da], LANES),
                                _pad_cols(wl[:, c_da:c_dg], LANES), wl[:, c_dg:c_ga]], axis=1).astype(BF16)
        w_ga = wl[:, c_ga:c_gb].astype(BF16)
        w_gb = wl[:, c_gb:].astype(BF16)
        mu = shift_mu[l]
        o_dw = 3 * RW_WIDTH
        mu_l = jnp.concatenate([mu[:o_dw], jnp.pad(mu[o_dw:o_dw + DECAY_LORA], (0, LANES - DECAY_LORA)),
                                jnp.pad(mu[o_dw + DECAY_LORA:o_dw + DECAY_LORA + AAA_LORA], (0, LANES - AAA_LORA)),
                                mu[o_dw + DECAY_LORA + AAA_LORA:]]).reshape(1, -1)
        gain_row = jnp.concatenate([jnp.tile(q_norm_g[l], 2 * DA_HEADS) * (DA_HEAD_DIM ** -0.5),
                                    jnp.tile(k_norm_g[l], 2 * DA_HEADS)]).reshape(1, 2 * qkw)
        lam_params = jnp.stack([lam_q1[l], lam_k1[l], lam_q2[l], lam_k2[l]])
        zrow = jnp.zeros((RW_WIDTH,), F32)
        rw_params = jnp.stack([w0[l], a0[l], k_k[l], k_a[l], r_k[l].reshape(-1), zrow, zrow, zrow])

        h = _rmsnorm(xf, norm1_g[l], NORM_EPS, tl["norm_tm"])
        qk = _qk_proj(h, w_qk, gain_row, cos_t, sinm_t, sinp_t, seq, tl["proj_tm"], tl["proj_tn"])
        v = _matmul(h, w_v, BF16, tl["proj_tm"], tl["proj_tn"], "v_proj")
        ya = _diff_attention(qk, v, lam_params, subln_g[l], batch, seq, lam_init, tl["attn_tq"], tl["attn_tk"])

        rw = _matmul(h, w_rw, F32, tl["proj_tm"], tl["proj_tn"], "rw_proj")
        prep = _rwkv_prep(rw, mu_l, rw_params, _pad_rows(w_up[l], LANES).astype(BF16),
                          _pad_rows(a_up[l], LANES).astype(BF16), g_up[l].astype(BF16),
                          seq, tl["prep_tm"], tl["prep_tn"])
        yb = _rwkv_chunks(prep, lnx_g[l], lnx_b[l], batch, seq)

        merged = _merge(h, ya, yb, w_ga, w_gb, proj_a[l].astype(BF16), proj_b[l].astype(BF16),
                        tl["merge_tm"], tl["merge_tn"])
        wr = _pad_cols(jnp.concatenate([router_g[l], router_e[l]], axis=1), LANES)
        wr_hi = wr.astype(BF16)
        wr_lo = (wr - wr_hi.astype(F32)).astype(BF16)
        rbias = jnp.pad(jnp.concatenate([router_g_b[l], router_e_b[l]]), (0, LANES - N_GROUPS - N_EXPERTS))
        x1, h2, route = _outproj(merged, xf, w_out[l].astype(BF16), norm2_g[l], wr_hi, wr_lo,
                                 rbias.reshape(1, LANES), tl["out_tm"])

        rows = tl["moe_rows"]
        row_tok, row_dst, block_e, n_used = _routing_tables(route, rows)
        y = _moe(h2, row_tok, row_dst, block_e, n_used, w_gate_e[l].astype(BF16), w_up_e[l].astype(BF16),
                 w_down_e[l].astype(BF16), rows)
        xf = _combine(x1, y, route, tl["comb_tm"])
    return xf.reshape(batch, seq, d)
```

```python
import functools
import math

import jax
import jax.numpy as jnp
from jax import lax
from jax.experimental import pallas as pl
from jax.experimental.pallas import tpu as pltpu

DA_HEADS = 8
DA_HEAD_DIM = 64
DA_V_DIM = 2 * DA_HEAD_DIM
ROT_DIM = DA_HEAD_DIM // 4
ROPE_THETA = 500000.0
SUBLN_EPS = 1e-5
RW_HEADS = 16
RW_HEAD_DIM = 64
RW_WIDTH = RW_HEADS * RW_HEAD_DIM
DECAY_LORA = 96
AAA_LORA = 96
GATE_LORA = 256
GN_EPS = 64e-5
N_GROUPS = 4
EXPERTS_PER_GROUP = 8
N_EXPERTS = N_GROUPS * EXPERTS_PER_GROUP
TOP_K = 2
NORM_EPS = 1e-6

LANES = 128
SUBLANES = 8
VMEM_LIMIT_BYTES = 56 * 1024 * 1024

CHUNK = 64

F32 = jnp.float32
BF16 = jnp.bfloat16

NT_DIMS = (((1,), (1,)), ((), ()))
TN_DIMS = (((0,), (0,)), ((), ()))


def _cparams(*sem):
    return pltpu.CompilerParams(dimension_semantics=tuple(sem), vmem_limit_bytes=VMEM_LIMIT_BYTES)


def _dot(a, b, dims=None, precision=None):
    if dims is None:
        return jnp.dot(a, b, preferred_element_type=F32, precision=precision)
    return lax.dot_general(a, b, dims, preferred_element_type=F32, precision=precision)


def _split3(x):
    h = x.astype(BF16)
    r = x - h.astype(F32)
    m = r.astype(BF16)
    l = (r - m.astype(F32)).astype(BF16)
    return h, m, l


def _dot_exact_rhs(sel_bf16, x_f32):
    h, m, l = _split3(x_f32)
    return _dot(sel_bf16, h) + _dot(sel_bf16, m) + _dot(sel_bf16, l)


def _group_sum64(x, bd):
    h, m, l = _split3(x)
    outs = []
    for s in range(x.shape[1] // LANES):
        sl = slice(s * LANES, (s + 1) * LANES)
        outs.append(_dot(h[:, sl], bd) + _dot(m[:, sl], bd) + _dot(l[:, sl], bd))
    return outs[0] if len(outs) == 1 else jnp.concatenate(outs, axis=1)


def _block_diag_ones(n, blk, dtype=BF16):
    r = lax.broadcasted_iota(jnp.int32, (n, n), 0) // blk
    c = lax.broadcasted_iota(jnp.int32, (n, n), 1) // blk
    return jnp.where(r == c, 1.0, 0.0).astype(dtype)


def _rmsnorm_kernel(x_ref, g_ref, o_ref, *, eps):
    x = x_ref[...]
    ms = jnp.mean(x * x, axis=-1, keepdims=True)
    o_ref[...] = (x * lax.rsqrt(ms + eps) * g_ref[...]).astype(o_ref.dtype)


def _rmsnorm(x, g, eps, tm):
    t, d = x.shape
    return pl.pallas_call(
        functools.partial(_rmsnorm_kernel, eps=eps),
        out_shape=jax.ShapeDtypeStruct((t, d), BF16),
        grid=(t // tm,),
        in_specs=[pl.BlockSpec((tm, d), lambda i: (i, 0)),
                  pl.BlockSpec((1, d), lambda i: (0, 0))],
        out_specs=pl.BlockSpec((tm, d), lambda i: (i, 0)),
        compiler_params=_cparams("parallel"),
        name="rmsnorm",
    )(x, g.reshape(1, d))


def _matmul_kernel(a_ref, w_ref, o_ref):
    o_ref[...] = _dot(a_ref[...], w_ref[...]).astype(o_ref.dtype)


def _matmul(a, w, out_dtype, tm, tn, name):
    t, k = a.shape
    n = w.shape[1]
    return pl.pallas_call(
        _matmul_kernel,
        out_shape=jax.ShapeDtypeStruct((t, n), out_dtype),
        grid=(t // tm, n // tn),
        in_specs=[pl.BlockSpec((tm, k), lambda i, j: (i, 0)),
                  pl.BlockSpec((k, tn), lambda i, j: (0, j))],
        out_specs=pl.BlockSpec((tm, tn), lambda i, j: (i, j)),
        compiler_params=_cparams("parallel", "parallel"),
        name=name,
    )(a, w)


def _qk_proj_kernel(a_ref, w_ref, gain_ref, cos_ref, sinm_ref, sinp_ref, o_ref, *, tn):
    acc = _dot(a_ref[...], w_ref[...])
    bd = _block_diag_ones(LANES, DA_HEAD_DIM)
    ms = _group_sum64(acc * acc, bd) * (1.0 / DA_HEAD_DIM)
    xn = acc * lax.rsqrt(ms + NORM_EPS) * gain_ref[...]
    reps = tn // LANES
    cos = jnp.tile(cos_ref[...], (1, reps))
    sinm = jnp.tile(sinm_ref[...], (1, reps))
    sinp = jnp.tile(sinp_ref[...], (1, reps))
    half = ROT_DIM // 2
    hi = pltpu.roll(xn, tn - half, 1)
    lo = pltpu.roll(xn, half, 1)
    o_ref[...] = (xn * cos + hi * sinm + lo * sinp).astype(o_ref.dtype)


def _qk_proj(h, w_qk, gain_row, cos_t, sinm_t, sinp_t, seq, tm, tn):
    t, k = h.shape
    n = w_qk.shape[1]
    nseq = seq // tm
    return pl.pallas_call(
        functools.partial(_qk_proj_kernel, tn=tn),
        out_shape=jax.ShapeDtypeStruct((t, n), BF16),
        grid=(t // tm, n // tn),
        in_specs=[pl.BlockSpec((tm, k), lambda i, j: (i, 0)),
                  pl.BlockSpec((k, tn), lambda i, j: (0, j)),
                  pl.BlockSpec((1, tn), lambda i, j: (0, j)),
                  pl.BlockSpec((tm, LANES), lambda i, j: (i % nseq, 0)),
                  pl.BlockSpec((tm, LANES), lambda i, j: (i % nseq, 0)),
                  pl.BlockSpec((tm, LANES), lambda i, j: (i % nseq, 0))],
        out_specs=pl.BlockSpec((tm, tn), lambda i, j: (i, j)),
        compiler_params=_cparams("parallel", "parallel"),
        name="qk_proj",
    )(h, w_qk, gain_row, cos_t, sinm_t, sinp_t)


def _rope_tables(seq):
    half = ROT_DIM // 2
    inv_freq = ROPE_THETA ** (-jnp.arange(0, ROT_DIM, 2, dtype=F32) / ROT_DIM)
    ang = jnp.arange(seq, dtype=F32)[:, None] * inv_freq[None, :]
    cos, sin = jnp.cos(ang), jnp.sin(ang)
    ones = jnp.ones((seq, DA_HEAD_DIM - ROT_DIM), F32)
    zeros = jnp.zeros((seq, DA_HEAD_DIM - ROT_DIM), F32)
    zh = jnp.zeros((seq, half), F32)
    cos64 = jnp.concatenate([cos, cos, ones], axis=1)
    sinm64 = jnp.concatenate([-sin, zh, zeros], axis=1)
    sinp64 = jnp.concatenate([zh, sin, zeros], axis=1)
    return tuple(jnp.concatenate([a, a], axis=1) for a in (cos64, sinm64, sinp64))


def _diff_attn_kernel(lam_ref, q_ref, k_ref, v_ref, g_ref, o_ref,
                      m1_ref, l1_ref, a1_ref, m2_ref, l2_ref, a2_ref, *, tq, tk, lam_init):
    i = pl.program_id(2)
    lane = lax.broadcasted_iota(jnp.int32, (1, DA_V_DIM), 1)
    q = q_ref[...]
    zero = jnp.zeros_like(q)
    q1 = jnp.where(lane < DA_HEAD_DIM, q, zero)
    q2 = jnp.where(lane >= DA_HEAD_DIM, q, zero)

    for m_ref, l_ref, a_ref in ((m1_ref, l1_ref, a1_ref), (m2_ref, l2_ref, a2_ref)):
        m_ref[...] = jnp.full(m_ref.shape, -jnp.inf, F32)
        l_ref[...] = jnp.zeros(l_ref.shape, F32)
        a_ref[...] = jnp.zeros(a_ref.shape, F32)

    stats = ((m1_ref, l1_ref, a1_ref), (m2_ref, l2_ref, a2_ref))

    def step(j, mask):
        off = pl.multiple_of(j * tk, tk)
        kj = k_ref[pl.ds(off, tk), :]
        vj = v_ref[pl.ds(off, tk), :]
        s = [_dot(qc, kj, NT_DIMS) for qc in (q1, q2)]
        for c, (m_ref, l_ref, a_ref) in enumerate(stats):
            sc = s[c] if mask is None else jnp.where(mask, s[c], -jnp.inf)
            m_old = m_ref[...]
            m_new = jnp.maximum(m_old, jnp.max(sc, axis=-1, keepdims=True))
            alpha = jnp.exp(m_old - m_new)
            p = jnp.exp(sc - m_new)
            l_ref[...] = alpha * l_ref[...] + jnp.sum(p, axis=-1, keepdims=True)
            a_ref[...] = alpha * a_ref[...] + _dot(p.astype(BF16), vj)
            m_ref[...] = m_new

    n_full = (i * tq) // tk

    def full_body(j, c):
        step(j, None)
        return c

    lax.fori_loop(0, n_full, full_body, 0)

    row = i * tq + lax.broadcasted_iota(jnp.int32, (tq, tk), 0)
    for d in range(max(1, tq // tk)):
        j = n_full + d
        col = j * tk + lax.broadcasted_iota(jnp.int32, (tq, tk), 1)
        step(j, col <= row)

    lq1, lk1, lq2, lk2 = (lam_ref[r:r + 1, :] for r in range(4))
    lam = (jnp.exp(jnp.sum(lq1 * lk1, axis=-1, keepdims=True))
           - jnp.exp(jnp.sum(lq2 * lk2, axis=-1, keepdims=True)) + lam_init)
    o = a1_ref[...] / l1_ref[...] - lam * (a2_ref[...] / l2_ref[...])
    ms = jnp.mean(o * o, axis=-1, keepdims=True)
    o_ref[...] = (o * lax.rsqrt(ms + SUBLN_EPS) * (g_ref[...] * (1.0 - lam_init))).astype(o_ref.dtype)


def _diff_attention(qk, v, lam_params, subln_g, batch, seq, lam_init, tq, tk):
    t = qk.shape[0]
    nq = seq // tq
    kern = functools.partial(_diff_attn_kernel, tq=tq, tk=tk, lam_init=lam_init)
    return pl.pallas_call(
        kern,
        out_shape=jax.ShapeDtypeStruct((t, DA_HEADS * DA_V_DIM), BF16),
        grid=(batch, DA_HEADS, nq),
        in_specs=[pl.BlockSpec((4, DA_HEAD_DIM), lambda b, h, i: (0, 0)),
                  pl.BlockSpec((tq, DA_V_DIM), lambda b, h, i: (b * nq + i, h)),
                  pl.BlockSpec((seq, DA_V_DIM), lambda b, h, i: (b, DA_HEADS + h)),
                  pl.BlockSpec((seq, DA_V_DIM), lambda b, h, i: (b, h)),
                  pl.BlockSpec((1, DA_V_DIM), lambda b, h, i: (0, 0))],
        out_specs=pl.BlockSpec((tq, DA_V_DIM), lambda b, h, i: (b * nq + i, h)),
        scratch_shapes=[pltpu.VMEM((tq, 1), F32), pltpu.VMEM((tq, 1), F32), pltpu.VMEM((tq, DA_V_DIM), F32),
                        pltpu.VMEM((tq, 1), F32), pltpu.VMEM((tq, 1), F32), pltpu.VMEM((tq, DA_V_DIM), F32)],
        compiler_params=_cparams("parallel", "parallel", "parallel"),
        name="diff_attention",
    )(lam_params, qk, qk, v, subln_g.reshape(1, DA_V_DIM))


DECAY_SCALE = math.exp(-0.5)


def _rwkv_prep_kernel(r_ref, k_ref, v_ref, lo_ref, pr_ref, pk_ref, pv_ref, plo_ref,
                      mur_ref, muk_ref, muv_ref, mulo_ref, par_ref, wup_ref, aup_ref, gup_ref,
                      at_ref, rt_ref, bt_ref, kt_ref, bg_ref, kg_ref, vb_ref, bonus_ref, g_ref, gam_ref,
                      *, tm, nseq):
    i = pl.program_id(0)
    seq_start = (i % nseq) == 0
    row0 = lax.broadcasted_iota(jnp.int32, (tm, 1), 0) == 0

    def shifted(x_ref, p_ref, mu_ref):
        x = x_ref[...]
        last = jnp.where(seq_start, 0.0, p_ref[SUBLANES - 1:SUBLANES, :])
        prev = jnp.where(row0, last, pltpu.roll(x, 1, 0))
        return x + (prev - x) * mu_ref[...]

    r = shifted(r_ref, pr_ref, mur_ref)
    k = shifted(k_ref, pk_ref, muk_ref)
    v = shifted(v_ref, pv_ref, muv_ref)
    lo = shifted(lo_ref, plo_ref, mulo_ref)
    dw, da, dg = lo[:, :LANES], lo[:, LANES:2 * LANES], lo[:, 2 * LANES:]
    w0, a0, k_k, k_a, r_k = (par_ref[j:j + 1, :] for j in range(5))

    u = w0 + _dot(jnp.tanh(dw).astype(BF16), wup_ref[...])
    ld = -DECAY_SCALE * jax.nn.sigmoid(u)
    a = jax.nn.sigmoid(a0 + _dot(da.astype(BF16), aup_ref[...]))
    g = _dot(jax.nn.sigmoid(dg).astype(BF16), gup_ref[...])

    bd = _block_diag_ones(LANES, RW_HEAD_DIM)
    kk = k * k_k
    kk = kk / jnp.maximum(jnp.sqrt(_group_sum64(kk * kk, bd)), 1e-12)
    k2 = k * (1.0 + (a - 1.0) * k_a)
    bonus = _group_sum64(r * k2 * r_k, bd) * v

    t_i = lax.broadcasted_iota(jnp.int32, (tm, tm), 0)
    s_i = lax.broadcasted_iota(jnp.int32, (tm, tm), 1)
    same = (t_i // CHUNK) == (s_i // CHUNK)
    tri = jnp.where(same & (s_i <= t_i), 1.0, 0.0).astype(BF16)
    rest = jnp.where(same & (s_i > t_i), 1.0, 0.0).astype(BF16)
    c_i = lax.broadcasted_iota(jnp.int32, (tm // CHUNK, tm), 0)
    cs_i = lax.broadcasted_iota(jnp.int32, (tm // CHUNK, tm), 1)
    whole = jnp.where(cs_i // CHUNK == c_i, 1.0, 0.0).astype(BF16)

    ldh, ldm, ldl = _split3(ld)

    def sel(m):
        return _dot(m, ldh) + _dot(m, ldm) + _dot(m, ldl)

    cum = sel(tri)
    e_neg = jnp.exp(-cum)
    e_rem = jnp.exp(sel(rest))
    b = kk * a
    at_ref[...] = (-kk * jnp.exp(cum - ld)).astype(BF16)
    rt_ref[...] = (r * jnp.exp(cum)).astype(BF16)
    bt_ref[...] = (b * e_neg).astype(BF16)
    kt_ref[...] = (k2 * e_neg).astype(BF16)
    bg_ref[...] = (b * e_rem).astype(BF16)
    kg_ref[...] = (k2 * e_rem).astype(BF16)
    vb_ref[...] = v.astype(BF16)
    bonus_ref[...] = bonus.astype(BF16)
    g_ref[...] = g.astype(BF16)
    gam_ref[0] = jnp.exp(sel(whole))


def _rwkv_prep(rw, mu, params, wup, aup, gup, seq, tm, tn):
    t = rw.shape[0]
    nj = RW_WIDTH // tn
    lo_w = 4 * LANES
    lo_blk = (3 * RW_WIDTH) // lo_w
    nseq = seq // tm
    rpb = tm // SUBLANES

    def cur(off):
        return pl.BlockSpec((tm, tn), lambda i, j: (i, off * nj + j))

    def prv(off):
        return pl.BlockSpec((SUBLANES, tn), lambda i, j: (jnp.maximum(i * rpb - 1, 0), off * nj + j))

    def row(off):
        return pl.BlockSpec((1, tn), lambda i, j: (0, off * nj + j))

    in_specs = [cur(0), cur(1), cur(2), pl.BlockSpec((tm, lo_w), lambda i, j: (i, lo_blk)),
                prv(0), prv(1), prv(2),
                pl.BlockSpec((SUBLANES, lo_w), lambda i, j: (jnp.maximum(i * rpb - 1, 0), lo_blk)),
                row(0), row(1), row(2), pl.BlockSpec((1, lo_w), lambda i, j: (0, lo_blk)),
                pl.BlockSpec((SUBLANES, tn), lambda i, j: (0, j)),
                pl.BlockSpec((LANES, tn), lambda i, j: (0, j)),
                pl.BlockSpec((LANES, tn), lambda i, j: (0, j)),
                pl.BlockSpec((2 * LANES, tn), lambda i, j: (0, j))]
    out_blk = pl.BlockSpec((tm, tn), lambda i, j: (i, j))
    outs = [jax.ShapeDtypeStruct((t, RW_WIDTH), BF16)] * 9
    outs.append(jax.ShapeDtypeStruct((t // tm, tm // CHUNK, RW_WIDTH), F32))
    out_specs = [out_blk] * 9 + [pl.BlockSpec((1, tm // CHUNK, tn), lambda i, j: (i, 0, j))]
    return pl.pallas_call(
        functools.partial(_rwkv_prep_kernel, tm=tm, nseq=nseq),
        out_shape=outs,
        grid=(t // tm, nj),
        in_specs=in_specs,
        out_specs=out_specs,
        compiler_params=_cparams("parallel", "parallel"),
        name="rwkv_prep",
    )(rw, rw, rw, rw, rw, rw, rw, rw, mu, mu, mu, mu, params, wup, aup, gup)


PAIR = 2 * RW_HEAD_DIM


def _dot_bf16(a, b, dims=None):
    return _dot(a.astype(BF16), b.astype(BF16), dims)


def _split2(x):
    h = x.astype(BF16)
    return h, (x - h.astype(F32)).astype(BF16)


def _dot_split_lhs(a, b, dims=None):
    ah, al = _split2(a)
    bh = b.astype(BF16)
    return _dot(ah, bh, dims) + _dot(al, bh, dims)


def _dot_split_rhs(a, b, dims=None):
    ah = a.astype(BF16)
    bh, bl = _split2(b)
    return _dot(ah, bh, dims) + _dot(ah, bl, dims)


def _rwkv_chunk_kernel(at_ref, rt_ref, bt_ref, kt_ref, bg_ref, kg_ref, v_ref, bonus_ref, g_ref,
                       gam_ref, lng_ref, lnb_ref, o_ref, s_ref):
    c = pl.program_id(1)

    @pl.when(c == 0)
    def _():
        s_ref[...] = jnp.zeros(s_ref.shape, F32)

    lane = lax.broadcasted_iota(jnp.int32, (1, PAIR), 1)
    first = lane < RW_HEAD_DIM
    rho = lax.broadcasted_iota(jnp.int32, (PAIR, PAIR), 0)
    sig = lax.broadcasted_iota(jnp.int32, (PAIR, PAIR), 1)
    strict, incl, eye = sig < rho, sig <= rho, sig == rho
    own = (rho // RW_HEAD_DIM) == (sig // RW_HEAD_DIM)

    def stacked(x):
        z = jnp.zeros_like(x)
        return jnp.concatenate([jnp.where(first, x, z), jnp.where(first, z, x)], axis=0)

    n_pairs = RW_HEADS // 2
    pairs = range(n_pairs)
    sls = [slice(p * PAIR, (p + 1) * PAIR) for p in pairs]
    zero = jnp.zeros((PAIR, PAIR), F32)
    st = [[stacked(ref[:, sl]) for ref in (at_ref, rt_ref, bt_ref, kt_ref, bg_ref, kg_ref, v_ref)] for sl in sls]
    prods = [_dot(jnp.concatenate([q[0], q[1]], axis=0), jnp.concatenate([q[2], q[3]], axis=0), NT_DIMS) for q in st]
    lmat = [jnp.where(strict, pr_[:PAIR, :PAIR], zero) for pr_ in prods]
    sak = [jnp.where(strict, pr_[:PAIR, PAIR:], zero) for pr_ in prods]
    lrbk = [jnp.concatenate([jnp.where(incl, pr_[PAIR:, :PAIR], zero), jnp.where(incl, pr_[PAIR:, PAIR:], zero)],
                            axis=1).astype(BF16) for pr_ in prods]
    akv = [_dot(sak[p].astype(BF16), st[p][6]) for p in pairs]
    x = [jnp.concatenate([st[p][0].astype(F32), akv[p]], axis=1) for p in pairs]
    li = lmat
    n_steps = CHUNK.bit_length() - 1
    for it in range(n_steps):
        if it + 1 < n_steps:
            res = [_dot_bf16(li[p], jnp.concatenate([li[p], x[p]], axis=1)) for p in pairs]
            x = [x[p] + res[p][:, PAIR:] for p in pairs]
            li = [res[p][:, :PAIR] for p in pairs]
        else:
            res = [_dot_bf16(li[p], x[p]) for p in pairs]
            x = [x[p] + res[p] for p in pairs]
    gmat = [jnp.concatenate([x[p], jnp.concatenate([zero, st[p][6].astype(F32)], axis=1)], axis=0).astype(BF16)
            for p in pairs]
    out1 = [_dot(lrbk[p], gmat[p]) for p in pairs]
    out2 = [_dot(gmat[p], jnp.concatenate([st[p][4], st[p][5]], axis=0), TN_DIMS) for p in pairs]
    s0 = [s_ref[p] for p in pairs]
    qe = [out1[p][:, :PAIR] + st[p][1].astype(F32) for p in pairs]
    y = [_dot_split_rhs(qe[p], s0[p], NT_DIMS) + out1[p][:, PAIR:] for p in pairs]
    for p in pairs:
        mmat = out2[p][:PAIR] + jnp.where(eye, gam_ref[0][:, sls[p]], 0.0)
        s_ref[p] = _dot_split_lhs(s0[p], mmat) + out2[p][PAIR:]
    for p in pairs:
        sl = sls[p]
        mean = jnp.sum(y[p], axis=-1, keepdims=True) * (1.0 / RW_HEAD_DIM)
        d = jnp.where(own, y[p] - mean, 0.0)
        var = jnp.sum(d * d, axis=-1, keepdims=True) * (1.0 / RW_HEAD_DIM)
        yn = d * lax.rsqrt(var + GN_EPS)
        yn = yn[:CHUNK] + yn[CHUNK:]
        out = (yn * lng_ref[:, sl] + lnb_ref[:, sl] + bonus_ref[:, sl].astype(F32)) * g_ref[:, sl].astype(F32)
        o_ref[:, sl] = out.astype(o_ref.dtype)


def _rwkv_chunks(prep, lnx_g, lnx_b, batch, seq):
    at, rt, bt, kt, bg, kg, vb, bonus, g, gam = prep
    t = at.shape[0]
    nc = seq // CHUNK
    gam = gam.reshape(t // CHUNK, 1, RW_WIDTH)
    blk = pl.BlockSpec((CHUNK, RW_WIDTH), lambda b, c: (b * nc + c, 0))
    rowspec = pl.BlockSpec((1, RW_WIDTH), lambda b, c: (0, 0))
    return pl.pallas_call(
        _rwkv_chunk_kernel,
        out_shape=jax.ShapeDtypeStruct((t, RW_WIDTH), BF16),
        grid=(batch, nc),
        in_specs=[blk] * 9 + [pl.BlockSpec((1, 1, RW_WIDTH), lambda b, c: (b * nc + c, 0, 0)), rowspec, rowspec],
        out_specs=blk,
        scratch_shapes=[pltpu.VMEM((RW_HEADS // 2, PAIR, PAIR), F32)],
        compiler_params=_cparams("arbitrary", "arbitrary"),
        name="rwkv_chunks",
    )(at, rt, bt, kt, bg, kg, vb, bonus, g, gam, lnx_g.reshape(1, RW_WIDTH), lnx_b.reshape(1, RW_WIDTH))


def _merge_kernel(h_ref, ya_ref, yb_ref, wga_ref, wgb_ref, pa_ref, pb_ref, o_ref):
    h = h_ref[...]
    ga = jax.nn.sigmoid(_dot(h, wga_ref[...]))
    gb = jax.nn.sigmoid(_dot(h, wgb_ref[...]))
    o_ref[...] = (ga * _dot(ya_ref[...], pa_ref[...]) + gb * _dot(yb_ref[...], pb_ref[...])).astype(o_ref.dtype)


def _merge(h, ya, yb, wga, wgb, pa, pb, tm, tn):
    t, d = h.shape
    n = wga.shape[1]
    ka, kb = ya.shape[1], yb.shape[1]
    return pl.pallas_call(
        _merge_kernel,
        out_shape=jax.ShapeDtypeStruct((t, n), BF16),
        grid=(t // tm, n // tn),
        in_specs=[pl.BlockSpec((tm, d), lambda i, j: (i, 0)),
                  pl.BlockSpec((tm, ka), lambda i, j: (i, 0)),
                  pl.BlockSpec((tm, kb), lambda i, j: (i, 0)),
                  pl.BlockSpec((d, tn), lambda i, j: (0, j)),
                  pl.BlockSpec((d, tn), lambda i, j: (0, j)),
                  pl.BlockSpec((ka, tn), lambda i, j: (0, j)),
                  pl.BlockSpec((kb, tn), lambda i, j: (0, j))],
        out_specs=pl.BlockSpec((tm, tn), lambda i, j: (i, j)),
        compiler_params=_cparams("parallel", "parallel"),
        name="gated_merge",
    )(h, ya, yb, wga, wgb, pa, pb)


def _first_lane_where(cond, lane):
    return jnp.min(jnp.where(cond, lane, LANES), axis=-1, keepdims=True)


def _outproj_kernel(m_ref, x_ref, wo_ref, g2_ref, wrh_ref, wrl_ref, rb_ref, x1_ref, h2_ref, route_ref):
    x1 = x_ref[...] + _dot(m_ref[...], wo_ref[...])
    x1_ref[...] = x1
    ms = jnp.mean(x1 * x1, axis=-1, keepdims=True)
    h2 = x1 * lax.rsqrt(ms + NORM_EPS) * g2_ref[...]
    h2_ref[...] = h2
    hh = h2.astype(BF16)
    hl = (h2 - hh.astype(F32)).astype(BF16)
    wrh = wrh_ref[...]
    lg = _dot(hh, wrh) + _dot(hl, wrh) + _dot(hh, wrl_ref[...]) + rb_ref[...]

    lane = lax.broadcasted_iota(jnp.int32, lg.shape, 1)
    neg = -jnp.inf
    is_g = lane < N_GROUPS
    mg = jnp.max(jnp.where(is_g, lg, neg), axis=-1, keepdims=True)
    eg = jnp.where(is_g, jnp.exp(lg - mg), 0.0)
    pg = eg / jnp.sum(eg, axis=-1, keepdims=True)
    p_g_top = jnp.max(pg, axis=-1, keepdims=True)
    g_idx = _first_lane_where(is_g & (pg == p_g_top), lane)
    lo = N_GROUPS + g_idx * EXPERTS_PER_GROUP
    sel = (lane >= lo) & (lane < lo + EXPERTS_PER_GROUP)
    me = jnp.max(jnp.where(sel, lg, neg), axis=-1, keepdims=True)
    ee = jnp.where(sel, jnp.exp(lg - me), 0.0)
    pe = ee / jnp.sum(ee, axis=-1, keepdims=True)
    pe = jnp.where(sel, pe, -1.0)
    v1 = jnp.max(pe, axis=-1, keepdims=True)
    i1 = _first_lane_where(pe == v1, lane)
    pe2 = jnp.where(lane == i1, -1.0, pe)
    v2 = jnp.max(pe2, axis=-1, keepdims=True)
    i2 = _first_lane_where(pe2 == v2, lane)
    den = v1 + v2
    route = jnp.where(lane == 0, p_g_top * v1 / den,
                      jnp.where(lane == 1, p_g_top * v2 / den,
                                jnp.where(lane == 2, (i1 - N_GROUPS).astype(F32),
                                          jnp.where(lane == 3, (i2 - N_GROUPS).astype(F32), 0.0))))
    route_ref[...] = route


def _outproj(merged, x, wo, g2, wr_hi, wr_lo, rbias, tm):
    t, d = x.shape
    nr = wr_hi.shape[1]
    return pl.pallas_call(
        _outproj_kernel,
        out_shape=[jax.ShapeDtypeStruct((t, d), F32), jax.ShapeDtypeStruct((t, d), F32),
                   jax.ShapeDtypeStruct((t, nr), F32)],
        grid=(t // tm,),
        in_specs=[pl.BlockSpec((tm, d), lambda i: (i, 0)),
                  pl.BlockSpec((tm, d), lambda i: (i, 0)),
                  pl.BlockSpec((d, d), lambda i: (0, 0)),
                  pl.BlockSpec((1, d), lambda i: (0, 0)),
                  pl.BlockSpec((d, nr), lambda i: (0, 0)),
                  pl.BlockSpec((d, nr), lambda i: (0, 0)),
                  pl.BlockSpec((1, nr), lambda i: (0, 0))],
        out_specs=[pl.BlockSpec((tm, d), lambda i: (i, 0)),
                   pl.BlockSpec((tm, d), lambda i: (i, 0)),
                   pl.BlockSpec((tm, nr), lambda i: (i, 0))],
        compiler_params=_cparams("parallel"),
        name="outproj_norm_router",
    )(merged, x, wo, g2.reshape(1, d), wr_hi, wr_lo, rbias)


IDX_SLOTS = 3


def _moe_kernel(be_ref, nused_ref, nvalid_ref, tok_hbm, dst_hbm, h_hbm, wg_ref, wu_ref, wd_ref, y_hbm,
                xbuf, obuf, tok_s, dst_s, idx_sem, g_sem, s_sem, *, rows):
    b = pl.program_id(0)
    n_used = nused_ref[0]
    last = n_used - 1

    def idx_copies(blk, sl):
        return (pltpu.make_async_copy(tok_hbm.at[blk], tok_s.at[sl], idx_sem.at[sl, 0]),
                pltpu.make_async_copy(dst_hbm.at[blk], dst_s.at[sl], idx_sem.at[sl, 1]))

    def gather_row(r, isl, xsl):
        return pltpu.make_async_copy(h_hbm.at[tok_s[isl, r]], xbuf.at[xsl, r], g_sem.at[xsl])

    def gather_all(xsl):
        return pltpu.make_async_copy(h_hbm.at[pl.ds(0, rows)], xbuf.at[xsl], g_sem.at[xsl])

    def scatter_row(r, isl, osl):
        return pltpu.make_async_copy(obuf.at[osl, r], y_hbm.at[dst_s[isl, r]], s_sem.at[osl])

    def wait_scatter(blk, osl):
        n = nvalid_ref[blk]

        @pl.when(n == rows)
        def _():
            pltpu.make_async_copy(obuf.at[osl], y_hbm.at[pl.ds(0, rows)], s_sem.at[osl]).wait()

        @pl.when(n < rows)
        def _():
            def one(r, c):
                pltpu.make_async_copy(obuf.at[osl, 0], y_hbm.at[0], s_sem.at[osl]).wait()
                return c

            lax.fori_loop(0, n, one, 0)

    @pl.when(b == 0)
    def _():
        for cp in idx_copies(0, 0):
            cp.start()
        for cp in idx_copies(0, 0):
            cp.wait()
        for cp in idx_copies(jnp.minimum(1, last), 1):
            cp.start()

        def issue(r, c):
            gather_row(r, 0, 0).start()
            return c

        lax.fori_loop(0, rows, issue, 0, unroll=8)

    @pl.when((b > 0) & (b < n_used))
    def _():
        wait_scatter(b - 1, (b - 1) % 2)

    @pl.when(b < n_used)
    def _():
        cur_i, nxt_i, ld_i = b % IDX_SLOTS, (b + 1) % IDX_SLOTS, (b + 2) % IDX_SLOTS
        cur_x, nxt_x = b % 2, (b + 1) % 2
        for cp in idx_copies(0, nxt_i):
            cp.wait()
        for r in range(rows):
            gather_row(r, nxt_i, nxt_x).start()
        for cp in idx_copies(jnp.minimum(b + 2, last), ld_i):
            cp.start()

        gather_all(cur_x).wait()
        x = xbuf[cur_x].astype(BF16)
        gate = _dot(x, wg_ref[...])
        up = _dot(x, wu_ref[...])
        mid = (gate * jax.nn.sigmoid(gate) * up).astype(BF16)
        obuf[cur_x] = _dot(mid, wd_ref[...])

    @pl.when((b < n_used) & (nvalid_ref[b] == rows))
    def _():
        for r in range(rows):
            scatter_row(r, b % IDX_SLOTS, b % 2).start()

    @pl.when((b < n_used) & (nvalid_ref[b] < rows))
    def _():
        def issue(r, c):
            scatter_row(r, b % IDX_SLOTS, b % 2).start()
            return c

        lax.fori_loop(0, nvalid_ref[b], issue, 0)

    @pl.when(b == last)
    def _():
        wait_scatter(b, b % 2)
        gather_all((b + 1) % 2).wait()
        for cp in idx_copies(0, (b + 2) % IDX_SLOTS):
            cp.wait()


def _moe(h2, row_tok, row_dst, block_e, n_used, nvalid, wg, wu, wd, rows):
    t, d = h2.shape
    n_blocks = row_tok.shape[0]
    a = t * TOP_K
    f = wg.shape[2]
    kern = functools.partial(_moe_kernel, rows=rows)
    grid_spec = pltpu.PrefetchScalarGridSpec(
        num_scalar_prefetch=3,
        grid=(n_blocks,),
        in_specs=[pl.BlockSpec(memory_space=pl.ANY),
                  pl.BlockSpec(memory_space=pl.ANY),
                  pl.BlockSpec(memory_space=pl.ANY),
                  pl.BlockSpec((None, d, f), lambda b, be, nu, nv: (be[b], 0, 0)),
                  pl.BlockSpec((None, d, f), lambda b, be, nu, nv: (be[b], 0, 0)),
                  pl.BlockSpec((None, f, d), lambda b, be, nu, nv: (be[b], 0, 0))],
        out_specs=pl.BlockSpec(memory_space=pl.ANY),
        scratch_shapes=[pltpu.VMEM((2, rows, d), F32),
                        pltpu.VMEM((2, rows, d), F32),
                        pltpu.SMEM((IDX_SLOTS, rows), jnp.int32),
                        pltpu.SMEM((IDX_SLOTS, rows), jnp.int32),
                        pltpu.SemaphoreType.DMA((IDX_SLOTS, 2)),
                        pltpu.SemaphoreType.DMA((2,)),
                        pltpu.SemaphoreType.DMA((2,))],
    )
    return pl.pallas_call(
        kern,
        out_shape=jax.ShapeDtypeStruct((a, d), F32),
        grid_spec=grid_spec,
        compiler_params=_cparams("arbitrary"),
        name="moe_experts",
    )(block_e, n_used, nvalid, row_tok, row_dst, h2, wg, wu, wd)


def _combine_kernel(x1_ref, y0_ref, y1_ref, route_ref, o_ref):
    gts = route_ref[...]
    o_ref[...] = x1_ref[...] + gts[:, 0:1] * y0_ref[...] + gts[:, 1:2] * y1_ref[...]


def _combine(x1, y, route, tm):
    t, d = x1.shape
    nb = t // tm
    return pl.pallas_call(
        _combine_kernel,
        out_shape=jax.ShapeDtypeStruct((t, d), F32),
        grid=(nb,),
        in_specs=[pl.BlockSpec((tm, d), lambda i: (i, 0)),
                  pl.BlockSpec((tm, d), lambda i: (i, 0)),
                  pl.BlockSpec((tm, d), lambda i: (nb + i, 0)),
                  pl.BlockSpec((tm, LANES), lambda i: (i, 0))],
        out_specs=pl.BlockSpec((tm, d), lambda i: (i, 0)),
        compiler_params=_cparams("parallel"),
        name="moe_combine",
    )(x1, y, y, route)


def _routing_tables(route, rows):
    t = route.shape[0]
    a = t * TOP_K
    expert = route[:, 2:2 + TOP_K].astype(jnp.int32)
    flat_e = expert.reshape(a)
    ids = jnp.arange(a, dtype=jnp.int32)
    order = jnp.sort(flat_e * a + ids) % a
    counts = jnp.sum((flat_e[:, None] == jnp.arange(N_EXPERTS, dtype=jnp.int32)[None, :]).astype(jnp.int32), axis=0)
    padded = (counts + rows - 1) // rows * rows
    start = jnp.cumsum(counts) - counts
    pend = jnp.cumsum(padded)
    pstart = pend - padded
    n_rows = a + N_EXPERTS * rows
    n_blocks = n_rows // rows
    blk_row0 = jnp.arange(n_blocks, dtype=jnp.int32) * rows
    block_e = jnp.minimum(jnp.sum((pend[None, :] <= blk_row0[:, None]).astype(jnp.int32), axis=1), N_EXPERTS - 1)
    rho = jnp.arange(n_rows, dtype=jnp.int32)
    be = jnp.repeat(block_e, rows)
    off = rho - pstart[be]
    valid = (off >= 0) & (off < counts[be])
    src = order[jnp.clip(start[be] + off, 0, a - 1)]
    row_tok = jnp.where(valid, src // TOP_K, 0)
    row_dst = jnp.where(valid, (src % TOP_K) * t + src // TOP_K, 0)
    n_used = (pend[-1] // rows).astype(jnp.int32).reshape(1)
    nvalid = jnp.clip(counts[block_e] - (blk_row0 - pstart[block_e]), 0, rows).astype(jnp.int32)
    return (row_tok.reshape(n_blocks, rows), row_dst.reshape(n_blocks, rows), block_e.astype(jnp.int32), n_used, nvalid)


def _tiles(t, seq):
    return dict(
        norm_tm=min(512, t),
        proj_tm=min(1024, seq), proj_tn=512,
        attn_tq=min(256, seq), attn_tk=min(1024, seq),
        prep_tm=min(256, seq), prep_tn=256,
        merge_tm=min(1024, t), merge_tn=512,
        out_tm=min(256, t),
        moe_rows=256,
        comb_tm=min(512, t),
    )


def _pad_rows(w, n):
    return jnp.pad(w, ((0, n - w.shape[0]), (0, 0)))


def _pad_cols(w, n):
    return jnp.pad(w, ((0, 0), (0, n - w.shape[1])))


def kernel(x, norm1_g, w_in, q_norm_g, k_norm_g, lam_q1, lam_k1, lam_q2, lam_k2, subln_g, shift_mu, w0, w_up, a0, a_up, g_up, k_k, k_a, r_k, lnx_g, lnx_b, proj_a, proj_b, w_out, norm2_g, router_g, router_g_b, router_e, router_e_b, w_gate_e, w_up_e, w_down_e):
    batch, seq, d = x.shape
    t = batch * seq
    depth = norm1_g.shape[0]
    tl = _tiles(t, seq)
    qkw = DA_HEADS * 2 * DA_HEAD_DIM
    vw = DA_HEADS * DA_V_DIM
    c_q, c_k, c_v = 0, qkw, 2 * qkw
    c_rw = c_v + vw
    c_dw = c_rw + 3 * RW_WIDTH
    c_da = c_dw + DECAY_LORA
    c_dg = c_da + AAA_LORA
    c_ga = c_dg + GATE_LORA
    c_gb = c_ga + d
    cos_t, sinm_t, sinp_t = _rope_tables(seq)
    xf = x.reshape(t, d)

    for l in range(depth):
        lam_init = 0.8 - 0.6 * math.exp(-0.3 * l)
        wl = w_in[l]
        w_qk = wl[:, c_q:c_v].astype(BF16)
        w_v = wl[:, c_v:c_rw].astype(BF16)
        w_rw = jnp.concatenate([wl[:, c_rw:c_dw], _pad_cols(wl[:, c_dw:c_da], LANES),
                                _pad_cols(wl[:, c_da:c_dg], LANES), wl[:, c_dg:c_ga]], axis=1).astype(BF16)
        w_ga = wl[:, c_ga:c_gb].astype(BF16)
        w_gb = wl[:, c_gb:].astype(BF16)
        mu = shift_mu[l]
        o_dw = 3 * RW_WIDTH
        mu_l = jnp.concatenate([mu[:o_dw], jnp.pad(mu[o_dw:o_dw + DECAY_LORA], (0, LANES - DECAY_LORA)),
                                jnp.pad(mu[o_dw + DECAY_LORA:o_dw + DECAY_LORA + AAA_LORA], (0, LANES - AAA_LORA)),
                                mu[o_dw + DECAY_LORA + AAA_LORA:]]).reshape(1, -1)
        gain_row = jnp.concatenate([jnp.tile(q_norm_g[l], 2 * DA_HEADS) * (DA_HEAD_DIM ** -0.5),
                                    jnp.tile(k_norm_g[l], 2 * DA_HEADS)]).reshape(1, 2 * qkw)
        lam_params = jnp.stack([lam_q1[l], lam_k1[l], lam_q2[l], lam_k2[l]])
        zrow = jnp.zeros((RW_WIDTH,), F32)
        rw_params = jnp.stack([w0[l], a0[l], k_k[l], k_a[l], r_k[l].reshape(-1), zrow, zrow, zrow])

        h = _rmsnorm(xf, norm1_g[l], NORM_EPS, tl["norm_tm"])
        qk = _qk_proj(h, w_qk, gain_row, cos_t, sinm_t, sinp_t, seq, tl["proj_tm"], tl["proj_tn"])
        v = _matmul(h, w_v, BF16, tl["proj_tm"], tl["proj_tn"], "v_proj")
        ya = _diff_attention(qk, v, lam_params, subln_g[l], batch, seq, lam_init, tl["attn_tq"], tl["attn_tk"])

        rw = _matmul(h, w_rw, F32, tl["proj_tm"], tl["proj_tn"], "rw_proj")
        prep = _rwkv_prep(rw, mu_l, rw_params, _pad_rows(w_up[l], LANES).astype(BF16),
                          _pad_rows(a_up[l], LANES).astype(BF16), g_up[l].astype(BF16),
                          seq, tl["prep_tm"], tl["prep_tn"])
        yb = _rwkv_chunks(prep, lnx_g[l], lnx_b[l], batch, seq)

        merged = _merge(h, ya, yb, w_ga, w_gb, proj_a[l].astype(BF16), proj_b[l].astype(BF16),
                        tl["merge_tm"], tl["merge_tn"])
        wr = _pad_cols(jnp.concatenate([router_g[l], router_e[l]], axis=1), LANES)
        wr_hi = wr.astype(BF16)
        wr_lo = (wr - wr_hi.astype(F32)).astype(BF16)
        rbias = jnp.pad(jnp.concatenate([router_g_b[l], router_e_b[l]]), (0, LANES - N_GROUPS - N_EXPERTS))
        x1, h2, route = _outproj(merged, xf, w_out[l].astype(BF16), norm2_g[l], wr_hi, wr_lo,
                                 rbias.reshape(1, LANES), tl["out_tm"])

        rows = tl["moe_rows"]
        row_tok, row_dst, block_e, n_used, nvalid = _routing_tables(route, rows)
        y = _moe(h2, row_tok, row_dst, block_e, n_used, nvalid, w_gate_e[l].astype(BF16), w_up_e[l].astype(BF16),
                 w_down_e[l].astype(BF16), rows)
        xf = _combine(x1, y, route, tl["comb_tm"])
    return xf.reshape(batch, seq, d)
```

```python
import functools
import math

import jax
import jax.numpy as jnp
from jax import lax
from jax.experimental import pallas as pl
from jax.experimental.pallas import tpu as pltpu

DA_HEADS = 8
DA_HEAD_DIM = 64
DA_V_DIM = 2 * DA_HEAD_DIM
ROT_DIM = DA_HEAD_DIM // 4
ROPE_THETA = 500000.0
SUBLN_EPS = 1e-5
RW_HEADS = 16
RW_HEAD_DIM = 64
RW_WIDTH = RW_HEADS * RW_HEAD_DIM
DECAY_LORA = 96
AAA_LORA = 96
GATE_LORA = 256
GN_EPS = 64e-5
N_GROUPS = 4
EXPERTS_PER_GROUP = 8
N_EXPERTS = N_GROUPS * EXPERTS_PER_GROUP
TOP_K = 2
NORM_EPS = 1e-6

LANES = 128
SUBLANES = 8
VMEM_LIMIT_BYTES = 56 * 1024 * 1024

CHUNK = 64
CAST_BLOCK_BYTES = 2 * 1024 * 1024

F32 = jnp.float32
BF16 = jnp.bfloat16

NT_DIMS = (((1,), (1,)), ((), ()))
TN_DIMS = (((0,), (0,)), ((), ()))


def _cparams(*sem):
    return pltpu.CompilerParams(dimension_semantics=tuple(sem), vmem_limit_bytes=VMEM_LIMIT_BYTES)


def _dot(a, b, dims=None, precision=None):
    if dims is None:
        return jnp.dot(a, b, preferred_element_type=F32, precision=precision)
    return lax.dot_general(a, b, dims, preferred_element_type=F32, precision=precision)


def _split3(x):
    h = x.astype(BF16)
    r = x - h.astype(F32)
    m = r.astype(BF16)
    l = (r - m.astype(F32)).astype(BF16)
    return h, m, l


def _dot_exact_rhs(sel_bf16, x_f32):
    h, m, l = _split3(x_f32)
    return _dot(sel_bf16, h) + _dot(sel_bf16, m) + _dot(sel_bf16, l)


def _group_sum64(x, bd, terms):
    parts = _split3(x)[:terms]
    outs = []
    for s in range(x.shape[1] // LANES):
        sl = slice(s * LANES, (s + 1) * LANES)
        acc = _dot(parts[0][:, sl], bd)
        for part in parts[1:]:
            acc = acc + _dot(part[:, sl], bd)
        outs.append(acc)
    return outs[0] if len(outs) == 1 else jnp.concatenate(outs, axis=1)


def _block_diag_ones(n, blk, dtype=BF16):
    r = lax.broadcasted_iota(jnp.int32, (n, n), 0) // blk
    c = lax.broadcasted_iota(jnp.int32, (n, n), 1) // blk
    return jnp.where(r == c, 1.0, 0.0).astype(dtype)


def _rmsnorm_kernel(x_ref, g_ref, o_ref, *, eps):
    x = x_ref[...]
    ms = jnp.mean(x * x, axis=-1, keepdims=True)
    o_ref[...] = (x * lax.rsqrt(ms + eps) * g_ref[...]).astype(o_ref.dtype)


def _rmsnorm(x, g, eps, tm):
    t, d = x.shape
    return pl.pallas_call(
        functools.partial(_rmsnorm_kernel, eps=eps),
        out_shape=jax.ShapeDtypeStruct((t, d), BF16),
        grid=(t // tm,),
        in_specs=[pl.BlockSpec((tm, d), lambda i: (i, 0)),
                  pl.BlockSpec((1, d), lambda i: (0, 0))],
        out_specs=pl.BlockSpec((tm, d), lambda i: (i, 0)),
        compiler_params=_cparams("parallel"),
        name="rmsnorm",
    )(x, g.reshape(1, d))


def _matmul_kernel(a_ref, w_ref, o_ref):
    o_ref[...] = _dot(a_ref[...], w_ref[...]).astype(o_ref.dtype)


def _matmul(a, w, out_dtype, tm, tn, name):
    t, k = a.shape
    n = w.shape[1]
    return pl.pallas_call(
        _matmul_kernel,
        out_shape=jax.ShapeDtypeStruct((t, n), out_dtype),
        grid=(t // tm, n // tn),
        in_specs=[pl.BlockSpec((tm, k), lambda i, j: (i, 0)),
                  pl.BlockSpec((k, tn), lambda i, j: (0, j))],
        out_specs=pl.BlockSpec((tm, tn), lambda i, j: (i, j)),
        compiler_params=_cparams("parallel", "parallel"),
        name=name,
    )(a, w)


def _qk_proj_kernel(a_ref, w_ref, gain_ref, cos_ref, sinm_ref, sinp_ref, o_ref, *, tn):
    acc = _dot(a_ref[...], w_ref[...])
    bd = _block_diag_ones(LANES, DA_HEAD_DIM)
    ms = _group_sum64(acc * acc, bd, 1) * (1.0 / DA_HEAD_DIM)
    xn = acc * lax.rsqrt(ms + NORM_EPS) * gain_ref[...]
    reps = tn // LANES
    cos = jnp.tile(cos_ref[...], (1, reps))
    sinm = jnp.tile(sinm_ref[...], (1, reps))
    sinp = jnp.tile(sinp_ref[...], (1, reps))
    half = ROT_DIM // 2
    hi = pltpu.roll(xn, tn - half, 1)
    lo = pltpu.roll(xn, half, 1)
    o_ref[...] = (xn * cos + hi * sinm + lo * sinp).astype(o_ref.dtype)


def _qk_proj(h, w_qk, gain_row, cos_t, sinm_t, sinp_t, seq, tm, tn):
    t, k = h.shape
    n = w_qk.shape[1]
    nseq = seq // tm
    return pl.pallas_call(
        functools.partial(_qk_proj_kernel, tn=tn),
        out_shape=jax.ShapeDtypeStruct((t, n), BF16),
        grid=(t // tm, n // tn),
        in_specs=[pl.BlockSpec((tm, k), lambda i, j: (i, 0)),
                  pl.BlockSpec((k, tn), lambda i, j: (0, j)),
                  pl.BlockSpec((1, tn), lambda i, j: (0, j)),
                  pl.BlockSpec((tm, LANES), lambda i, j: (i % nseq, 0)),
                  pl.BlockSpec((tm, LANES), lambda i, j: (i % nseq, 0)),
                  pl.BlockSpec((tm, LANES), lambda i, j: (i % nseq, 0))],
        out_specs=pl.BlockSpec((tm, tn), lambda i, j: (i, j)),
        compiler_params=_cparams("parallel", "parallel"),
        name="qk_proj",
    )(h, w_qk, gain_row, cos_t, sinm_t, sinp_t)


def _rope_tables(seq):
    half = ROT_DIM // 2
    inv_freq = ROPE_THETA ** (-jnp.arange(0, ROT_DIM, 2, dtype=F32) / ROT_DIM)
    ang = jnp.arange(seq, dtype=F32)[:, None] * inv_freq[None, :]
    cos, sin = jnp.cos(ang), jnp.sin(ang)
    ones = jnp.ones((seq, DA_HEAD_DIM - ROT_DIM), F32)
    zeros = jnp.zeros((seq, DA_HEAD_DIM - ROT_DIM), F32)
    zh = jnp.zeros((seq, half), F32)
    cos64 = jnp.concatenate([cos, cos, ones], axis=1)
    sinm64 = jnp.concatenate([-sin, zh, zeros], axis=1)
    sinp64 = jnp.concatenate([zh, sin, zeros], axis=1)
    return tuple(jnp.concatenate([a, a], axis=1) for a in (cos64, sinm64, sinp64))


def _diff_attn_kernel(lam_ref, q_ref, k_ref, v_ref, g_ref, o_ref, m_ref, l_ref, a_ref, *, tq, tk, lam_init, nh):
    i = pl.program_id(2)
    lane = lax.broadcasted_iota(jnp.int32, (1, DA_V_DIM), 1)
    qs = []
    for h in range(nh):
        q = q_ref[:, h * DA_V_DIM:(h + 1) * DA_V_DIM]
        zero = jnp.zeros_like(q)
        qs.append(jnp.where(lane < DA_HEAD_DIM, q, zero))
        qs.append(jnp.where(lane >= DA_HEAD_DIM, q, zero))
    nc = 2 * nh
    m_ref[...] = jnp.full(m_ref.shape, -jnp.inf, F32)
    l_ref[...] = jnp.zeros(l_ref.shape, F32)
    a_ref[...] = jnp.zeros(a_ref.shape, F32)

    def step(j, mask):
        off = pl.multiple_of(j * tk, tk)
        ks = [k_ref[pl.ds(off, tk), h * DA_V_DIM:(h + 1) * DA_V_DIM] for h in range(nh)]
        vs = [v_ref[pl.ds(off, tk), h * DA_V_DIM:(h + 1) * DA_V_DIM] for h in range(nh)]
        s = [_dot(qs[c], ks[c // 2], NT_DIMS) for c in range(nc)]
        for c in range(nc):
            sc = s[c] if mask is None else jnp.where(mask, s[c], -jnp.inf)
            m_old = m_ref[c]
            m_new = jnp.maximum(m_old, jnp.max(sc, axis=-1, keepdims=True))
            alpha = jnp.exp(m_old - m_new)
            p = jnp.exp(sc - m_new)
            l_ref[c] = alpha * l_ref[c] + jnp.sum(p, axis=-1, keepdims=True)
            a_ref[c] = alpha * a_ref[c] + _dot(p.astype(BF16), vs[c // 2])
            m_ref[c] = m_new

    n_full = (i * tq) // tk

    def full_body(j, c):
        step(j, None)
        return c

    lax.fori_loop(0, n_full, full_body, 0)

    row = i * tq + lax.broadcasted_iota(jnp.int32, (tq, tk), 0)
    for d in range(max(1, tq // tk)):
        j = n_full + d
        col = j * tk + lax.broadcasted_iota(jnp.int32, (tq, tk), 1)
        step(j, col <= row)

    lq1, lk1, lq2, lk2 = (lam_ref[r:r + 1, :] for r in range(4))
    lam = (jnp.exp(jnp.sum(lq1 * lk1, axis=-1, keepdims=True))
           - jnp.exp(jnp.sum(lq2 * lk2, axis=-1, keepdims=True)) + lam_init)
    for h in range(nh):
        o = a_ref[2 * h] / l_ref[2 * h] - lam * (a_ref[2 * h + 1] / l_ref[2 * h + 1])
        ms = jnp.mean(o * o, axis=-1, keepdims=True)
        o = o * lax.rsqrt(ms + SUBLN_EPS) * (g_ref[...] * (1.0 - lam_init))
        o_ref[:, h * DA_V_DIM:(h + 1) * DA_V_DIM] = o.astype(o_ref.dtype)


def _diff_attention(qk, v, lam_params, subln_g, batch, seq, lam_init, tq, tk, nh):
    t = qk.shape[0]
    nq = seq // tq
    w = nh * DA_V_DIM
    hg = DA_HEADS // nh
    kern = functools.partial(_diff_attn_kernel, tq=tq, tk=tk, lam_init=lam_init, nh=nh)
    return pl.pallas_call(
        kern,
        out_shape=jax.ShapeDtypeStruct((t, DA_HEADS * DA_V_DIM), BF16),
        grid=(batch, hg, nq),
        in_specs=[pl.BlockSpec((4, DA_HEAD_DIM), lambda b, h, i: (0, 0)),
                  pl.BlockSpec((tq, w), lambda b, h, i: (b * nq + i, h)),
                  pl.BlockSpec((seq, w), lambda b, h, i: (b, hg + h)),
                  pl.BlockSpec((seq, w), lambda b, h, i: (b, h)),
                  pl.BlockSpec((1, DA_V_DIM), lambda b, h, i: (0, 0))],
        out_specs=pl.BlockSpec((tq, w), lambda b, h, i: (b * nq + i, h)),
        scratch_shapes=[pltpu.VMEM((2 * nh, tq, 1), F32), pltpu.VMEM((2 * nh, tq, 1), F32),
                        pltpu.VMEM((2 * nh, tq, DA_V_DIM), F32)],
        compiler_params=_cparams("parallel", "parallel", "parallel"),
        name="diff_attention",
    )(lam_params, qk, qk, v, subln_g.reshape(1, DA_V_DIM))


DECAY_SCALE = math.exp(-0.5)


def _rwkv_prep_kernel(r_ref, k_ref, v_ref, lo_ref, pr_ref, pk_ref, pv_ref, plo_ref,
                      mur_ref, muk_ref, muv_ref, mulo_ref, par_ref, wup_ref, aup_ref, gup_ref,
                      at_ref, rt_ref, bt_ref, kt_ref, bg_ref, kg_ref, vb_ref, bonus_ref, g_ref, gam_ref,
                      *, tm, nseq):
    i = pl.program_id(0)
    seq_start = (i % nseq) == 0
    row0 = lax.broadcasted_iota(jnp.int32, (tm, 1), 0) == 0

    def shifted(x_ref, p_ref, mu_ref):
        x = x_ref[...]
        last = jnp.where(seq_start, 0.0, p_ref[SUBLANES - 1:SUBLANES, :])
        prev = jnp.where(row0, last, pltpu.roll(x, 1, 0))
        return x + (prev - x) * mu_ref[...]

    r = shifted(r_ref, pr_ref, mur_ref)
    k = shifted(k_ref, pk_ref, muk_ref)
    v = shifted(v_ref, pv_ref, muv_ref)
    lo = shifted(lo_ref, plo_ref, mulo_ref)
    dw, da, dg = lo[:, :LANES], lo[:, LANES:2 * LANES], lo[:, 2 * LANES:]
    w0, a0, k_k, k_a, r_k = (par_ref[j:j + 1, :] for j in range(5))

    u = w0 + _dot(jnp.tanh(dw).astype(BF16), wup_ref[...])
    ld = -DECAY_SCALE * jax.nn.sigmoid(u)
    a = jax.nn.sigmoid(a0 + _dot(da.astype(BF16), aup_ref[...]))
    g = _dot(jax.nn.sigmoid(dg).astype(BF16), gup_ref[...])

    bd = _block_diag_ones(LANES, RW_HEAD_DIM)
    kk = k * k_k
    kk = kk / jnp.maximum(jnp.sqrt(_group_sum64(kk * kk, bd, 2)), 1e-12)
    k2 = k * (1.0 + (a - 1.0) * k_a)
    bonus = _group_sum64(r * k2 * r_k, bd, 2) * v

    t_i = lax.broadcasted_iota(jnp.int32, (tm, tm), 0)
    s_i = lax.broadcasted_iota(jnp.int32, (tm, tm), 1)
    same = (t_i // CHUNK) == (s_i // CHUNK)
    tri = jnp.where(same & (s_i <= t_i), 1.0, 0.0).astype(BF16)
    rest = jnp.where(same & (s_i > t_i), 1.0, 0.0).astype(BF16)
    c_i = lax.broadcasted_iota(jnp.int32, (tm // CHUNK, tm), 0)
    cs_i = lax.broadcasted_iota(jnp.int32, (tm // CHUNK, tm), 1)
    whole = jnp.where(cs_i // CHUNK == c_i, 1.0, 0.0).astype(BF16)

    ldh, ldm, ldl = _split3(ld)

    def sel(m):
        return _dot(m, ldh) + _dot(m, ldm) + _dot(m, ldl)

    cum = sel(tri)
    e_neg = jnp.exp(-cum)
    e_rem = jnp.exp(sel(rest))
    b = kk * a
    at_ref[...] = (-kk * jnp.exp(cum - ld)).astype(BF16)
    rt_ref[...] = (r * jnp.exp(cum)).astype(BF16)
    bt_ref[...] = (b * e_neg).astype(BF16)
    kt_ref[...] = (k2 * e_neg).astype(BF16)
    bg_ref[...] = (b * e_rem).astype(BF16)
    kg_ref[...] = (k2 * e_rem).astype(BF16)
    vb_ref[...] = v.astype(BF16)
    bonus_ref[...] = bonus.astype(BF16)
    g_ref[...] = g.astype(BF16)
    gam_ref[0] = jnp.exp(sel(whole))


def _rwkv_prep(rw, mu, params, wup, aup, gup, seq, tm, tn):
    t = rw.shape[0]
    nj = RW_WIDTH // tn
    lo_w = 4 * LANES
    lo_blk = (3 * RW_WIDTH) // lo_w
    nseq = seq // tm
    rpb = tm // SUBLANES

    def cur(off):
        return pl.BlockSpec((tm, tn), lambda i, j: (i, off * nj + j))

    def prv(off):
        return pl.BlockSpec((SUBLANES, tn), lambda i, j: (jnp.maximum(i * rpb - 1, 0), off * nj + j))

    def row(off):
        return pl.BlockSpec((1, tn), lambda i, j: (0, off * nj + j))

    in_specs = [cur(0), cur(1), cur(2), pl.BlockSpec((tm, lo_w), lambda i, j: (i, lo_blk)),
                prv(0), prv(1), prv(2),
                pl.BlockSpec((SUBLANES, lo_w), lambda i, j: (jnp.maximum(i * rpb - 1, 0), lo_blk)),
                row(0), row(1), row(2), pl.BlockSpec((1, lo_w), lambda i, j: (0, lo_blk)),
                pl.BlockSpec((SUBLANES, tn), lambda i, j: (0, j)),
                pl.BlockSpec((LANES, tn), lambda i, j: (0, j)),
                pl.BlockSpec((LANES, tn), lambda i, j: (0, j)),
                pl.BlockSpec((2 * LANES, tn), lambda i, j: (0, j))]
    out_blk = pl.BlockSpec((tm, tn), lambda i, j: (i, j))
    outs = [jax.ShapeDtypeStruct((t, RW_WIDTH), BF16)] * 9
    outs.append(jax.ShapeDtypeStruct((t // tm, tm // CHUNK, RW_WIDTH), F32))
    out_specs = [out_blk] * 9 + [pl.BlockSpec((1, tm // CHUNK, tn), lambda i, j: (i, 0, j))]
    return pl.pallas_call(
        functools.partial(_rwkv_prep_kernel, tm=tm, nseq=nseq),
        out_shape=outs,
        grid=(t // tm, nj),
        in_specs=in_specs,
        out_specs=out_specs,
        compiler_params=_cparams("parallel", "parallel"),
        name="rwkv_prep",
    )(rw, rw, rw, rw, rw, rw, rw, rw, mu, mu, mu, mu, params, wup, aup, gup)


PAIR = 2 * RW_HEAD_DIM


def _dot_bf16(a, b, dims=None):
    return _dot(a.astype(BF16), b.astype(BF16), dims)


def _split2(x):
    h = x.astype(BF16)
    return h, (x - h.astype(F32)).astype(BF16)


def _dot_split_lhs(a, b, dims=None):
    ah, al = _split2(a)
    bh = b.astype(BF16)
    return _dot(ah, bh, dims) + _dot(al, bh, dims)


def _rwkv_chunk_kernel(at_ref, rt_ref, bt_ref, kt_ref, bg_ref, kg_ref, v_ref, bonus_ref, g_ref,
                       gam_ref, lng_ref, lnb_ref, *rest, n_cast):
    cast_in, o_ref, cast_out, s_ref = rest[:n_cast], rest[n_cast], rest[n_cast + 1:2 * n_cast + 1], rest[-1]
    c = pl.program_id(1)

    for src, dst in zip(cast_in, cast_out):
        dst[...] = src[...].astype(dst.dtype)

    @pl.when(c == 0)
    def _():
        s_ref[...] = jnp.zeros(s_ref.shape, F32)

    lane = lax.broadcasted_iota(jnp.int32, (1, PAIR), 1)
    first = lane < RW_HEAD_DIM
    rho = lax.broadcasted_iota(jnp.int32, (PAIR, PAIR), 0)
    sig = lax.broadcasted_iota(jnp.int32, (PAIR, PAIR), 1)
    strict, incl, eye = sig < rho, sig <= rho, sig == rho
    own = (rho // RW_HEAD_DIM) == (sig // RW_HEAD_DIM)

    def stacked(x):
        z = jnp.zeros_like(x)
        return jnp.concatenate([jnp.where(first, x, z), jnp.where(first, z, x)], axis=0)

    n_pairs = RW_HEADS // 2
    pairs = range(n_pairs)
    sls = [slice(p * PAIR, (p + 1) * PAIR) for p in pairs]
    zero = jnp.zeros((PAIR, PAIR), F32)
    st = [[stacked(ref[:, sl]) for ref in (at_ref, rt_ref, bt_ref, kt_ref, bg_ref, kg_ref, v_ref)] for sl in sls]
    prods = [_dot(jnp.concatenate([q[0], q[1]], axis=0), jnp.concatenate([q[2], q[3]], axis=0), NT_DIMS) for q in st]
    lmat = [jnp.where(strict, pr_[:PAIR, :PAIR], zero) for pr_ in prods]
    sak = [jnp.where(strict, pr_[:PAIR, PAIR:], zero) for pr_ in prods]
    lrbk = [jnp.concatenate([jnp.where(incl, pr_[PAIR:, :PAIR], zero), jnp.where(incl, pr_[PAIR:, PAIR:], zero)],
                            axis=1).astype(BF16) for pr_ in prods]
    akv = [_dot(sak[p].astype(BF16), st[p][6]) for p in pairs]
    x = [jnp.concatenate([st[p][0].astype(F32), akv[p]], axis=1) for p in pairs]
    li = lmat
    n_steps = CHUNK.bit_length() - 1
    for it in range(n_steps):
        if it + 1 < n_steps:
            res = [_dot_bf16(li[p], jnp.concatenate([li[p], x[p]], axis=1)) for p in pairs]
            x = [x[p] + res[p][:, PAIR:] for p in pairs]
            li = [res[p][:, :PAIR] for p in pairs]
        else:
            res = [_dot_bf16(li[p], x[p]) for p in pairs]
            x = [x[p] + res[p] for p in pairs]
    gmat = [jnp.concatenate([x[p], jnp.concatenate([zero, st[p][6].astype(F32)], axis=1)], axis=0).astype(BF16)
            for p in pairs]
    out1 = [_dot(lrbk[p], gmat[p]) for p in pairs]
    out2 = [_dot(gmat[p], jnp.concatenate([st[p][4], st[p][5]], axis=0), TN_DIMS) for p in pairs]
    s0 = [s_ref[p] for p in pairs]
    qe = [out1[p][:, :PAIR] + st[p][1].astype(F32) for p in pairs]
    y = [_dot_bf16(qe[p], s0[p], NT_DIMS) + out1[p][:, PAIR:] for p in pairs]
    for p in pairs:
        mmat = out2[p][:PAIR] + jnp.where(eye, gam_ref[0][:, sls[p]], 0.0)
        s_ref[p] = _dot_split_lhs(s0[p], mmat) + out2[p][PAIR:]
    for p in pairs:
        sl = sls[p]
        mean = jnp.sum(y[p], axis=-1, keepdims=True) * (1.0 / RW_HEAD_DIM)
        d = jnp.where(own, y[p] - mean, 0.0)
        var = jnp.sum(d * d, axis=-1, keepdims=True) * (1.0 / RW_HEAD_DIM)
        yn = d * lax.rsqrt(var + GN_EPS)
        yn = yn[:CHUNK] + yn[CHUNK:]
        out = (yn * lng_ref[:, sl] + lnb_ref[:, sl] + bonus_ref[:, sl].astype(F32)) * g_ref[:, sl].astype(F32)
        o_ref[:, sl] = out.astype(o_ref.dtype)


def _rwkv_chunks(prep, lnx_g, lnx_b, batch, seq, casts=()):
    at, rt, bt, kt, bg, kg, vb, bonus, g, gam = prep
    t = at.shape[0]
    nc = seq // CHUNK
    steps = batch * nc
    gam = gam.reshape(t // CHUNK, 1, RW_WIDTH)
    blk = pl.BlockSpec((CHUNK, RW_WIDTH), lambda b, c: (b * nc + c, 0))
    rowspec = pl.BlockSpec((1, RW_WIDTH), lambda b, c: (0, 0))
    cast_specs = [pl.BlockSpec((w.shape[0] // steps, w.shape[1]), lambda b, c: (b * nc + c, 0)) for w in casts]
    outs = pl.pallas_call(
        functools.partial(_rwkv_chunk_kernel, n_cast=len(casts)),
        out_shape=[jax.ShapeDtypeStruct((t, RW_WIDTH), BF16)] + [jax.ShapeDtypeStruct(w.shape, BF16) for w in casts],
        grid=(batch, nc),
        in_specs=([blk] * 9 + [pl.BlockSpec((1, 1, RW_WIDTH), lambda b, c: (b * nc + c, 0, 0)), rowspec, rowspec]
                  + cast_specs),
        out_specs=[blk] + cast_specs,
        scratch_shapes=[pltpu.VMEM((RW_HEADS // 2, PAIR, PAIR), F32)],
        compiler_params=_cparams("arbitrary", "arbitrary"),
        name="rwkv_chunks",
    )(at, rt, bt, kt, bg, kg, vb, bonus, g, gam, lnx_g.reshape(1, RW_WIDTH), lnx_b.reshape(1, RW_WIDTH), *casts)
    return outs[0], outs[1:]


def _cast_kernel(x_ref, o_ref):
    o_ref[...] = x_ref[...].astype(o_ref.dtype)


def _cast_bf16(w, rows):
    n, m = w.shape
    return pl.pallas_call(
        _cast_kernel,
        out_shape=jax.ShapeDtypeStruct((n, m), BF16),
        grid=(n // rows,),
        in_specs=[pl.BlockSpec((rows, m), lambda i: (i, 0))],
        out_specs=pl.BlockSpec((rows, m), lambda i: (i, 0)),
        compiler_params=_cparams("parallel"),
        name="cast_bf16",
    )(w)


def _merge_kernel(h_ref, ya_ref, yb_ref, wga_ref, wgb_ref, pa_ref, pb_ref, o_ref):
    h = h_ref[...]
    ga = jax.nn.sigmoid(_dot(h, wga_ref[...]))
    gb = jax.nn.sigmoid(_dot(h, wgb_ref[...]))
    o_ref[...] = (ga * _dot(ya_ref[...], pa_ref[...]) + gb * _dot(yb_ref[...], pb_ref[...])).astype(o_ref.dtype)


def _merge(h, ya, yb, wga, wgb, pa, pb, tm, tn):
    t, d = h.shape
    n = wga.shape[1]
    ka, kb = ya.shape[1], yb.shape[1]
    return pl.pallas_call(
        _merge_kernel,
        out_shape=jax.ShapeDtypeStruct((t, n), BF16),
        grid=(t // tm, n // tn),
        in_specs=[pl.BlockSpec((tm, d), lambda i, j: (i, 0)),
                  pl.BlockSpec((tm, ka), lambda i, j: (i, 0)),
                  pl.BlockSpec((tm, kb), lambda i, j: (i, 0)),
                  pl.BlockSpec((d, tn), lambda i, j: (0, j)),
                  pl.BlockSpec((d, tn), lambda i, j: (0, j)),
                  pl.BlockSpec((ka, tn), lambda i, j: (0, j)),
                  pl.BlockSpec((kb, tn), lambda i, j: (0, j))],
        out_specs=pl.BlockSpec((tm, tn), lambda i, j: (i, j)),
        compiler_params=_cparams("parallel", "parallel"),
        name="gated_merge",
    )(h, ya, yb, wga, wgb, pa, pb)


def _first_lane_where(cond, lane):
    return jnp.min(jnp.where(cond, lane, LANES), axis=-1, keepdims=True)


def _outproj_kernel(m_ref, x_ref, wo_ref, g2_ref, wrh_ref, wrl_ref, rb_ref, x1_ref, h2_ref, route_ref):
    x1 = x_ref[...] + _dot(m_ref[...], wo_ref[...])
    x1_ref[...] = x1
    ms = jnp.mean(x1 * x1, axis=-1, keepdims=True)
    h2 = x1 * lax.rsqrt(ms + NORM_EPS) * g2_ref[...]
    h2_ref[...] = h2
    hh = h2.astype(BF16)
    hl = (h2 - hh.astype(F32)).astype(BF16)
    wrh = wrh_ref[...]
    lg = _dot(hh, wrh) + _dot(hl, wrh) + _dot(hh, wrl_ref[...]) + rb_ref[...]

    lane = lax.broadcasted_iota(jnp.int32, lg.shape, 1)
    neg = -jnp.inf
    is_g = lane < N_GROUPS
    mg = jnp.max(jnp.where(is_g, lg, neg), axis=-1, keepdims=True)
    eg = jnp.where(is_g, jnp.exp(lg - mg), 0.0)
    pg = eg / jnp.sum(eg, axis=-1, keepdims=True)
    p_g_top = jnp.max(pg, axis=-1, keepdims=True)
    g_idx = _first_lane_where(is_g & (pg == p_g_top), lane)
    lo = N_GROUPS + g_idx * EXPERTS_PER_GROUP
    sel = (lane >= lo) & (lane < lo + EXPERTS_PER_GROUP)
    me = jnp.max(jnp.where(sel, lg, neg), axis=-1, keepdims=True)
    ee = jnp.where(sel, jnp.exp(lg - me), 0.0)
    pe = ee / jnp.sum(ee, axis=-1, keepdims=True)
    pe = jnp.where(sel, pe, -1.0)
    v1 = jnp.max(pe, axis=-1, keepdims=True)
    i1 = _first_lane_where(pe == v1, lane)
    pe2 = jnp.where(lane == i1, -1.0, pe)
    v2 = jnp.max(pe2, axis=-1, keepdims=True)
    i2 = _first_lane_where(pe2 == v2, lane)
    den = v1 + v2
    route = jnp.where(lane == 0, p_g_top * v1 / den,
                      jnp.where(lane == 1, p_g_top * v2 / den,
                                jnp.where(lane == 2, (i1 - N_GROUPS).astype(F32),
                                          jnp.where(lane == 3, (i2 - N_GROUPS).astype(F32), 0.0))))
    route_ref[...] = route


def _outproj(merged, x, wo, g2, wr_hi, wr_lo, rbias, tm):
    t, d = x.shape
    nr = wr_hi.shape[1]
    return pl.pallas_call(
        _outproj_kernel,
        out_shape=[jax.ShapeDtypeStruct((t, d), F32), jax.ShapeDtypeStruct((t, d), F32),
                   jax.ShapeDtypeStruct((t, nr), F32)],
        grid=(t // tm,),
        in_specs=[pl.BlockSpec((tm, d), lambda i: (i, 0)),
                  pl.BlockSpec((tm, d), lambda i: (i, 0)),
                  pl.BlockSpec((d, d), lambda i: (0, 0)),
                  pl.BlockSpec((1, d), lambda i: (0, 0)),
                  pl.BlockSpec((d, nr), lambda i: (0, 0)),
                  pl.BlockSpec((d, nr), lambda i: (0, 0)),
                  pl.BlockSpec((1, nr), lambda i: (0, 0))],
        out_specs=[pl.BlockSpec((tm, d), lambda i: (i, 0)),
                   pl.BlockSpec((tm, d), lambda i: (i, 0)),
                   pl.BlockSpec((tm, nr), lambda i: (i, 0))],
        compiler_params=_cparams("parallel"),
        name="outproj_norm_router",
    )(merged, x, wo, g2.reshape(1, d), wr_hi, wr_lo, rbias)


IDX_SLOTS = 3


def _moe_kernel(be_ref, nused_ref, nvalid_ref, tok_hbm, dst_hbm, h_hbm, wg_ref, wu_ref, wd_ref, y_hbm,
                xbuf, obuf, tok_s, dst_s, idx_sem, g_sem, s_sem, *, rows):
    b = pl.program_id(0)
    n_used = nused_ref[0]
    last = n_used - 1

    def idx_copies(blk, sl):
        return (pltpu.make_async_copy(tok_hbm.at[blk], tok_s.at[sl], idx_sem.at[sl, 0]),
                pltpu.make_async_copy(dst_hbm.at[blk], dst_s.at[sl], idx_sem.at[sl, 1]))

    def gather_row(r, isl, xsl):
        return pltpu.make_async_copy(h_hbm.at[tok_s[isl, r]], xbuf.at[xsl, r], g_sem.at[xsl])

    def gather_all(xsl):
        return pltpu.make_async_copy(h_hbm.at[pl.ds(0, rows)], xbuf.at[xsl], g_sem.at[xsl])

    def scatter_row(r, isl, osl):
        return pltpu.make_async_copy(obuf.at[osl, r], y_hbm.at[dst_s[isl, r]], s_sem.at[osl])

    def wait_scatter(blk, osl):
        n = nvalid_ref[blk]

        @pl.when(n == rows)
        def _():
            pltpu.make_async_copy(obuf.at[osl], y_hbm.at[pl.ds(0, rows)], s_sem.at[osl]).wait()

        @pl.when(n < rows)
        def _():
            def one(r, c):
                pltpu.make_async_copy(obuf.at[osl, 0], y_hbm.at[0], s_sem.at[osl]).wait()
                return c

            lax.fori_loop(0, n, one, 0)

    @pl.when(b == 0)
    def _():
        for cp in idx_copies(0, 0):
            cp.start()
        for cp in idx_copies(0, 0):
            cp.wait()
        for cp in idx_copies(jnp.minimum(1, last), 1):
            cp.start()

        def issue(r, c):
            gather_row(r, 0, 0).start()
            return c

        lax.fori_loop(0, rows, issue, 0, unroll=8)

    @pl.when((b > 0) & (b < n_used))
    def _():
        wait_scatter(b - 1, (b - 1) % 2)

    @pl.when(b < n_used)
    def _():
        cur_i, nxt_i, ld_i = b % IDX_SLOTS, (b + 1) % IDX_SLOTS, (b + 2) % IDX_SLOTS
        cur_x, nxt_x = b % 2, (b + 1) % 2
        for cp in idx_copies(0, nxt_i):
            cp.wait()
        for r in range(rows):
            gather_row(r, nxt_i, nxt_x).start()
        for cp in idx_copies(jnp.minimum(b + 2, last), ld_i):
            cp.start()

        gather_all(cur_x).wait()
        x = xbuf[cur_x].astype(BF16)
        gate = _dot(x, wg_ref[...])
        up = _dot(x, wu_ref[...])
        mid = (gate * jax.nn.sigmoid(gate) * up).astype(BF16)
        obuf[cur_x] = _dot(mid, wd_ref[...])

    @pl.when((b < n_used) & (nvalid_ref[b] == rows))
    def _():
        for r in range(rows):
            scatter_row(r, b % IDX_SLOTS, b % 2).start()

    @pl.when((b < n_used) & (nvalid_ref[b] < rows))
    def _():
        def issue(r, c):
            scatter_row(r, b % IDX_SLOTS, b % 2).start()
            return c

        lax.fori_loop(0, nvalid_ref[b], issue, 0)

    @pl.when(b == last)
    def _():
        wait_scatter(b, b % 2)
        gather_all((b + 1) % 2).wait()
        for cp in idx_copies(0, (b + 2) % IDX_SLOTS):
            cp.wait()


def _moe(h2, row_tok, row_dst, block_e, n_used, nvalid, wg, wu, wd, rows):
    t, d = h2.shape
    n_blocks = row_tok.shape[0]
    a = t * TOP_K
    f = wg.shape[2]
    kern = functools.partial(_moe_kernel, rows=rows)
    grid_spec = pltpu.PrefetchScalarGridSpec(
        num_scalar_prefetch=3,
        grid=(n_blocks,),
        in_specs=[pl.BlockSpec(memory_space=pl.ANY),
                  pl.BlockSpec(memory_space=pl.ANY),
                  pl.BlockSpec(memory_space=pl.ANY),
                  pl.BlockSpec((None, d, f), lambda b, be, nu, nv: (be[b], 0, 0)),
                  pl.BlockSpec((None, d, f), lambda b, be, nu, nv: (be[b], 0, 0)),
                  pl.BlockSpec((None, f, d), lambda b, be, nu, nv: (be[b], 0, 0))],
        out_specs=pl.BlockSpec(memory_space=pl.ANY),
        scratch_shapes=[pltpu.VMEM((2, rows, d), F32),
                        pltpu.VMEM((2, rows, d), F32),
                        pltpu.SMEM((IDX_SLOTS, rows), jnp.int32),
                        pltpu.SMEM((IDX_SLOTS, rows), jnp.int32),
                        pltpu.SemaphoreType.DMA((IDX_SLOTS, 2)),
                        pltpu.SemaphoreType.DMA((2,)),
                        pltpu.SemaphoreType.DMA((2,))],
    )
    return pl.pallas_call(
        kern,
        out_shape=jax.ShapeDtypeStruct((a, d), F32),
        grid_spec=grid_spec,
        compiler_params=_cparams("arbitrary"),
        name="moe_experts",
    )(block_e, n_used, nvalid, row_tok, row_dst, h2, wg, wu, wd)


def _combine_kernel(x1_ref, y0_ref, y1_ref, route_ref, o_ref):
    gts = route_ref[...]
    o_ref[...] = x1_ref[...] + gts[:, 0:1] * y0_ref[...] + gts[:, 1:2] * y1_ref[...]


def _combine(x1, y, route, tm):
    t, d = x1.shape
    nb = t // tm
    return pl.pallas_call(
        _combine_kernel,
        out_shape=jax.ShapeDtypeStruct((t, d), F32),
        grid=(nb,),
        in_specs=[pl.BlockSpec((tm, d), lambda i: (i, 0)),
                  pl.BlockSpec((tm, d), lambda i: (i, 0)),
                  pl.BlockSpec((tm, d), lambda i: (nb + i, 0)),
                  pl.BlockSpec((tm, LANES), lambda i: (i, 0))],
        out_specs=pl.BlockSpec((tm, d), lambda i: (i, 0)),
        compiler_params=_cparams("parallel"),
        name="moe_combine",
    )(x1, y, y, route)


def _routing_tables(route, rows):
    t = route.shape[0]
    a = t * TOP_K
    expert = route[:, 2:2 + TOP_K].astype(jnp.int32)
    flat_e = expert.reshape(a)
    ids = jnp.arange(a, dtype=jnp.int32)
    order = jnp.sort(flat_e * a + ids) % a
    counts = jnp.sum((flat_e[:, None] == jnp.arange(N_EXPERTS, dtype=jnp.int32)[None, :]).astype(jnp.int32), axis=0)
    padded = (counts + rows - 1) // rows * rows
    start = jnp.cumsum(counts) - counts
    pend = jnp.cumsum(padded)
    pstart = pend - padded
    n_rows = a + N_EXPERTS * rows
    n_blocks = n_rows // rows
    blk_row0 = jnp.arange(n_blocks, dtype=jnp.int32) * rows
    block_e = jnp.minimum(jnp.sum((pend[None, :] <= blk_row0[:, None]).astype(jnp.int32), axis=1), N_EXPERTS - 1)
    rho = jnp.arange(n_rows, dtype=jnp.int32)
    be = jnp.repeat(block_e, rows)
    off = rho - pstart[be]
    valid = (off >= 0) & (off < counts[be])
    src = order[jnp.clip(start[be] + off, 0, a - 1)]
    row_tok = jnp.where(valid, src // TOP_K, 0)
    row_dst = jnp.where(valid, (src % TOP_K) * t + src // TOP_K, 0)
    n_used = (pend[-1] // rows).astype(jnp.int32).reshape(1)
    nvalid = jnp.clip(counts[block_e] - (blk_row0 - pstart[block_e]), 0, rows).astype(jnp.int32)
    return (row_tok.reshape(n_blocks, rows), row_dst.reshape(n_blocks, rows), block_e.astype(jnp.int32), n_used, nvalid)


def _tiles(t, seq):
    return dict(
        norm_tm=min(512, t),
        proj_tm=min(1024, seq), proj_tn=512,
        attn_tq=min(256, seq), attn_tk=min(1024, seq), attn_heads=2,
        prep_tm=min(256, seq), prep_tn=256,
        merge_tm=min(1024, t), merge_tn=512,
        out_tm=min(256, t),
        moe_rows=256,
        comb_tm=min(512, t),
    )


def _pad_rows(w, n):
    return jnp.pad(w, ((0, n - w.shape[0]), (0, 0)))


def _pad_cols(w, n):
    return jnp.pad(w, ((0, 0), (0, n - w.shape[1])))


def kernel(x, norm1_g, w_in, q_norm_g, k_norm_g, lam_q1, lam_k1, lam_q2, lam_k2, subln_g, shift_mu, w0, w_up, a0, a_up, g_up, k_k, k_a, r_k, lnx_g, lnx_b, proj_a, proj_b, w_out, norm2_g, router_g, router_g_b, router_e, router_e_b, w_gate_e, w_up_e, w_down_e):
    batch, seq, d = x.shape
    t = batch * seq
    depth = norm1_g.shape[0]
    tl = _tiles(t, seq)
    qkw = DA_HEADS * 2 * DA_HEAD_DIM
    vw = DA_HEADS * DA_V_DIM
    c_q, c_k, c_v = 0, qkw, 2 * qkw
    c_rw = c_v + vw
    c_dw = c_rw + 3 * RW_WIDTH
    c_da = c_dw + DECAY_LORA
    c_dg = c_da + AAA_LORA
    c_ga = c_dg + GATE_LORA
    c_gb = c_ga + d
    cos_t, sinm_t, sinp_t = _rope_tables(seq)
    xf = x.reshape(t, d)

    for l in range(depth):
        lam_init = 0.8 - 0.6 * math.exp(-0.3 * l)
        wl = w_in[l]
        w_qk = wl[:, c_q:c_v].astype(BF16)
        w_v = wl[:, c_v:c_rw].astype(BF16)
        w_rw = jnp.concatenate([wl[:, c_rw:c_dw], _pad_cols(wl[:, c_dw:c_da], LANES),
                                _pad_cols(wl[:, c_da:c_dg], LANES), wl[:, c_dg:c_ga]], axis=1).astype(BF16)
        w_ga = wl[:, c_ga:c_gb].astype(BF16)
        w_gb = wl[:, c_gb:].astype(BF16)
        mu = shift_mu[l]
        o_dw = 3 * RW_WIDTH
        mu_l = jnp.concatenate([mu[:o_dw], jnp.pad(mu[o_dw:o_dw + DECAY_LORA], (0, LANES - DECAY_LORA)),
                                jnp.pad(mu[o_dw + DECAY_LORA:o_dw + DECAY_LORA + AAA_LORA], (0, LANES - AAA_LORA)),
                                mu[o_dw + DECAY_LORA + AAA_LORA:]]).reshape(1, -1)
        gain_row = jnp.concatenate([jnp.tile(q_norm_g[l], 2 * DA_HEADS) * (DA_HEAD_DIM ** -0.5),
                                    jnp.tile(k_norm_g[l], 2 * DA_HEADS)]).reshape(1, 2 * qkw)
        lam_params = jnp.stack([lam_q1[l], lam_k1[l], lam_q2[l], lam_k2[l]])
        zrow = jnp.zeros((RW_WIDTH,), F32)
        rw_params = jnp.stack([w0[l], a0[l], k_k[l], k_a[l], r_k[l].reshape(-1), zrow, zrow, zrow])

        h = _rmsnorm(xf, norm1_g[l], NORM_EPS, tl["norm_tm"])
        qk = _qk_proj(h, w_qk, gain_row, cos_t, sinm_t, sinp_t, seq, tl["proj_tm"], tl["proj_tn"])
        v = _matmul(h, w_v, BF16, tl["proj_tm"], tl["proj_tn"], "v_proj")
        ya = _diff_attention(qk, v, lam_params, subln_g[l], batch, seq, lam_init, tl["attn_tq"], tl["attn_tk"],
                             tl["attn_heads"])

        rw = _matmul(h, w_rw, F32, tl["proj_tm"], tl["proj_tn"], "rw_proj")
        prep = _rwkv_prep(rw, mu_l, rw_params, _pad_rows(w_up[l], LANES).astype(BF16),
                          _pad_rows(a_up[l], LANES).astype(BF16), g_up[l].astype(BF16),
                          seq, tl["prep_tm"], tl["prep_tn"])
        ew = (w_gate_e[l], w_up_e[l], w_down_e[l])
        ew2d = tuple(w.reshape(-1, w.shape[-1]) for w in ew)
        steps = batch * (seq // CHUNK)
        ride = all(w.shape[0] % (steps * 2 * SUBLANES) == 0 and w.size * 4 // steps <= CAST_BLOCK_BYTES for w in ew2d)
        yb, ew_bf16 = _rwkv_chunks(prep, lnx_g[l], lnx_b[l], batch, seq, ew2d if ride else ())
        if not ride:
            ew_bf16 = tuple(_cast_bf16(w, CAST_BLOCK_BYTES // (4 * w.shape[1])) for w in ew2d)
        wg_e, wu_e, wd_e = (c.reshape(w.shape) for c, w in zip(ew_bf16, ew))

        merged = _merge(h, ya, yb, w_ga, w_gb, proj_a[l].astype(BF16), proj_b[l].astype(BF16),
                        tl["merge_tm"], tl["merge_tn"])
        wr = _pad_cols(jnp.concatenate([router_g[l], router_e[l]], axis=1), LANES)
        wr_hi = wr.astype(BF16)
        wr_lo = (wr - wr_hi.astype(F32)).astype(BF16)
        rbias = jnp.pad(jnp.concatenate([router_g_b[l], router_e_b[l]]), (0, LANES - N_GROUPS - N_EXPERTS))
        x1, h2, route = _outproj(merged, xf, w_out[l].astype(BF16), norm2_g[l], wr_hi, wr_lo,
                                 rbias.reshape(1, LANES), tl["out_tm"])

        rows = tl["moe_rows"]
        row_tok, row_dst, block_e, n_used, nvalid = _routing_tables(route, rows)
        y = _moe(h2, row_tok, row_dst, block_e, n_used, nvalid, wg_e, wu_e, wd_e, rows)
        xf = _combine(x1, y, route, tl["comb_tm"])
    return xf.reshape(batch, seq, d)
```

```python
import functools
import math

import jax
import jax.numpy as jnp
from jax import lax
from jax.experimental import pallas as pl
from jax.experimental.pallas import tpu as pltpu

DA_HEADS = 8
DA_HEAD_DIM = 64
DA_V_DIM = 2 * DA_HEAD_DIM
ROT_DIM = DA_HEAD_DIM // 4
ROPE_THETA = 500000.0
SUBLN_EPS = 1e-5
RW_HEADS = 16
RW_HEAD_DIM = 64
RW_WIDTH = RW_HEADS * RW_HEAD_DIM
DECAY_LORA = 96
AAA_LORA = 96
GATE_LORA = 256
GN_EPS = 64e-5
N_GROUPS = 4
EXPERTS_PER_GROUP = 8
N_EXPERTS = N_GROUPS * EXPERTS_PER_GROUP
TOP_K = 2
NORM_EPS = 1e-6

LANES = 128
SUBLANES = 8
VMEM_LIMIT_BYTES = 56 * 1024 * 1024

CHUNK = 64
CAST_BLOCK_BYTES = 2 * 1024 * 1024

F32 = jnp.float32
BF16 = jnp.bfloat16

LOG2_E = math.log2(math.e)

NT_DIMS = (((1,), (1,)), ((), ()))
TN_DIMS = (((0,), (0,)), ((), ()))


def _cparams(*sem):
    return pltpu.CompilerParams(dimension_semantics=tuple(sem), vmem_limit_bytes=VMEM_LIMIT_BYTES)


def _dot(a, b, dims=None, precision=None):
    if dims is None:
        return jnp.dot(a, b, preferred_element_type=F32, precision=precision)
    return lax.dot_general(a, b, dims, preferred_element_type=F32, precision=precision)


def _split3(x):
    h = x.astype(BF16)
    r = x - h.astype(F32)
    m = r.astype(BF16)
    l = (r - m.astype(F32)).astype(BF16)
    return h, m, l


def _dot_exact_rhs(sel_bf16, x_f32):
    h, m, l = _split3(x_f32)
    return _dot(sel_bf16, h) + _dot(sel_bf16, m) + _dot(sel_bf16, l)


def _group_sum64(x, bd, terms):
    parts = _split3(x)[:terms]
    outs = []
    for s in range(x.shape[1] // LANES):
        sl = slice(s * LANES, (s + 1) * LANES)
        acc = _dot(parts[0][:, sl], bd)
        for part in parts[1:]:
            acc = acc + _dot(part[:, sl], bd)
        outs.append(acc)
    return outs[0] if len(outs) == 1 else jnp.concatenate(outs, axis=1)


def _block_diag_ones(n, blk, dtype=BF16):
    r = lax.broadcasted_iota(jnp.int32, (n, n), 0) // blk
    c = lax.broadcasted_iota(jnp.int32, (n, n), 1) // blk
    return jnp.where(r == c, 1.0, 0.0).astype(dtype)


def _rmsnorm_kernel(x_ref, g_ref, o_ref, *, eps):
    x = x_ref[...]
    ms = jnp.mean(x * x, axis=-1, keepdims=True)
    o_ref[...] = (x * lax.rsqrt(ms + eps) * g_ref[...]).astype(o_ref.dtype)


def _rmsnorm(x, g, eps, tm):
    t, d = x.shape
    return pl.pallas_call(
        functools.partial(_rmsnorm_kernel, eps=eps),
        out_shape=jax.ShapeDtypeStruct((t, d), BF16),
        grid=(t // tm,),
        in_specs=[pl.BlockSpec((tm, d), lambda i: (i, 0)),
                  pl.BlockSpec((1, d), lambda i: (0, 0))],
        out_specs=pl.BlockSpec((tm, d), lambda i: (i, 0)),
        compiler_params=_cparams("parallel"),
        name="rmsnorm",
    )(x, g.reshape(1, d))


def _matmul_kernel(a_ref, w_ref, o_ref):
    o_ref[...] = _dot(a_ref[...], w_ref[...]).astype(o_ref.dtype)


def _matmul(a, w, out_dtype, tm, tn, name):
    t, k = a.shape
    n = w.shape[1]
    return pl.pallas_call(
        _matmul_kernel,
        out_shape=jax.ShapeDtypeStruct((t, n), out_dtype),
        grid=(t // tm, n // tn),
        in_specs=[pl.BlockSpec((tm, k), lambda i, j: (i, 0)),
                  pl.BlockSpec((k, tn), lambda i, j: (0, j))],
        out_specs=pl.BlockSpec((tm, tn), lambda i, j: (i, j)),
        compiler_params=_cparams("parallel", "parallel"),
        name=name,
    )(a, w)


def _qk_proj_kernel(a_ref, w_ref, gain_ref, cos_ref, sinm_ref, sinp_ref, o_ref, *, tn):
    acc = _dot(a_ref[...], w_ref[...])
    bd = _block_diag_ones(LANES, DA_HEAD_DIM)
    ms = _group_sum64(acc * acc, bd, 1) * (1.0 / DA_HEAD_DIM)
    xn = acc * lax.rsqrt(ms + NORM_EPS) * gain_ref[...]
    reps = tn // LANES
    cos = jnp.tile(cos_ref[...], (1, reps))
    sinm = jnp.tile(sinm_ref[...], (1, reps))
    sinp = jnp.tile(sinp_ref[...], (1, reps))
    half = ROT_DIM // 2
    hi = pltpu.roll(xn, tn - half, 1)
    lo = pltpu.roll(xn, half, 1)
    o_ref[...] = (xn * cos + hi * sinm + lo * sinp).astype(o_ref.dtype)


def _qk_proj(h, w_qk, gain_row, cos_t, sinm_t, sinp_t, seq, tm, tn):
    t, k = h.shape
    n = w_qk.shape[1]
    nseq = seq // tm
    return pl.pallas_call(
        functools.partial(_qk_proj_kernel, tn=tn),
        out_shape=jax.ShapeDtypeStruct((t, n), BF16),
        grid=(t // tm, n // tn),
        in_specs=[pl.BlockSpec((tm, k), lambda i, j: (i, 0)),
                  pl.BlockSpec((k, tn), lambda i, j: (0, j)),
                  pl.BlockSpec((1, tn), lambda i, j: (0, j)),
                  pl.BlockSpec((tm, LANES), lambda i, j: (i % nseq, 0)),
                  pl.BlockSpec((tm, LANES), lambda i, j: (i % nseq, 0)),
                  pl.BlockSpec((tm, LANES), lambda i, j: (i % nseq, 0))],
        out_specs=pl.BlockSpec((tm, tn), lambda i, j: (i, j)),
        compiler_params=_cparams("parallel", "parallel"),
        name="qk_proj",
    )(h, w_qk, gain_row, cos_t, sinm_t, sinp_t)


def _rope_tables(seq):
    half = ROT_DIM // 2
    inv_freq = ROPE_THETA ** (-jnp.arange(0, ROT_DIM, 2, dtype=F32) / ROT_DIM)
    ang = jnp.arange(seq, dtype=F32)[:, None] * inv_freq[None, :]
    cos, sin = jnp.cos(ang), jnp.sin(ang)
    ones = jnp.ones((seq, DA_HEAD_DIM - ROT_DIM), F32)
    zeros = jnp.zeros((seq, DA_HEAD_DIM - ROT_DIM), F32)
    zh = jnp.zeros((seq, half), F32)
    cos64 = jnp.concatenate([cos, cos, ones], axis=1)
    sinm64 = jnp.concatenate([-sin, zh, zeros], axis=1)
    sinp64 = jnp.concatenate([zh, sin, zeros], axis=1)
    return tuple(jnp.concatenate([a, a], axis=1) for a in (cos64, sinm64, sinp64))


def _diff_attn_kernel(lam_ref, q_ref, k_ref, v_ref, g_ref, o_ref, m_ref, l_ref, a_ref, *, tq, tk, lam_init, nh):
    i = pl.program_id(2)
    lane = lax.broadcasted_iota(jnp.int32, (1, DA_V_DIM), 1)
    qs = []
    for h in range(nh):
        q = q_ref[:, h * DA_V_DIM:(h + 1) * DA_V_DIM]
        zero = jnp.zeros_like(q)
        qs.append(jnp.where(lane < DA_HEAD_DIM, q, zero))
        qs.append(jnp.where(lane >= DA_HEAD_DIM, q, zero))
    nc = 2 * nh
    m_ref[...] = jnp.full(m_ref.shape, -jnp.inf, F32)
    l_ref[...] = jnp.zeros(l_ref.shape, F32)
    a_ref[...] = jnp.zeros(a_ref.shape, F32)

    def step(j, mask):
        off = pl.multiple_of(j * tk, tk)
        ks = [k_ref[pl.ds(off, tk), h * DA_V_DIM:(h + 1) * DA_V_DIM] for h in range(nh)]
        vs = [v_ref[pl.ds(off, tk), h * DA_V_DIM:(h + 1) * DA_V_DIM] for h in range(nh)]
        s = [_dot(qs[c], ks[c // 2], NT_DIMS) for c in range(nc)]
        for c in range(nc):
            sc = s[c] if mask is None else jnp.where(mask, s[c], -jnp.inf)
            m_old = m_ref[c]
            m_new = jnp.maximum(m_old, jnp.max(sc, axis=-1, keepdims=True))
            alpha = jnp.exp2(m_old - m_new)
            p = jnp.exp2(sc - m_new)
            l_ref[c] = alpha * l_ref[c] + jnp.sum(p, axis=-1, keepdims=True)
            a_ref[c] = alpha * a_ref[c] + _dot(p.astype(BF16), vs[c // 2])
            m_ref[c] = m_new

    n_full = (i * tq) // tk

    def full_body(j, c):
        step(j, None)
        return c

    lax.fori_loop(0, n_full, full_body, 0)

    row = i * tq + lax.broadcasted_iota(jnp.int32, (tq, tk), 0)
    for d in range(max(1, tq // tk)):
        j = n_full + d
        col = j * tk + lax.broadcasted_iota(jnp.int32, (tq, tk), 1)
        step(j, col <= row)

    lq1, lk1, lq2, lk2 = (lam_ref[r:r + 1, :] for r in range(4))
    lam = (jnp.exp(jnp.sum(lq1 * lk1, axis=-1, keepdims=True))
           - jnp.exp(jnp.sum(lq2 * lk2, axis=-1, keepdims=True)) + lam_init)
    for h in range(nh):
        o = a_ref[2 * h] / l_ref[2 * h] - lam * (a_ref[2 * h + 1] / l_ref[2 * h + 1])
        ms = jnp.mean(o * o, axis=-1, keepdims=True)
        o = o * lax.rsqrt(ms + SUBLN_EPS) * (g_ref[...] * (1.0 - lam_init))
        o_ref[:, h * DA_V_DIM:(h + 1) * DA_V_DIM] = o.astype(o_ref.dtype)


def _diff_attention(qk, v, lam_params, subln_g, batch, seq, lam_init, tq, tk, nh):
    t = qk.shape[0]
    nq = seq // tq
    w = nh * DA_V_DIM
    hg = DA_HEADS // nh
    kern = functools.partial(_diff_attn_kernel, tq=tq, tk=tk, lam_init=lam_init, nh=nh)
    return pl.pallas_call(
        kern,
        out_shape=jax.ShapeDtypeStruct((t, DA_HEADS * DA_V_DIM), BF16),
        grid=(batch, hg, nq),
        in_specs=[pl.BlockSpec((4, DA_HEAD_DIM), lambda b, h, i: (0, 0)),
                  pl.BlockSpec((tq, w), lambda b, h, i: (b * nq + i, h)),
                  pl.BlockSpec((seq, w), lambda b, h, i: (b, hg + h)),
                  pl.BlockSpec((seq, w), lambda b, h, i: (b, h)),
                  pl.BlockSpec((1, DA_V_DIM), lambda b, h, i: (0, 0))],
        out_specs=pl.BlockSpec((tq, w), lambda b, h, i: (b * nq + i, h)),
        scratch_shapes=[pltpu.VMEM((2 * nh, tq, 1), F32), pltpu.VMEM((2 * nh, tq, 1), F32),
                        pltpu.VMEM((2 * nh, tq, DA_V_DIM), F32)],
        compiler_params=_cparams("parallel", "parallel", "parallel"),
        name="diff_attention",
    )(lam_params, qk, qk, v, subln_g.reshape(1, DA_V_DIM))


DECAY_SCALE = math.exp(-0.5)


def _rwkv_prep_kernel(r_ref, k_ref, v_ref, lo_ref, pr_ref, pk_ref, pv_ref, plo_ref,
                      mur_ref, muk_ref, muv_ref, mulo_ref, par_ref, wup_ref, aup_ref, gup_ref,
                      at_ref, rt_ref, bt_ref, kt_ref, bg_ref, kg_ref, vb_ref, bonus_ref, g_ref, gam_ref,
                      *, tm, nseq):
    i = pl.program_id(0)
    seq_start = (i % nseq) == 0
    row0 = lax.broadcasted_iota(jnp.int32, (tm, 1), 0) == 0

    def shifted(x_ref, p_ref, mu_ref):
        x = x_ref[...]
        last = jnp.where(seq_start, 0.0, p_ref[SUBLANES - 1:SUBLANES, :])
        prev = jnp.where(row0, last, pltpu.roll(x, 1, 0))
        return x + (prev - x) * mu_ref[...]

    r = shifted(r_ref, pr_ref, mur_ref)
    k = shifted(k_ref, pk_ref, muk_ref)
    v = shifted(v_ref, pv_ref, muv_ref)
    lo = shifted(lo_ref, plo_ref, mulo_ref)
    dw, da, dg = lo[:, :LANES], lo[:, LANES:2 * LANES], lo[:, 2 * LANES:]
    w0, a0, k_k, k_a, r_k = (par_ref[j:j + 1, :] for j in range(5))

    u = w0 + _dot(jnp.tanh(dw).astype(BF16), wup_ref[...])
    ld = -DECAY_SCALE * jax.nn.sigmoid(u)
    a = jax.nn.sigmoid(a0 + _dot(da.astype(BF16), aup_ref[...]))
    g = _dot(jax.nn.sigmoid(dg).astype(BF16), gup_ref[...])

    bd = _block_diag_ones(LANES, RW_HEAD_DIM)
    kk = k * k_k
    kk = kk / jnp.maximum(jnp.sqrt(_group_sum64(kk * kk, bd, 2)), 1e-12)
    k2 = k * (1.0 + (a - 1.0) * k_a)
    bonus = _group_sum64(r * k2 * r_k, bd, 2) * v

    t_i = lax.broadcasted_iota(jnp.int32, (tm, tm), 0)
    s_i = lax.broadcasted_iota(jnp.int32, (tm, tm), 1)
    same = (t_i // CHUNK) == (s_i // CHUNK)
    tri = jnp.where(same & (s_i <= t_i), 1.0, 0.0).astype(BF16)
    rest = jnp.where(same & (s_i > t_i), 1.0, 0.0).astype(BF16)
    c_i = lax.broadcasted_iota(jnp.int32, (tm // CHUNK, tm), 0)
    cs_i = lax.broadcasted_iota(jnp.int32, (tm // CHUNK, tm), 1)
    whole = jnp.where(cs_i // CHUNK == c_i, 1.0, 0.0).astype(BF16)

    ldh, ldm, ldl = _split3(ld)

    def sel(m):
        return _dot(m, ldh) + _dot(m, ldm) + _dot(m, ldl)

    cum = sel(tri)
    e_neg = jnp.exp(-cum)
    e_rem = jnp.exp(sel(rest))
    b = kk * a
    at_ref[...] = (-kk * jnp.exp(cum - ld)).astype(BF16)
    rt_ref[...] = (r * jnp.exp(cum)).astype(BF16)
    bt_ref[...] = (b * e_neg).astype(BF16)
    kt_ref[...] = (k2 * e_neg).astype(BF16)
    bg_ref[...] = (b * e_rem).astype(BF16)
    kg_ref[...] = (k2 * e_rem).astype(BF16)
    vb_ref[...] = v.astype(BF16)
    bonus_ref[...] = bonus.astype(BF16)
    g_ref[...] = g.astype(BF16)
    gam_ref[0] = jnp.exp(sel(whole))


def _rwkv_prep(rw, mu, params, wup, aup, gup, seq, tm, tn):
    t = rw.shape[0]
    nj = RW_WIDTH // tn
    lo_w = 4 * LANES
    lo_blk = (3 * RW_WIDTH) // lo_w
    nseq = seq // tm
    rpb = tm // SUBLANES

    def cur(off):
        return pl.BlockSpec((tm, tn), lambda i, j: (i, off * nj + j))

    def prv(off):
        return pl.BlockSpec((SUBLANES, tn), lambda i, j: (jnp.maximum(i * rpb - 1, 0), off * nj + j))

    def row(off):
        return pl.BlockSpec((1, tn), lambda i, j: (0, off * nj + j))

    in_specs = [cur(0), cur(1), cur(2), pl.BlockSpec((tm, lo_w), lambda i, j: (i, lo_blk)),
                prv(0), prv(1), prv(2),
                pl.BlockSpec((SUBLANES, lo_w), lambda i, j: (jnp.maximum(i * rpb - 1, 0), lo_blk)),
                row(0), row(1), row(2), pl.BlockSpec((1, lo_w), lambda i, j: (0, lo_blk)),
                pl.BlockSpec((SUBLANES, tn), lambda i, j: (0, j)),
                pl.BlockSpec((LANES, tn), lambda i, j: (0, j)),
                pl.BlockSpec((LANES, tn), lambda i, j: (0, j)),
                pl.BlockSpec((2 * LANES, tn), lambda i, j: (0, j))]
    out_blk = pl.BlockSpec((tm, tn), lambda i, j: (i, j))
    outs = [jax.ShapeDtypeStruct((t, RW_WIDTH), BF16)] * 9
    outs.append(jax.ShapeDtypeStruct((t // tm, tm // CHUNK, RW_WIDTH), F32))
    out_specs = [out_blk] * 9 + [pl.BlockSpec((1, tm // CHUNK, tn), lambda i, j: (i, 0, j))]
    return pl.pallas_call(
        functools.partial(_rwkv_prep_kernel, tm=tm, nseq=nseq),
        out_shape=outs,
        grid=(t // tm, nj),
        in_specs=in_specs,
        out_specs=out_specs,
        compiler_params=_cparams("parallel", "parallel"),
        name="rwkv_prep",
    )(rw, rw, rw, rw, rw, rw, rw, rw, mu, mu, mu, mu, params, wup, aup, gup)


PAIR = 2 * RW_HEAD_DIM


def _dot_bf16(a, b, dims=None):
    return _dot(a.astype(BF16), b.astype(BF16), dims)


def _split2(x):
    h = x.astype(BF16)
    return h, (x - h.astype(F32)).astype(BF16)


def _dot_split_lhs(a, b, dims=None):
    ah, al = _split2(a)
    bh = b.astype(BF16)
    return _dot(ah, bh, dims) + _dot(al, bh, dims)


def _rwkv_chunk_kernel(at_ref, rt_ref, bt_ref, kt_ref, bg_ref, kg_ref, v_ref, bonus_ref, g_ref,
                       gam_ref, lng_ref, lnb_ref, *rest, n_cast):
    cast_in, o_ref, cast_out, s_ref = rest[:n_cast], rest[n_cast], rest[n_cast + 1:2 * n_cast + 1], rest[-1]
    c = pl.program_id(1)

    for src, dst in zip(cast_in, cast_out):
        dst[...] = src[...].astype(dst.dtype)

    @pl.when(c == 0)
    def _():
        s_ref[...] = jnp.zeros(s_ref.shape, F32)

    lane = lax.broadcasted_iota(jnp.int32, (1, PAIR), 1)
    first = lane < RW_HEAD_DIM
    rho = lax.broadcasted_iota(jnp.int32, (PAIR, PAIR), 0)
    sig = lax.broadcasted_iota(jnp.int32, (PAIR, PAIR), 1)
    strict, incl, eye = sig < rho, sig <= rho, sig == rho
    own = (rho // RW_HEAD_DIM) == (sig // RW_HEAD_DIM)

    def stacked(x):
        z = jnp.zeros_like(x)
        return jnp.concatenate([jnp.where(first, x, z), jnp.where(first, z, x)], axis=0)

    n_pairs = RW_HEADS // 2
    pairs = range(n_pairs)
    sls = [slice(p * PAIR, (p + 1) * PAIR) for p in pairs]
    zero = jnp.zeros((PAIR, PAIR), F32)
    st = [[stacked(ref[:, sl]) for ref in (at_ref, rt_ref, bt_ref, kt_ref, bg_ref, kg_ref, v_ref)] for sl in sls]
    prods = [_dot(jnp.concatenate([q[0], q[1]], axis=0), jnp.concatenate([q[2], q[3]], axis=0), NT_DIMS) for q in st]
    lmat = [jnp.where(strict, pr_[:PAIR, :PAIR], zero) for pr_ in prods]
    sak = [jnp.where(strict, pr_[:PAIR, PAIR:], zero) for pr_ in prods]
    lrbk = [jnp.concatenate([jnp.where(incl, pr_[PAIR:, :PAIR], zero), jnp.where(incl, pr_[PAIR:, PAIR:], zero)],
                            axis=1).astype(BF16) for pr_ in prods]
    akv = [_dot(sak[p].astype(BF16), st[p][6]) for p in pairs]
    ident = jnp.where(eye, 1.0, 0.0)
    li = [_dot_bf16(lmat[p], lmat[p]) for p in pairs]
    tmat = [ident + lmat[p] for p in pairs]
    n_steps = CHUNK.bit_length() - 1
    for it in range(1, n_steps):
        if it + 1 < n_steps:
            res = [_dot_bf16(li[p], jnp.concatenate([li[p], tmat[p]], axis=1)) for p in pairs]
            tmat = [tmat[p] + res[p][:, PAIR:] for p in pairs]
            li = [res[p][:, :PAIR] for p in pairs]
        else:
            tmat = [tmat[p] + _dot_bf16(li[p], tmat[p]) for p in pairs]
    x = [_dot_bf16(tmat[p], jnp.concatenate([st[p][0].astype(F32), akv[p]], axis=1)) for p in pairs]
    gmat = [jnp.concatenate([x[p], jnp.concatenate([zero, st[p][6].astype(F32)], axis=1)], axis=0).astype(BF16)
            for p in pairs]
    out1 = [_dot(lrbk[p], gmat[p]) for p in pairs]
    out2 = [_dot(gmat[p], jnp.concatenate([st[p][4], st[p][5]], axis=0), TN_DIMS) for p in pairs]
    s0 = [s_ref[p] for p in pairs]
    qe = [out1[p][:, :PAIR] + st[p][1].astype(F32) for p in pairs]
    y = [_dot_bf16(qe[p], s0[p], NT_DIMS) + out1[p][:, PAIR:] for p in pairs]
    for p in pairs:
        mmat = out2[p][:PAIR] + jnp.where(eye, gam_ref[0][:, sls[p]], 0.0)
        s_ref[p] = _dot_split_lhs(s0[p], mmat) + out2[p][PAIR:]
    for p in pairs:
        sl = sls[p]
        mean = jnp.sum(y[p], axis=-1, keepdims=True) * (1.0 / RW_HEAD_DIM)
        d = jnp.where(own, y[p] - mean, 0.0)
        var = jnp.sum(d * d, axis=-1, keepdims=True) * (1.0 / RW_HEAD_DIM)
        yn = d * lax.rsqrt(var + GN_EPS)
        yn = yn[:CHUNK] + yn[CHUNK:]
        out = (yn * lng_ref[:, sl] + lnb_ref[:, sl] + bonus_ref[:, sl].astype(F32)) * g_ref[:, sl].astype(F32)
        o_ref[:, sl] = out.astype(o_ref.dtype)


def _rwkv_chunks(prep, lnx_g, lnx_b, batch, seq, casts=()):
    at, rt, bt, kt, bg, kg, vb, bonus, g, gam = prep
    t = at.shape[0]
    nc = seq // CHUNK
    steps = batch * nc
    gam = gam.reshape(t // CHUNK, 1, RW_WIDTH)
    blk = pl.BlockSpec((CHUNK, RW_WIDTH), lambda b, c: (b * nc + c, 0))
    rowspec = pl.BlockSpec((1, RW_WIDTH), lambda b, c: (0, 0))
    cast_specs = [pl.BlockSpec((w.shape[0] // steps, w.shape[1]), lambda b, c: (b * nc + c, 0)) for w in casts]
    outs = pl.pallas_call(
        functools.partial(_rwkv_chunk_kernel, n_cast=len(casts)),
        out_shape=[jax.ShapeDtypeStruct((t, RW_WIDTH), BF16)] + [jax.ShapeDtypeStruct(w.shape, BF16) for w in casts],
        grid=(batch, nc),
        in_specs=([blk] * 9 + [pl.BlockSpec((1, 1, RW_WIDTH), lambda b, c: (b * nc + c, 0, 0)), rowspec, rowspec]
                  + cast_specs),
        out_specs=[blk] + cast_specs,
        scratch_shapes=[pltpu.VMEM((RW_HEADS // 2, PAIR, PAIR), F32)],
        compiler_params=_cparams("arbitrary", "arbitrary"),
        name="rwkv_chunks",
    )(at, rt, bt, kt, bg, kg, vb, bonus, g, gam, lnx_g.reshape(1, RW_WIDTH), lnx_b.reshape(1, RW_WIDTH), *casts)
    return outs[0], outs[1:]


def _cast_kernel(x_ref, o_ref):
    o_ref[...] = x_ref[...].astype(o_ref.dtype)


def _cast_bf16(w, rows):
    n, m = w.shape
    return pl.pallas_call(
        _cast_kernel,
        out_shape=jax.ShapeDtypeStruct((n, m), BF16),
        grid=(n // rows,),
        in_specs=[pl.BlockSpec((rows, m), lambda i: (i, 0))],
        out_specs=pl.BlockSpec((rows, m), lambda i: (i, 0)),
        compiler_params=_cparams("parallel"),
        name="cast_bf16",
    )(w)


def _merge_kernel(h_ref, ya_ref, yb_ref, wga_ref, wgb_ref, pa_ref, pb_ref, o_ref):
    h = h_ref[...]
    ga = jax.nn.sigmoid(_dot(h, wga_ref[...]))
    gb = jax.nn.sigmoid(_dot(h, wgb_ref[...]))
    o_ref[...] = (ga * _dot(ya_ref[...], pa_ref[...]) + gb * _dot(yb_ref[...], pb_ref[...])).astype(o_ref.dtype)


def _merge(h, ya, yb, wga, wgb, pa, pb, tm, tn):
    t, d = h.shape
    n = wga.shape[1]
    ka, kb = ya.shape[1], yb.shape[1]
    return pl.pallas_call(
        _merge_kernel,
        out_shape=jax.ShapeDtypeStruct((t, n), BF16),
        grid=(t // tm, n // tn),
        in_specs=[pl.BlockSpec((tm, d), lambda i, j: (i, 0)),
                  pl.BlockSpec((tm, ka), lambda i, j: (i, 0)),
                  pl.BlockSpec((tm, kb), lambda i, j: (i, 0)),
                  pl.BlockSpec((d, tn), lambda i, j: (0, j)),
                  pl.BlockSpec((d, tn), lambda i, j: (0, j)),
                  pl.BlockSpec((ka, tn), lambda i, j: (0, j)),
                  pl.BlockSpec((kb, tn), lambda i, j: (0, j))],
        out_specs=pl.BlockSpec((tm, tn), lambda i, j: (i, j)),
        compiler_params=_cparams("parallel", "parallel"),
        name="gated_merge",
    )(h, ya, yb, wga, wgb, pa, pb)


def _first_lane_where(cond, lane):
    return jnp.min(jnp.where(cond, lane, LANES), axis=-1, keepdims=True)


def _outproj_kernel(m_ref, x_ref, wo_ref, g2_ref, wrh_ref, wrl_ref, rb_ref, x1_ref, h2_ref, route_ref):
    x1 = x_ref[...] + _dot(m_ref[...], wo_ref[...])
    x1_ref[...] = x1
    ms = jnp.mean(x1 * x1, axis=-1, keepdims=True)
    h2 = x1 * lax.rsqrt(ms + NORM_EPS) * g2_ref[...]
    h2_ref[...] = h2
    hh = h2.astype(BF16)
    hl = (h2 - hh.astype(F32)).astype(BF16)
    wrh = wrh_ref[...]
    lg = _dot(hh, wrh) + _dot(hl, wrh) + _dot(hh, wrl_ref[...]) + rb_ref[...]

    lane = lax.broadcasted_iota(jnp.int32, lg.shape, 1)
    neg = -jnp.inf
    is_g = lane < N_GROUPS
    mg = jnp.max(jnp.where(is_g, lg, neg), axis=-1, keepdims=True)
    eg = jnp.where(is_g, jnp.exp(lg - mg), 0.0)
    pg = eg / jnp.sum(eg, axis=-1, keepdims=True)
    p_g_top = jnp.max(pg, axis=-1, keepdims=True)
    g_idx = _first_lane_where(is_g & (pg == p_g_top), lane)
    lo = N_GROUPS + g_idx * EXPERTS_PER_GROUP
    sel = (lane >= lo) & (lane < lo + EXPERTS_PER_GROUP)
    me = jnp.max(jnp.where(sel, lg, neg), axis=-1, keepdims=True)
    ee = jnp.where(sel, jnp.exp(lg - me), 0.0)
    pe = ee / jnp.sum(ee, axis=-1, keepdims=True)
    pe = jnp.where(sel, pe, -1.0)
    v1 = jnp.max(pe, axis=-1, keepdims=True)
    i1 = _first_lane_where(pe == v1, lane)
    pe2 = jnp.where(lane == i1, -1.0, pe)
    v2 = jnp.max(pe2, axis=-1, keepdims=True)
    i2 = _first_lane_where(pe2 == v2, lane)
    den = v1 + v2
    route = jnp.where(lane == 0, p_g_top * v1 / den,
                      jnp.where(lane == 1, p_g_top * v2 / den,
                                jnp.where(lane == 2, (i1 - N_GROUPS).astype(F32),
                                          jnp.where(lane == 3, (i2 - N_GROUPS).astype(F32), 0.0))))
    route_ref[...] = route


def _outproj(merged, x, wo, g2, wr_hi, wr_lo, rbias, tm):
    t, d = x.shape
    nr = wr_hi.shape[1]
    return pl.pallas_call(
        _outproj_kernel,
        out_shape=[jax.ShapeDtypeStruct((t, d), F32), jax.ShapeDtypeStruct((t, d), F32),
                   jax.ShapeDtypeStruct((t, nr), F32)],
        grid=(t // tm,),
        in_specs=[pl.BlockSpec((tm, d), lambda i: (i, 0)),
                  pl.BlockSpec((tm, d), lambda i: (i, 0)),
                  pl.BlockSpec((d, d), lambda i: (0, 0)),
                  pl.BlockSpec((1, d), lambda i: (0, 0)),
                  pl.BlockSpec((d, nr), lambda i: (0, 0)),
                  pl.BlockSpec((d, nr), lambda i: (0, 0)),
                  pl.BlockSpec((1, nr), lambda i: (0, 0))],
        out_specs=[pl.BlockSpec((tm, d), lambda i: (i, 0)),
                   pl.BlockSpec((tm, d), lambda i: (i, 0)),
                   pl.BlockSpec((tm, nr), lambda i: (i, 0))],
        compiler_params=_cparams("parallel"),
        name="outproj_norm_router",
    )(merged, x, wo, g2.reshape(1, d), wr_hi, wr_lo, rbias)


IDX_SLOTS = 3


def _moe_kernel(be_ref, nused_ref, nvalid_ref, tok_hbm, dst_hbm, h_hbm, wg_ref, wu_ref, wd_ref, y_hbm,
                xbuf, obuf, tok_s, dst_s, idx_sem, g_sem, s_sem, *, rows):
    b = pl.program_id(0)
    n_used = nused_ref[0]
    last = n_used - 1

    def idx_copies(blk, sl):
        return (pltpu.make_async_copy(tok_hbm.at[blk], tok_s.at[sl], idx_sem.at[sl, 0]),
                pltpu.make_async_copy(dst_hbm.at[blk], dst_s.at[sl], idx_sem.at[sl, 1]))

    def gather_row(r, isl, xsl):
        return pltpu.make_async_copy(h_hbm.at[tok_s[isl, r]], xbuf.at[xsl, r], g_sem.at[xsl])

    def gather_all(xsl):
        return pltpu.make_async_copy(h_hbm.at[pl.ds(0, rows)], xbuf.at[xsl], g_sem.at[xsl])

    def scatter_row(r, isl, osl):
        return pltpu.make_async_copy(obuf.at[osl, r], y_hbm.at[dst_s[isl, r]], s_sem.at[osl])

    def wait_scatter(blk, osl):
        n = nvalid_ref[blk]

        @pl.when(n == rows)
        def _():
            pltpu.make_async_copy(obuf.at[osl], y_hbm.at[pl.ds(0, rows)], s_sem.at[osl]).wait()

        @pl.when(n < rows)
        def _():
            def one(r, c):
                pltpu.make_async_copy(obuf.at[osl, 0], y_hbm.at[0], s_sem.at[osl]).wait()
                return c

            lax.fori_loop(0, n, one, 0)

    @pl.when(b == 0)
    def _():
        for cp in idx_copies(0, 0):
            cp.start()
        for cp in idx_copies(0, 0):
            cp.wait()
        for cp in idx_copies(jnp.minimum(1, last), 1):
            cp.start()

        def issue(r, c):
            gather_row(r, 0, 0).start()
            return c

        lax.fori_loop(0, rows, issue, 0, unroll=8)

    @pl.when((b > 1) & (b < n_used))
    def _():
        wait_scatter(b - 2, b % 2)

    for par in range(2):
        @pl.when((b < n_used) & (b % 2 == par))
        def _(par=par):
            nxt_i, ld_i = (b + 1) % IDX_SLOTS, (b + 2) % IDX_SLOTS
            for cp in idx_copies(0, nxt_i):
                cp.wait()
            for r in range(rows):
                gather_row(r, nxt_i, 1 - par).start()
            for cp in idx_copies(jnp.minimum(b + 2, last), ld_i):
                cp.start()

            gather_all(par).wait()
            x = xbuf[par].astype(BF16)
            gate = _dot(x, wg_ref[...])
            up = _dot(x, wu_ref[...])
            mid = (gate * jax.nn.sigmoid(gate) * up).astype(BF16)
            obuf[par] = _dot(mid, wd_ref[...])

        @pl.when((b < n_used) & (b % 2 == par) & (nvalid_ref[b] == rows))
        def _(par=par):
            for r in range(rows):
                scatter_row(r, b % IDX_SLOTS, par).start()

    @pl.when((b < n_used) & (nvalid_ref[b] < rows))
    def _():
        def issue(r, c):
            scatter_row(r, b % IDX_SLOTS, b % 2).start()
            return c

        lax.fori_loop(0, nvalid_ref[b], issue, 0)

    @pl.when(b == last)
    def _():
        @pl.when(b > 0)
        def _():
            wait_scatter(b - 1, (b - 1) % 2)

        wait_scatter(b, b % 2)
        gather_all((b + 1) % 2).wait()
        for cp in idx_copies(0, (b + 2) % IDX_SLOTS):
            cp.wait()


def _moe(h2, row_tok, row_dst, block_e, n_used, nvalid, wg, wu, wd, rows):
    t, d = h2.shape
    n_blocks = row_tok.shape[0]
    a = t * TOP_K
    f = wg.shape[2]
    kern = functools.partial(_moe_kernel, rows=rows)
    grid_spec = pltpu.PrefetchScalarGridSpec(
        num_scalar_prefetch=3,
        grid=(n_blocks,),
        in_specs=[pl.BlockSpec(memory_space=pl.ANY),
                  pl.BlockSpec(memory_space=pl.ANY),
                  pl.BlockSpec(memory_space=pl.ANY),
                  pl.BlockSpec((None, d, f), lambda b, be, nu, nv: (be[b], 0, 0)),
                  pl.BlockSpec((None, d, f), lambda b, be, nu, nv: (be[b], 0, 0)),
                  pl.BlockSpec((None, f, d), lambda b, be, nu, nv: (be[b], 0, 0))],
        out_specs=pl.BlockSpec(memory_space=pl.ANY),
        scratch_shapes=[pltpu.VMEM((2, rows, d), F32),
                        pltpu.VMEM((2, rows, d), F32),
                        pltpu.SMEM((IDX_SLOTS, rows), jnp.int32),
                        pltpu.SMEM((IDX_SLOTS, rows), jnp.int32),
                        pltpu.SemaphoreType.DMA((IDX_SLOTS, 2)),
                        pltpu.SemaphoreType.DMA((2,)),
                        pltpu.SemaphoreType.DMA((2,))],
    )
    return pl.pallas_call(
        kern,
        out_shape=jax.ShapeDtypeStruct((a, d), F32),
        grid_spec=grid_spec,
        compiler_params=_cparams("arbitrary"),
        name="moe_experts",
    )(block_e, n_used, nvalid, row_tok, row_dst, h2, wg, wu, wd)


def _combine_kernel(x1_ref, y0_ref, y1_ref, route_ref, o_ref):
    gts = route_ref[...]
    o_ref[...] = x1_ref[...] + gts[:, 0:1] * y0_ref[...] + gts[:, 1:2] * y1_ref[...]


def _combine(x1, y, route, tm):
    t, d = x1.shape
    nb = t // tm
    return pl.pallas_call(
        _combine_kernel,
        out_shape=jax.ShapeDtypeStruct((t, d), F32),
        grid=(nb,),
        in_specs=[pl.BlockSpec((tm, d), lambda i: (i, 0)),
                  pl.BlockSpec((tm, d), lambda i: (i, 0)),
                  pl.BlockSpec((tm, d), lambda i: (nb + i, 0)),
                  pl.BlockSpec((tm, LANES), lambda i: (i, 0))],
        out_specs=pl.BlockSpec((tm, d), lambda i: (i, 0)),
        compiler_params=_cparams("parallel"),
        name="moe_combine",
    )(x1, y, y, route)


def _routing_tables(route, rows):
    t = route.shape[0]
    a = t * TOP_K
    expert = route[:, 2:2 + TOP_K].astype(jnp.int32)
    flat_e = expert.reshape(a)
    ids = jnp.arange(a, dtype=jnp.int32)
    order = jnp.sort(flat_e * a + ids) % a
    counts = jnp.sum((flat_e[:, None] == jnp.arange(N_EXPERTS, dtype=jnp.int32)[None, :]).astype(jnp.int32), axis=0)
    padded = (counts + rows - 1) // rows * rows
    start = jnp.cumsum(counts) - counts
    pend = jnp.cumsum(padded)
    pstart = pend - padded
    n_rows = a + N_EXPERTS * rows
    n_blocks = n_rows // rows
    blk_row0 = jnp.arange(n_blocks, dtype=jnp.int32) * rows
    block_e = jnp.minimum(jnp.sum((pend[None, :] <= blk_row0[:, None]).astype(jnp.int32), axis=1), N_EXPERTS - 1)
    rho = jnp.arange(n_rows, dtype=jnp.int32)
    be = jnp.repeat(block_e, rows)
    off = rho - pstart[be]
    valid = (off >= 0) & (off < counts[be])
    src = order[jnp.clip(start[be] + off, 0, a - 1)]
    row_tok = jnp.where(valid, src // TOP_K, 0)
    row_dst = jnp.where(valid, (src % TOP_K) * t + src // TOP_K, 0)
    n_used = (pend[-1] // rows).astype(jnp.int32).reshape(1)
    nvalid = jnp.clip(counts[block_e] - (blk_row0 - pstart[block_e]), 0, rows).astype(jnp.int32)
    return (row_tok.reshape(n_blocks, rows), row_dst.reshape(n_blocks, rows), block_e.astype(jnp.int32), n_used, nvalid)


def _tiles(t, seq):
    return dict(
        norm_tm=min(512, t),
        proj_tm=min(1024, seq), proj_tn=512,
        attn_tq=min(256, seq), attn_tk=min(1024, seq), attn_heads=2,
        prep_tm=min(256, seq), prep_tn=1024,
        merge_tm=min(1024, t), merge_tn=512,
        out_tm=min(512, t),
        moe_rows=256,
        comb_tm=min(512, t),
    )


def _pad_rows(w, n):
    return jnp.pad(w, ((0, n - w.shape[0]), (0, 0)))


def _pad_cols(w, n):
    return jnp.pad(w, ((0, 0), (0, n - w.shape[1])))


def kernel(x, norm1_g, w_in, q_norm_g, k_norm_g, lam_q1, lam_k1, lam_q2, lam_k2, subln_g, shift_mu, w0, w_up, a0, a_up, g_up, k_k, k_a, r_k, lnx_g, lnx_b, proj_a, proj_b, w_out, norm2_g, router_g, router_g_b, router_e, router_e_b, w_gate_e, w_up_e, w_down_e):
    batch, seq, d = x.shape
    t = batch * seq
    depth = norm1_g.shape[0]
    tl = _tiles(t, seq)
    qkw = DA_HEADS * 2 * DA_HEAD_DIM
    vw = DA_HEADS * DA_V_DIM
    c_q, c_k, c_v = 0, qkw, 2 * qkw
    c_rw = c_v + vw
    c_dw = c_rw + 3 * RW_WIDTH
    c_da = c_dw + DECAY_LORA
    c_dg = c_da + AAA_LORA
    c_ga = c_dg + GATE_LORA
    c_gb = c_ga + d
    cos_t, sinm_t, sinp_t = _rope_tables(seq)
    xf = x.reshape(t, d)

    for l in range(depth):
        lam_init = 0.8 - 0.6 * math.exp(-0.3 * l)
        wl = w_in[l]
        w_qk = wl[:, c_q:c_v].astype(BF16)
        w_v = wl[:, c_v:c_rw].astype(BF16)
        w_rw = jnp.concatenate([wl[:, c_rw:c_dw], _pad_cols(wl[:, c_dw:c_da], LANES),
                                _pad_cols(wl[:, c_da:c_dg], LANES), wl[:, c_dg:c_ga]], axis=1).astype(BF16)
        w_ga = wl[:, c_ga:c_gb].astype(BF16)
        w_gb = wl[:, c_gb:].astype(BF16)
        mu = shift_mu[l]
        o_dw = 3 * RW_WIDTH
        mu_l = jnp.concatenate([mu[:o_dw], jnp.pad(mu[o_dw:o_dw + DECAY_LORA], (0, LANES - DECAY_LORA)),
                                jnp.pad(mu[o_dw + DECAY_LORA:o_dw + DECAY_LORA + AAA_LORA], (0, LANES - AAA_LORA)),
                                mu[o_dw + DECAY_LORA + AAA_LORA:]]).reshape(1, -1)
        gain_row = jnp.concatenate([jnp.tile(q_norm_g[l], 2 * DA_HEADS) * (DA_HEAD_DIM ** -0.5 * LOG2_E),
                                    jnp.tile(k_norm_g[l], 2 * DA_HEADS)]).reshape(1, 2 * qkw)
        lam_params = jnp.stack([lam_q1[l], lam_k1[l], lam_q2[l], lam_k2[l]])
        zrow = jnp.zeros((RW_WIDTH,), F32)
        rw_params = jnp.stack([w0[l], a0[l], k_k[l], k_a[l], r_k[l].reshape(-1), zrow, zrow, zrow])

        h = _rmsnorm(xf, norm1_g[l], NORM_EPS, tl["norm_tm"])
        qk = _qk_proj(h, w_qk, gain_row, cos_t, sinm_t, sinp_t, seq, tl["proj_tm"], tl["proj_tn"])
        v = _matmul(h, w_v, BF16, tl["proj_tm"], tl["proj_tn"], "v_proj")
        ya = _diff_attention(qk, v, lam_params, subln_g[l], batch, seq, lam_init, tl["attn_tq"], tl["attn_tk"],
                             tl["attn_heads"])

        rw = _matmul(h, w_rw, F32, tl["proj_tm"], tl["proj_tn"], "rw_proj")
        prep = _rwkv_prep(rw, mu_l, rw_params, _pad_rows(w_up[l], LANES).astype(BF16),
                          _pad_rows(a_up[l], LANES).astype(BF16), g_up[l].astype(BF16),
                          seq, tl["prep_tm"], tl["prep_tn"])
        ew = (w_gate_e[l], w_up_e[l], w_down_e[l])
        ew2d = tuple(w.reshape(-1, w.shape[-1]) for w in ew)
        steps = batch * (seq // CHUNK)
        ride = all(w.shape[0] % (steps * 2 * SUBLANES) == 0 and w.size * 4 // steps <= CAST_BLOCK_BYTES for w in ew2d)
        yb, ew_bf16 = _rwkv_chunks(prep, lnx_g[l], lnx_b[l], batch, seq, ew2d if ride else ())
        if not ride:
            ew_bf16 = tuple(_cast_bf16(w, CAST_BLOCK_BYTES // (4 * w.shape[1])) for w in ew2d)
        wg_e, wu_e, wd_e = (c.reshape(w.shape) for c, w in zip(ew_bf16, ew))

        merged = _merge(h, ya, yb, w_ga, w_gb, proj_a[l].astype(BF16), proj_b[l].astype(BF16),
                        tl["merge_tm"], tl["merge_tn"])
        wr = _pad_cols(jnp.concatenate([router_g[l], router_e[l]], axis=1), LANES)
        wr_hi = wr.astype(BF16)
        wr_lo = (wr - wr_hi.astype(F32)).astype(BF16)
        rbias = jnp.pad(jnp.concatenate([router_g_b[l], router_e_b[l]]), (0, LANES - N_GROUPS - N_EXPERTS))
        x1, h2, route = _outproj(merged, xf, w_out[l].astype(BF16), norm2_g[l], wr_hi, wr_lo,
                                 rbias.reshape(1, LANES), tl["out_tm"])

        rows = tl["moe_rows"]
        row_tok, row_dst, block_e, n_used, nvalid = _routing_tables(route, rows)
        y = _moe(h2, row_tok, row_dst, block_e, n_used, nvalid, wg_e, wu_e, wd_e, rows)
        xf = _combine(x1, y, route, tl["comb_tm"])
    return xf.reshape(batch, seq, d)
```

```python
import functools
import math

import jax
import jax.numpy as jnp
from jax import lax
from jax.experimental import pallas as pl
from jax.experimental.pallas import tpu as pltpu

DA_HEADS = 8
DA_HEAD_DIM = 64
DA_V_DIM = 2 * DA_HEAD_DIM
ROT_DIM = DA_HEAD_DIM // 4
ROPE_THETA = 500000.0
SUBLN_EPS = 1e-5
RW_HEADS = 16
RW_HEAD_DIM = 64
RW_WIDTH = RW_HEADS * RW_HEAD_DIM
DECAY_LORA = 96
AAA_LORA = 96
GATE_LORA = 256
GN_EPS = 64e-5
N_GROUPS = 4
EXPERTS_PER_GROUP = 8
N_EXPERTS = N_GROUPS * EXPERTS_PER_GROUP
TOP_K = 2
NORM_EPS = 1e-6

LANES = 128
SUBLANES = 8
VMEM_LIMIT_BYTES = 56 * 1024 * 1024

CHUNK = 64
CAST_BLOCK_BYTES = 2 * 1024 * 1024

F32 = jnp.float32
BF16 = jnp.bfloat16

LOG2_E = math.log2(math.e)

NT_DIMS = (((1,), (1,)), ((), ()))
TN_DIMS = (((0,), (0,)), ((), ()))


def _cparams(*sem):
    return pltpu.CompilerParams(dimension_semantics=tuple(sem), vmem_limit_bytes=VMEM_LIMIT_BYTES)


def _dot(a, b, dims=None, precision=None):
    if dims is None:
        return jnp.dot(a, b, preferred_element_type=F32, precision=precision)
    return lax.dot_general(a, b, dims, preferred_element_type=F32, precision=precision)


def _split3(x):
    h = x.astype(BF16)
    r = x - h.astype(F32)
    m = r.astype(BF16)
    l = (r - m.astype(F32)).astype(BF16)
    return h, m, l


def _dot_exact_rhs(sel_bf16, x_f32):
    h, m, l = _split3(x_f32)
    return _dot(sel_bf16, h) + _dot(sel_bf16, m) + _dot(sel_bf16, l)


def _group_sum64(x, bd, terms):
    parts = _split3(x)[:terms]
    outs = []
    for s in range(x.shape[1] // LANES):
        sl = slice(s * LANES, (s + 1) * LANES)
        acc = _dot(parts[0][:, sl], bd)
        for part in parts[1:]:
            acc = acc + _dot(part[:, sl], bd)
        outs.append(acc)
    return outs[0] if len(outs) == 1 else jnp.concatenate(outs, axis=1)


def _block_diag_ones(n, blk, dtype=BF16):
    r = lax.broadcasted_iota(jnp.int32, (n, n), 0) // blk
    c = lax.broadcasted_iota(jnp.int32, (n, n), 1) // blk
    return jnp.where(r == c, 1.0, 0.0).astype(dtype)


def _rmsnorm_kernel(x_ref, g_ref, o_ref, *, eps):
    x = x_ref[...]
    ms = jnp.mean(x * x, axis=-1, keepdims=True)
    o_ref[...] = (x * lax.rsqrt(ms + eps) * g_ref[...]).astype(o_ref.dtype)


def _rmsnorm(x, g, eps, tm):
    t, d = x.shape
    return pl.pallas_call(
        functools.partial(_rmsnorm_kernel, eps=eps),
        out_shape=jax.ShapeDtypeStruct((t, d), BF16),
        grid=(t // tm,),
        in_specs=[pl.BlockSpec((tm, d), lambda i: (i, 0)),
                  pl.BlockSpec((1, d), lambda i: (0, 0))],
        out_specs=pl.BlockSpec((tm, d), lambda i: (i, 0)),
        compiler_params=_cparams("parallel"),
        name="rmsnorm",
    )(x, g.reshape(1, d))


def _matmul_kernel(a_ref, w_ref, o_ref):
    o_ref[...] = _dot(a_ref[...], w_ref[...]).astype(o_ref.dtype)


def _matmul(a, w, out_dtype, tm, tn, name):
    t, k = a.shape
    n = w.shape[1]
    return pl.pallas_call(
        _matmul_kernel,
        out_shape=jax.ShapeDtypeStruct((t, n), out_dtype),
        grid=(t // tm, n // tn),
        in_specs=[pl.BlockSpec((tm, k), lambda i, j: (i, 0)),
                  pl.BlockSpec((k, tn), lambda i, j: (0, j))],
        out_specs=pl.BlockSpec((tm, tn), lambda i, j: (i, j)),
        compiler_params=_cparams("parallel", "parallel"),
        name=name,
    )(a, w)


def _qk_proj_kernel(a_ref, w_ref, gain_ref, cos_ref, sinm_ref, sinp_ref, o_ref, *, tn):
    acc = _dot(a_ref[...], w_ref[...])
    bd = _block_diag_ones(LANES, DA_HEAD_DIM)
    ms = _group_sum64(acc * acc, bd, 1) * (1.0 / DA_HEAD_DIM)
    xn = acc * lax.rsqrt(ms + NORM_EPS) * gain_ref[...]
    reps = tn // LANES
    cos = jnp.tile(cos_ref[...], (1, reps))
    sinm = jnp.tile(sinm_ref[...], (1, reps))
    sinp = jnp.tile(sinp_ref[...], (1, reps))
    half = ROT_DIM // 2
    hi = pltpu.roll(xn, tn - half, 1)
    lo = pltpu.roll(xn, half, 1)
    o_ref[...] = (xn * cos + hi * sinm + lo * sinp).astype(o_ref.dtype)


def _qk_proj(h, w_qk, gain_row, cos_t, sinm_t, sinp_t, seq, tm, tn):
    t, k = h.shape
    n = w_qk.shape[1]
    nseq = seq // tm
    return pl.pallas_call(
        functools.partial(_qk_proj_kernel, tn=tn),
        out_shape=jax.ShapeDtypeStruct((t, n), BF16),
        grid=(t // tm, n // tn),
        in_specs=[pl.BlockSpec((tm, k), lambda i, j: (i, 0)),
                  pl.BlockSpec((k, tn), lambda i, j: (0, j)),
                  pl.BlockSpec((1, tn), lambda i, j: (0, j)),
                  pl.BlockSpec((tm, LANES), lambda i, j: (i % nseq, 0)),
                  pl.BlockSpec((tm, LANES), lambda i, j: (i % nseq, 0)),
                  pl.BlockSpec((tm, LANES), lambda i, j: (i % nseq, 0))],
        out_specs=pl.BlockSpec((tm, tn), lambda i, j: (i, j)),
        compiler_params=_cparams("parallel", "parallel"),
        name="qk_proj",
    )(h, w_qk, gain_row, cos_t, sinm_t, sinp_t)


def _rope_tables(seq):
    half = ROT_DIM // 2
    inv_freq = ROPE_THETA ** (-jnp.arange(0, ROT_DIM, 2, dtype=F32) / ROT_DIM)
    ang = jnp.arange(seq, dtype=F32)[:, None] * inv_freq[None, :]
    cos, sin = jnp.cos(ang), jnp.sin(ang)
    ones = jnp.ones((seq, DA_HEAD_DIM - ROT_DIM), F32)
    zeros = jnp.zeros((seq, DA_HEAD_DIM - ROT_DIM), F32)
    zh = jnp.zeros((seq, half), F32)
    cos64 = jnp.concatenate([cos, cos, ones], axis=1)
    sinm64 = jnp.concatenate([-sin, zh, zeros], axis=1)
    sinp64 = jnp.concatenate([zh, sin, zeros], axis=1)
    return tuple(jnp.concatenate([a, a], axis=1) for a in (cos64, sinm64, sinp64))


def _diff_attn_kernel(lam_ref, q_ref, k_ref, v_ref, g_ref, o_ref, m_ref, l_ref, a_ref, *, tq, tk, lam_init, nh):
    i = pl.program_id(2)
    lane = lax.broadcasted_iota(jnp.int32, (1, DA_V_DIM), 1)
    qs = []
    for h in range(nh):
        q = q_ref[:, h * DA_V_DIM:(h + 1) * DA_V_DIM]
        zero = jnp.zeros_like(q)
        qs.append(jnp.where(lane < DA_HEAD_DIM, q, zero))
        qs.append(jnp.where(lane >= DA_HEAD_DIM, q, zero))
    nc = 2 * nh
    m_ref[...] = jnp.full(m_ref.shape, -jnp.inf, F32)
    l_ref[...] = jnp.zeros(l_ref.shape, F32)
    a_ref[...] = jnp.zeros(a_ref.shape, F32)

    def step(j, mask):
        off = pl.multiple_of(j * tk, tk)
        ks = [k_ref[pl.ds(off, tk), h * DA_V_DIM:(h + 1) * DA_V_DIM] for h in range(nh)]
        vs = [v_ref[pl.ds(off, tk), h * DA_V_DIM:(h + 1) * DA_V_DIM] for h in range(nh)]
        s = [_dot(qs[c], ks[c // 2], NT_DIMS) for c in range(nc)]
        for c in range(nc):
            sc = s[c] if mask is None else jnp.where(mask, s[c], -jnp.inf)
            m_old = m_ref[c]
            m_new = jnp.maximum(m_old, jnp.max(sc, axis=-1, keepdims=True))
            alpha = jnp.exp2(m_old - m_new)
            p = jnp.exp2(sc - m_new)
            part = p[:, :LANES]
            for kk in range(1, tk // LANES):
                part = part + p[:, kk * LANES:(kk + 1) * LANES]
            l_ref[c] = alpha * l_ref[c] + part
            a_ref[c] = alpha * a_ref[c] + _dot(p.astype(BF16), vs[c // 2])
            m_ref[c] = m_new

    n_full = (i * tq) // tk

    def full_body(j, c):
        step(j, None)
        return c

    lax.fori_loop(0, n_full, full_body, 0)

    row = i * tq + lax.broadcasted_iota(jnp.int32, (tq, tk), 0)
    for d in range(max(1, tq // tk)):
        j = n_full + d
        col = j * tk + lax.broadcasted_iota(jnp.int32, (tq, tk), 1)
        step(j, col <= row)

    lq1, lk1, lq2, lk2 = (lam_ref[r:r + 1, :] for r in range(4))
    lam = (jnp.exp(jnp.sum(lq1 * lk1, axis=-1, keepdims=True))
           - jnp.exp(jnp.sum(lq2 * lk2, axis=-1, keepdims=True)) + lam_init)
    for h in range(nh):
        l1 = jnp.sum(l_ref[2 * h], axis=-1, keepdims=True)
        l2 = jnp.sum(l_ref[2 * h + 1], axis=-1, keepdims=True)
        o = a_ref[2 * h] / l1 - lam * (a_ref[2 * h + 1] / l2)
        ms = jnp.mean(o * o, axis=-1, keepdims=True)
        o = o * lax.rsqrt(ms + SUBLN_EPS) * (g_ref[...] * (1.0 - lam_init))
        o_ref[:, h * DA_V_DIM:(h + 1) * DA_V_DIM] = o.astype(o_ref.dtype)


def _diff_attention(qk, v, lam_params, subln_g, batch, seq, lam_init, tq, tk, nh):
    t = qk.shape[0]
    nq = seq // tq
    w = nh * DA_V_DIM
    hg = DA_HEADS // nh
    kern = functools.partial(_diff_attn_kernel, tq=tq, tk=tk, lam_init=lam_init, nh=nh)
    return pl.pallas_call(
        kern,
        out_shape=jax.ShapeDtypeStruct((t, DA_HEADS * DA_V_DIM), BF16),
        grid=(batch, hg, nq),
        in_specs=[pl.BlockSpec((4, DA_HEAD_DIM), lambda b, h, i: (0, 0)),
                  pl.BlockSpec((tq, w), lambda b, h, i: (b * nq + i, h)),
                  pl.BlockSpec((seq, w), lambda b, h, i: (b, hg + h)),
                  pl.BlockSpec((seq, w), lambda b, h, i: (b, h)),
                  pl.BlockSpec((1, DA_V_DIM), lambda b, h, i: (0, 0))],
        out_specs=pl.BlockSpec((tq, w), lambda b, h, i: (b * nq + i, h)),
        scratch_shapes=[pltpu.VMEM((2 * nh, tq, 1), F32), pltpu.VMEM((2 * nh, tq, LANES), F32),
                        pltpu.VMEM((2 * nh, tq, DA_V_DIM), F32)],
        compiler_params=_cparams("parallel", "parallel", "parallel"),
        name="diff_attention",
    )(lam_params, qk, qk, v, subln_g.reshape(1, DA_V_DIM))


DECAY_SCALE = math.exp(-0.5)


def _rwkv_prep_kernel(r_ref, k_ref, v_ref, lo_ref, pr_ref, pk_ref, pv_ref, plo_ref,
                      mur_ref, muk_ref, muv_ref, mulo_ref, par_ref, wup_ref, aup_ref, gup_ref,
                      at_ref, rt_ref, bt_ref, kt_ref, bg_ref, kg_ref, vb_ref, bonus_ref, g_ref, gam_ref,
                      *, tm, nseq):
    i = pl.program_id(0)
    seq_start = (i % nseq) == 0
    row0 = lax.broadcasted_iota(jnp.int32, (tm, 1), 0) == 0

    def shifted(x_ref, p_ref, mu_ref):
        x = x_ref[...]
        last = jnp.where(seq_start, 0.0, p_ref[SUBLANES - 1:SUBLANES, :])
        prev = jnp.where(row0, last, pltpu.roll(x, 1, 0))
        return x + (prev - x) * mu_ref[...]

    r = shifted(r_ref, pr_ref, mur_ref)
    k = shifted(k_ref, pk_ref, muk_ref)
    v = shifted(v_ref, pv_ref, muv_ref)
    lo = shifted(lo_ref, plo_ref, mulo_ref)
    dw, da, dg = lo[:, :LANES], lo[:, LANES:2 * LANES], lo[:, 2 * LANES:]
    w0, a0, k_k, k_a, r_k = (par_ref[j:j + 1, :] for j in range(5))

    u = w0 + _dot(jnp.tanh(dw).astype(BF16), wup_ref[...])
    ld = -DECAY_SCALE * jax.nn.sigmoid(u)
    a = jax.nn.sigmoid(a0 + _dot(da.astype(BF16), aup_ref[...]))
    g = _dot(jax.nn.sigmoid(dg).astype(BF16), gup_ref[...])

    bd = _block_diag_ones(LANES, RW_HEAD_DIM)
    kk = k * k_k
    kk = kk / jnp.maximum(jnp.sqrt(_group_sum64(kk * kk, bd, 2)), 1e-12)
    k2 = k * (1.0 + (a - 1.0) * k_a)
    bonus = _group_sum64(r * k2 * r_k, bd, 2) * v

    t_i = lax.broadcasted_iota(jnp.int32, (tm, tm), 0)
    s_i = lax.broadcasted_iota(jnp.int32, (tm, tm), 1)
    same = (t_i // CHUNK) == (s_i // CHUNK)
    tri = jnp.where(same & (s_i <= t_i), 1.0, 0.0).astype(BF16)
    rest = jnp.where(same & (s_i > t_i), 1.0, 0.0).astype(BF16)
    c_i = lax.broadcasted_iota(jnp.int32, (tm // CHUNK, tm), 0)
    cs_i = lax.broadcasted_iota(jnp.int32, (tm // CHUNK, tm), 1)
    whole = jnp.where(cs_i // CHUNK == c_i, 1.0, 0.0).astype(BF16)

    ldh, ldm, ldl = _split3(ld)

    def sel(m):
        return _dot(m, ldh) + _dot(m, ldm) + _dot(m, ldl)

    cum = sel(tri)
    e_neg = jnp.exp(-cum)
    e_rem = jnp.exp(sel(rest))
    b = kk * a
    at_ref[...] = (-kk * jnp.exp(cum - ld)).astype(BF16)
    rt_ref[...] = (r * jnp.exp(cum)).astype(BF16)
    bt_ref[...] = (b * e_neg).astype(BF16)
    kt_ref[...] = (k2 * e_neg).astype(BF16)
    bg_ref[...] = (b * e_rem).astype(BF16)
    kg_ref[...] = (k2 * e_rem).astype(BF16)
    vb_ref[...] = v.astype(BF16)
    bonus_ref[...] = bonus.astype(BF16)
    g_ref[...] = g.astype(BF16)
    gam_ref[0] = jnp.exp(sel(whole))


def _rwkv_prep(rw, mu, params, wup, aup, gup, seq, tm, tn):
    t = rw.shape[0]
    nj = RW_WIDTH // tn
    lo_w = 4 * LANES
    lo_blk = (3 * RW_WIDTH) // lo_w
    nseq = seq // tm
    rpb = tm // SUBLANES

    def cur(off):
        return pl.BlockSpec((tm, tn), lambda i, j: (i, off * nj + j))

    def prv(off):
        return pl.BlockSpec((SUBLANES, tn), lambda i, j: (jnp.maximum(i * rpb - 1, 0), off * nj + j))

    def row(off):
        return pl.BlockSpec((1, tn), lambda i, j: (0, off * nj + j))

    in_specs = [cur(0), cur(1), cur(2), pl.BlockSpec((tm, lo_w), lambda i, j: (i, lo_blk)),
                prv(0), prv(1), prv(2),
                pl.BlockSpec((SUBLANES, lo_w), lambda i, j: (jnp.maximum(i * rpb - 1, 0), lo_blk)),
                row(0), row(1), row(2), pl.BlockSpec((1, lo_w), lambda i, j: (0, lo_blk)),
                pl.BlockSpec((SUBLANES, tn), lambda i, j: (0, j)),
                pl.BlockSpec((LANES, tn), lambda i, j: (0, j)),
                pl.BlockSpec((LANES, tn), lambda i, j: (0, j)),
                pl.BlockSpec((2 * LANES, tn), lambda i, j: (0, j))]
    out_blk = pl.BlockSpec((tm, tn), lambda i, j: (i, j))
    outs = [jax.ShapeDtypeStruct((t, RW_WIDTH), BF16)] * 9
    outs.append(jax.ShapeDtypeStruct((t // tm, tm // CHUNK, RW_WIDTH), F32))
    out_specs = [out_blk] * 9 + [pl.BlockSpec((1, tm // CHUNK, tn), lambda i, j: (i, 0, j))]
    return pl.pallas_call(
        functools.partial(_rwkv_prep_kernel, tm=tm, nseq=nseq),
        out_shape=outs,
        grid=(t // tm, nj),
        in_specs=in_specs,
        out_specs=out_specs,
        compiler_params=_cparams("parallel", "parallel"),
        name="rwkv_prep",
    )(rw, rw, rw, rw, rw, rw, rw, rw, mu, mu, mu, mu, params, wup, aup, gup)


PAIR = 2 * RW_HEAD_DIM


def _dot_bf16(a, b, dims=None):
    return _dot(a.astype(BF16), b.astype(BF16), dims)


def _split2(x):
    h = x.astype(BF16)
    return h, (x - h.astype(F32)).astype(BF16)


def _dot_split_lhs(a, b, dims=None):
    ah, al = _split2(a)
    bh = b.astype(BF16)
    return _dot(ah, bh, dims) + _dot(al, bh, dims)


def _rwkv_chunk_kernel(at_ref, rt_ref, bt_ref, kt_ref, bg_ref, kg_ref, v_ref, bonus_ref, g_ref,
                       gam_ref, lng_ref, lnb_ref, *rest, n_cast, cps):
    cast_in, o_ref, cast_out, s_ref = rest[:n_cast], rest[n_cast], rest[n_cast + 1:2 * n_cast + 1], rest[-1]
    c = pl.program_id(1)

    for src, dst in zip(cast_in, cast_out):
        dst[...] = src[...].astype(dst.dtype)

    @pl.when(c == 0)
    def _():
        s_ref[...] = jnp.zeros(s_ref.shape, F32)

    lane = lax.broadcasted_iota(jnp.int32, (1, PAIR), 1)
    first = lane < RW_HEAD_DIM
    rho = lax.broadcasted_iota(jnp.int32, (PAIR, PAIR), 0)
    sig = lax.broadcasted_iota(jnp.int32, (PAIR, PAIR), 1)
    strict, incl, eye = sig < rho, sig <= rho, sig == rho
    own = (rho // RW_HEAD_DIM) == (sig // RW_HEAD_DIM)

    def stacked(x):
        z = jnp.zeros_like(x)
        return jnp.concatenate([jnp.where(first, x, z), jnp.where(first, z, x)], axis=0)

    n_pairs = RW_HEADS // 2
    units = [(ci, p) for ci in range(cps) for p in range(n_pairs)]
    rws = [slice(ci * CHUNK, (ci + 1) * CHUNK) for ci in range(cps)]
    sls = [slice(p * PAIR, (p + 1) * PAIR) for p in range(n_pairs)]
    zero = jnp.zeros((PAIR, PAIR), F32)
    st = [[stacked(ref[rws[ci], sls[p]]) for ref in (at_ref, rt_ref, bt_ref, kt_ref, bg_ref, kg_ref, v_ref)]
          for ci, p in units]
    un = range(len(units))
    prods = [_dot(jnp.concatenate([q[0], q[1]], axis=0), jnp.concatenate([q[2], q[3]], axis=0), NT_DIMS) for q in st]
    lmat = [jnp.where(strict, pr_[:PAIR, :PAIR], zero) for pr_ in prods]
    sak = [jnp.where(strict, pr_[:PAIR, PAIR:], zero) for pr_ in prods]
    lrbk = [jnp.concatenate([jnp.where(incl, pr_[PAIR:, :PAIR], zero), jnp.where(incl, pr_[PAIR:, PAIR:], zero)],
                            axis=1).astype(BF16) for pr_ in prods]
    akv = [_dot(sak[u].astype(BF16), st[u][6]) for u in un]
    ident = jnp.where(eye, 1.0, 0.0)
    li = [_dot_bf16(lmat[u], lmat[u]) for u in un]
    tmat = [ident + lmat[u] for u in un]
    n_steps = CHUNK.bit_length() - 1
    for it in range(1, n_steps):
        if it + 1 < n_steps:
            res = [_dot_bf16(li[u], jnp.concatenate([li[u], tmat[u]], axis=1)) for u in un]
            tmat = [tmat[u] + res[u][:, PAIR:] for u in un]
            li = [res[u][:, :PAIR] for u in un]
        else:
            tmat = [tmat[u] + _dot_bf16(li[u], tmat[u]) for u in un]
    x = [_dot_bf16(tmat[u], jnp.concatenate([st[u][0].astype(F32), akv[u]], axis=1)) for u in un]
    gmat = [jnp.concatenate([x[u], jnp.concatenate([zero, st[u][6].astype(F32)], axis=1)], axis=0).astype(BF16)
            for u in un]
    out1 = [_dot(lrbk[u], gmat[u]) for u in un]
    out2 = [_dot(gmat[u], jnp.concatenate([st[u][4], st[u][5]], axis=0), TN_DIMS) for u in un]
    qe = [out1[u][:, :PAIR] + st[u][1].astype(F32) for u in un]

    state = [s_ref[p] for p in range(n_pairs)]
    for ci in range(cps):
        base = ci * n_pairs
        y = [_dot_bf16(qe[base + p], state[p], NT_DIMS) + out1[base + p][:, PAIR:] for p in range(n_pairs)]
        nxt = []
        for p in range(n_pairs):
            mmat = out2[base + p][:PAIR] + jnp.where(eye, gam_ref[ci][:, sls[p]], 0.0)
            nxt.append(_dot_split_lhs(state[p], mmat) + out2[base + p][PAIR:])
        state = nxt
        for p in range(n_pairs):
            sl = sls[p]
            mean = jnp.sum(y[p], axis=-1, keepdims=True) * (1.0 / RW_HEAD_DIM)
            d = jnp.where(own, y[p] - mean, 0.0)
            var = jnp.sum(d * d, axis=-1, keepdims=True) * (1.0 / RW_HEAD_DIM)
            yn = d * lax.rsqrt(var + GN_EPS)
            yn = yn[:CHUNK] + yn[CHUNK:]
            out = ((yn * lng_ref[:, sl] + lnb_ref[:, sl] + bonus_ref[rws[ci], sl].astype(F32))
                   * g_ref[rws[ci], sl].astype(F32))
            o_ref[rws[ci], sl] = out.astype(o_ref.dtype)
    for p in range(n_pairs):
        s_ref[p] = state[p]


def _rwkv_chunks(prep, lnx_g, lnx_b, batch, seq, cps, casts=()):
    at, rt, bt, kt, bg, kg, vb, bonus, g, gam = prep
    t = at.shape[0]
    nc = seq // (CHUNK * cps)
    steps = batch * nc
    gam = gam.reshape(t // CHUNK, 1, RW_WIDTH)
    blk = pl.BlockSpec((cps * CHUNK, RW_WIDTH), lambda b, c: (b * nc + c, 0))
    rowspec = pl.BlockSpec((1, RW_WIDTH), lambda b, c: (0, 0))
    cast_specs = [pl.BlockSpec((w.shape[0] // steps, w.shape[1]), lambda b, c: (b * nc + c, 0)) for w in casts]
    outs = pl.pallas_call(
        functools.partial(_rwkv_chunk_kernel, n_cast=len(casts), cps=cps),
        out_shape=[jax.ShapeDtypeStruct((t, RW_WIDTH), BF16)] + [jax.ShapeDtypeStruct(w.shape, BF16) for w in casts],
        grid=(batch, nc),
        in_specs=([blk] * 9 + [pl.BlockSpec((cps, 1, RW_WIDTH), lambda b, c: (b * nc + c, 0, 0)), rowspec, rowspec]
                  + cast_specs),
        out_specs=[blk] + cast_specs,
        scratch_shapes=[pltpu.VMEM((RW_HEADS // 2, PAIR, PAIR), F32)],
        compiler_params=_cparams("arbitrary", "arbitrary"),
        name="rwkv_chunks",
    )(at, rt, bt, kt, bg, kg, vb, bonus, g, gam, lnx_g.reshape(1, RW_WIDTH), lnx_b.reshape(1, RW_WIDTH), *casts)
    return outs[0], outs[1:]


def _cast_kernel(x_ref, o_ref):
    o_ref[...] = x_ref[...].astype(o_ref.dtype)


def _cast_bf16(w, rows):
    n, m = w.shape
    return pl.pallas_call(
        _cast_kernel,
        out_shape=jax.ShapeDtypeStruct((n, m), BF16),
        grid=(n // rows,),
        in_specs=[pl.BlockSpec((rows, m), lambda i: (i, 0))],
        out_specs=pl.BlockSpec((rows, m), lambda i: (i, 0)),
        compiler_params=_cparams("parallel"),
        name="cast_bf16",
    )(w)


def _merge_kernel(h_ref, ya_ref, yb_ref, wga_ref, wgb_ref, pa_ref, pb_ref, o_ref):
    h = h_ref[...]
    ga = jax.nn.sigmoid(_dot(h, wga_ref[...]))
    gb = jax.nn.sigmoid(_dot(h, wgb_ref[...]))
    o_ref[...] = (ga * _dot(ya_ref[...], pa_ref[...]) + gb * _dot(yb_ref[...], pb_ref[...])).astype(o_ref.dtype)


def _merge(h, ya, yb, wga, wgb, pa, pb, tm, tn):
    t, d = h.shape
    n = wga.shape[1]
    ka, kb = ya.shape[1], yb.shape[1]
    return pl.pallas_call(
        _merge_kernel,
        out_shape=jax.ShapeDtypeStruct((t, n), BF16),
        grid=(t // tm, n // tn),
        in_specs=[pl.BlockSpec((tm, d), lambda i, j: (i, 0)),
                  pl.BlockSpec((tm, ka), lambda i, j: (i, 0)),
                  pl.BlockSpec((tm, kb), lambda i, j: (i, 0)),
                  pl.BlockSpec((d, tn), lambda i, j: (0, j)),
                  pl.BlockSpec((d, tn), lambda i, j: (0, j)),
                  pl.BlockSpec((ka, tn), lambda i, j: (0, j)),
                  pl.BlockSpec((kb, tn), lambda i, j: (0, j))],
        out_specs=pl.BlockSpec((tm, tn), lambda i, j: (i, j)),
        compiler_params=_cparams("parallel", "parallel"),
        name="gated_merge",
    )(h, ya, yb, wga, wgb, pa, pb)


def _first_lane_where(cond, lane):
    return jnp.min(jnp.where(cond, lane, LANES), axis=-1, keepdims=True)


def _outproj_kernel(m_ref, x_ref, wo_ref, g2_ref, wrh_ref, wrl_ref, rb_ref, x1_ref, h2_ref, route_ref):
    x1 = x_ref[...] + _dot(m_ref[...], wo_ref[...])
    x1_ref[...] = x1
    ms = jnp.mean(x1 * x1, axis=-1, keepdims=True)
    h2 = x1 * lax.rsqrt(ms + NORM_EPS) * g2_ref[...]
    h2_ref[...] = h2
    hh = h2.astype(BF16)
    hl = (h2 - hh.astype(F32)).astype(BF16)
    wrh = wrh_ref[...]
    lg = _dot(hh, wrh) + _dot(hl, wrh) + _dot(hh, wrl_ref[...]) + rb_ref[...]

    lane = lax.broadcasted_iota(jnp.int32, lg.shape, 1)
    neg = -jnp.inf
    is_g = lane < N_GROUPS
    mg = jnp.max(jnp.where(is_g, lg, neg), axis=-1, keepdims=True)
    eg = jnp.where(is_g, jnp.exp(lg - mg), 0.0)
    pg = eg / jnp.sum(eg, axis=-1, keepdims=True)
    p_g_top = jnp.max(pg, axis=-1, keepdims=True)
    g_idx = _first_lane_where(is_g & (pg == p_g_top), lane)
    lo = N_GROUPS + g_idx * EXPERTS_PER_GROUP
    sel = (lane >= lo) & (lane < lo + EXPERTS_PER_GROUP)
    me = jnp.max(jnp.where(sel, lg, neg), axis=-1, keepdims=True)
    ee = jnp.where(sel, jnp.exp(lg - me), 0.0)
    pe = ee / jnp.sum(ee, axis=-1, keepdims=True)
    pe = jnp.where(sel, pe, -1.0)
    v1 = jnp.max(pe, axis=-1, keepdims=True)
    i1 = _first_lane_where(pe == v1, lane)
    pe2 = jnp.where(lane == i1, -1.0, pe)
    v2 = jnp.max(pe2, axis=-1, keepdims=True)
    i2 = _first_lane_where(pe2 == v2, lane)
    den = v1 + v2
    route = jnp.where(lane == 0, p_g_top * v1 / den,
                      jnp.where(lane == 1, p_g_top * v2 / den,
                                jnp.where(lane == 2, (i1 - N_GROUPS).astype(F32),
                                          jnp.where(lane == 3, (i2 - N_GROUPS).astype(F32), 0.0))))
    route_ref[...] = route


def _outproj(merged, x, wo, g2, wr_hi, wr_lo, rbias, tm):
    t, d = x.shape
    nr = wr_hi.shape[1]
    return pl.pallas_call(
        _outproj_kernel,
        out_shape=[jax.ShapeDtypeStruct((t, d), F32), jax.ShapeDtypeStruct((t, d), F32),
                   jax.ShapeDtypeStruct((t, nr), F32)],
        grid=(t // tm,),
        in_specs=[pl.BlockSpec((tm, d), lambda i: (i, 0)),
                  pl.BlockSpec((tm, d), lambda i: (i, 0)),
                  pl.BlockSpec((d, d), lambda i: (0, 0)),
                  pl.BlockSpec((1, d), lambda i: (0, 0)),
                  pl.BlockSpec((d, nr), lambda i: (0, 0)),
                  pl.BlockSpec((d, nr), lambda i: (0, 0)),
                  pl.BlockSpec((1, nr), lambda i: (0, 0))],
        out_specs=[pl.BlockSpec((tm, d), lambda i: (i, 0)),
                   pl.BlockSpec((tm, d), lambda i: (i, 0)),
                   pl.BlockSpec((tm, nr), lambda i: (i, 0))],
        compiler_params=_cparams("parallel"),
        name="outproj_norm_router",
    )(merged, x, wo, g2.reshape(1, d), wr_hi, wr_lo, rbias)


IDX_SLOTS = 3


def _moe_kernel(be_ref, nused_ref, nvalid_ref, tok_hbm, dst_hbm, h_hbm, wg_ref, wu_ref, wd_ref, y_hbm,
                xbuf, obuf, tok_s, dst_s, idx_sem, g_sem, s_sem, *, rows):
    b = pl.program_id(0)
    n_used = nused_ref[0]
    last = n_used - 1

    def idx_copies(blk, sl):
        return (pltpu.make_async_copy(tok_hbm.at[blk], tok_s.at[sl], idx_sem.at[sl, 0]),
                pltpu.make_async_copy(dst_hbm.at[blk], dst_s.at[sl], idx_sem.at[sl, 1]))

    def gather_row(r, isl, xsl):
        return pltpu.make_async_copy(h_hbm.at[tok_s[isl, r]], xbuf.at[xsl, r], g_sem.at[xsl])

    def gather_all(xsl):
        return pltpu.make_async_copy(h_hbm.at[pl.ds(0, rows)], xbuf.at[xsl], g_sem.at[xsl])

    def scatter_row(r, isl, osl):
        return pltpu.make_async_copy(obuf.at[osl, r], y_hbm.at[dst_s[isl, r]], s_sem.at[osl])

    def wait_scatter(blk, osl):
        n = nvalid_ref[blk]

        @pl.when(n == rows)
        def _():
            pltpu.make_async_copy(obuf.at[osl], y_hbm.at[pl.ds(0, rows)], s_sem.at[osl]).wait()

        @pl.when(n < rows)
        def _():
            def one(r, c):
                pltpu.make_async_copy(obuf.at[osl, 0], y_hbm.at[0], s_sem.at[osl]).wait()
                return c

            lax.fori_loop(0, n, one, 0)

    @pl.when(b == 0)
    def _():
        for cp in idx_copies(0, 0):
            cp.start()
        for cp in idx_copies(0, 0):
            cp.wait()
        for cp in idx_copies(jnp.minimum(1, last), 1):
            cp.start()

        def issue(r, c):
            gather_row(r, 0, 0).start()
            return c

        lax.fori_loop(0, rows, issue, 0, unroll=8)

    @pl.when((b > 1) & (b < n_used))
    def _():
        wait_scatter(b - 2, b % 2)

    for par in range(2):
        @pl.when((b < n_used) & (b % 2 == par))
        def _(par=par):
            nxt_i, ld_i = (b + 1) % IDX_SLOTS, (b + 2) % IDX_SLOTS
            for cp in idx_copies(0, nxt_i):
                cp.wait()
            for r in range(rows):
                gather_row(r, nxt_i, 1 - par).start()
            for cp in idx_copies(jnp.minimum(b + 2, last), ld_i):
                cp.start()

            gather_all(par).wait()
            x = xbuf[par].astype(BF16)
            gate = _dot(x, wg_ref[...])
            up = _dot(x, wu_ref[...])
            mid = (gate * jax.nn.sigmoid(gate) * up).astype(BF16)
            obuf[par] = _dot(mid, wd_ref[...])

        @pl.when((b < n_used) & (b % 2 == par) & (nvalid_ref[b] == rows))
        def _(par=par):
            for r in range(rows):
                scatter_row(r, b % IDX_SLOTS, par).start()

    @pl.when((b < n_used) & (nvalid_ref[b] < rows))
    def _():
        def issue(r, c):
            scatter_row(r, b % IDX_SLOTS, b % 2).start()
            return c

        lax.fori_loop(0, nvalid_ref[b], issue, 0)

    @pl.when(b == last)
    def _():
        @pl.when(b > 0)
        def _():
            wait_scatter(b - 1, (b - 1) % 2)

        wait_scatter(b, b % 2)
        gather_all((b + 1) % 2).wait()
        for cp in idx_copies(0, (b + 2) % IDX_SLOTS):
            cp.wait()


def _moe(h2, row_tok, row_dst, block_e, n_used, nvalid, wg, wu, wd, rows):
    t, d = h2.shape
    n_blocks = row_tok.shape[0]
    a = t * TOP_K
    f = wg.shape[2]
    kern = functools.partial(_moe_kernel, rows=rows)
    grid_spec = pltpu.PrefetchScalarGridSpec(
        num_scalar_prefetch=3,
        grid=(n_blocks,),
        in_specs=[pl.BlockSpec(memory_space=pl.ANY),
                  pl.BlockSpec(memory_space=pl.ANY),
                  pl.BlockSpec(memory_space=pl.ANY),
                  pl.BlockSpec((None, d, f), lambda b, be, nu, nv: (be[b], 0, 0)),
                  pl.BlockSpec((None, d, f), lambda b, be, nu, nv: (be[b], 0, 0)),
                  pl.BlockSpec((None, f, d), lambda b, be, nu, nv: (be[b], 0, 0))],
        out_specs=pl.BlockSpec(memory_space=pl.ANY),
        scratch_shapes=[pltpu.VMEM((2, rows, d), F32),
                        pltpu.VMEM((2, rows, d), F32),
                        pltpu.SMEM((IDX_SLOTS, rows), jnp.int32),
                        pltpu.SMEM((IDX_SLOTS, rows), jnp.int32),
                        pltpu.SemaphoreType.DMA((IDX_SLOTS, 2)),
                        pltpu.SemaphoreType.DMA((2,)),
                        pltpu.SemaphoreType.DMA((2,))],
    )
    return pl.pallas_call(
        kern,
        out_shape=jax.ShapeDtypeStruct((a, d), F32),
        grid_spec=grid_spec,
        compiler_params=_cparams("arbitrary"),
        name="moe_experts",
    )(block_e, n_used, nvalid, row_tok, row_dst, h2, wg, wu, wd)


def _combine_kernel(x1_ref, y0_ref, y1_ref, route_ref, o_ref):
    gts = route_ref[...]
    o_ref[...] = x1_ref[...] + gts[:, 0:1] * y0_ref[...] + gts[:, 1:2] * y1_ref[...]


def _combine(x1, y, route, tm):
    t, d = x1.shape
    nb = t // tm
    return pl.pallas_call(
        _combine_kernel,
        out_shape=jax.ShapeDtypeStruct((t, d), F32),
        grid=(nb,),
        in_specs=[pl.BlockSpec((tm, d), lambda i: (i, 0)),
                  pl.BlockSpec((tm, d), lambda i: (i, 0)),
                  pl.BlockSpec((tm, d), lambda i: (nb + i, 0)),
                  pl.BlockSpec((tm, LANES), lambda i: (i, 0))],
        out_specs=pl.BlockSpec((tm, d), lambda i: (i, 0)),
        compiler_params=_cparams("parallel"),
        name="moe_combine",
    )(x1, y, y, route)


def _routing_tables(route, rows):
    t = route.shape[0]
    a = t * TOP_K
    expert = route[:, 2:2 + TOP_K].astype(jnp.int32)
    flat_e = expert.reshape(a)
    ids = jnp.arange(a, dtype=jnp.int32)
    order = jnp.sort(flat_e * a + ids) % a
    counts = jnp.sum((flat_e[:, None] == jnp.arange(N_EXPERTS, dtype=jnp.int32)[None, :]).astype(jnp.int32), axis=0)
    padded = (counts + rows - 1) // rows * rows
    start = jnp.cumsum(counts) - counts
    pend = jnp.cumsum(padded)
    pstart = pend - padded
    n_rows = a + N_EXPERTS * rows
    n_blocks = n_rows // rows
    blk_row0 = jnp.arange(n_blocks, dtype=jnp.int32) * rows
    block_e = jnp.minimum(jnp.sum((pend[None, :] <= blk_row0[:, None]).astype(jnp.int32), axis=1), N_EXPERTS - 1)
    off0 = blk_row0 - pstart[block_e]
    cnt_b = counts[block_e]
    off = off0[:, None] + jnp.arange(rows, dtype=jnp.int32)[None, :]
    valid = (off >= 0) & (off < cnt_b[:, None])
    src = order[jnp.clip(start[block_e][:, None] + off, 0, a - 1)]
    row_tok = jnp.where(valid, src // TOP_K, 0)
    row_dst = jnp.where(valid, (src % TOP_K) * t + src // TOP_K, 0)
    n_used = (pend[-1] // rows).astype(jnp.int32).reshape(1)
    nvalid = jnp.clip(cnt_b - off0, 0, rows).astype(jnp.int32)
    return row_tok, row_dst, block_e.astype(jnp.int32), n_used, nvalid


def _tiles(t, seq):
    return dict(
        norm_tm=min(512, t),
        proj_tm=min(1024, seq), proj_tn=512,
        attn_tq=min(256, seq), attn_tk=min(1024, seq), attn_heads=4,
        prep_tm=min(256, seq), prep_tn=1024, rwkv_cps=2,
        merge_tm=min(1024, t), merge_tn=512,
        out_tm=min(512, t),
        moe_rows=256,
        comb_tm=min(512, t),
    )


def _pad_rows(w, n):
    return jnp.pad(w, ((0, n - w.shape[0]), (0, 0)))


def _pad_cols(w, n):
    return jnp.pad(w, ((0, 0), (0, n - w.shape[1])))


def kernel(x, norm1_g, w_in, q_norm_g, k_norm_g, lam_q1, lam_k1, lam_q2, lam_k2, subln_g, shift_mu, w0, w_up, a0, a_up, g_up, k_k, k_a, r_k, lnx_g, lnx_b, proj_a, proj_b, w_out, norm2_g, router_g, router_g_b, router_e, router_e_b, w_gate_e, w_up_e, w_down_e):
    batch, seq, d = x.shape
    t = batch * seq
    depth = norm1_g.shape[0]
    tl = _tiles(t, seq)
    qkw = DA_HEADS * 2 * DA_HEAD_DIM
    vw = DA_HEADS * DA_V_DIM
    c_q, c_k, c_v = 0, qkw, 2 * qkw
    c_rw = c_v + vw
    c_dw = c_rw + 3 * RW_WIDTH
    c_da = c_dw + DECAY_LORA
    c_dg = c_da + AAA_LORA
    c_ga = c_dg + GATE_LORA
    c_gb = c_ga + d
    cos_t, sinm_t, sinp_t = _rope_tables(seq)
    xf = x.reshape(t, d)

    for l in range(depth):
        lam_init = 0.8 - 0.6 * math.exp(-0.3 * l)
        wl = w_in[l]
        w_qk = wl[:, c_q:c_v].astype(BF16)
        w_v = wl[:, c_v:c_rw].astype(BF16)
        w_rw = jnp.concatenate([wl[:, c_rw:c_dw], _pad_cols(wl[:, c_dw:c_da], LANES),
                                _pad_cols(wl[:, c_da:c_dg], LANES), wl[:, c_dg:c_ga]], axis=1).astype(BF16)
        w_ga = wl[:, c_ga:c_gb].astype(BF16)
        w_gb = wl[:, c_gb:].astype(BF16)
        mu = shift_mu[l]
        o_dw = 3 * RW_WIDTH
        mu_l = jnp.concatenate([mu[:o_dw], jnp.pad(mu[o_dw:o_dw + DECAY_LORA], (0, LANES - DECAY_LORA)),
                                jnp.pad(mu[o_dw + DECAY_LORA:o_dw + DECAY_LORA + AAA_LORA], (0, LANES - AAA_LORA)),
                                mu[o_dw + DECAY_LORA + AAA_LORA:]]).reshape(1, -1)
        gain_row = jnp.concatenate([jnp.tile(q_norm_g[l], 2 * DA_HEADS) * (DA_HEAD_DIM ** -0.5 * LOG2_E),
                                    jnp.tile(k_norm_g[l], 2 * DA_HEADS)]).reshape(1, 2 * qkw)
        lam_params = jnp.stack([lam_q1[l], lam_k1[l], lam_q2[l], lam_k2[l]])
        zrow = jnp.zeros((RW_WIDTH,), F32)
        rw_params = jnp.stack([w0[l], a0[l], k_k[l], k_a[l], r_k[l].reshape(-1), zrow, zrow, zrow])

        h = _rmsnorm(xf, norm1_g[l], NORM_EPS, tl["norm_tm"])
        qk = _qk_proj(h, w_qk, gain_row, cos_t, sinm_t, sinp_t, seq, tl["proj_tm"], tl["proj_tn"])
        v = _matmul(h, w_v, BF16, tl["proj_tm"], tl["proj_tn"], "v_proj")
        ya = _diff_attention(qk, v, lam_params, subln_g[l], batch, seq, lam_init, tl["attn_tq"], tl["attn_tk"],
                             tl["attn_heads"])

        rw = _matmul(h, w_rw, F32, tl["proj_tm"], tl["proj_tn"], "rw_proj")
        prep = _rwkv_prep(rw, mu_l, rw_params, _pad_rows(w_up[l], LANES).astype(BF16),
                          _pad_rows(a_up[l], LANES).astype(BF16), g_up[l].astype(BF16),
                          seq, tl["prep_tm"], tl["prep_tn"])
        ew = (w_gate_e[l], w_up_e[l], w_down_e[l])
        ew2d = tuple(w.reshape(-1, w.shape[-1]) for w in ew)
        cps = tl["rwkv_cps"]
        steps = batch * (seq // (CHUNK * cps))
        ride = all(w.shape[0] % (steps * 2 * SUBLANES) == 0 and w.size * 4 // steps <= CAST_BLOCK_BYTES for w in ew2d)
        yb, ew_bf16 = _rwkv_chunks(prep, lnx_g[l], lnx_b[l], batch, seq, cps, ew2d if ride else ())
        if not ride:
            ew_bf16 = tuple(_cast_bf16(w, CAST_BLOCK_BYTES // (4 * w.shape[1])) for w in ew2d)
        wg_e, wu_e, wd_e = (c.reshape(w.shape) for c, w in zip(ew_bf16, ew))

        merged = _merge(h, ya, yb, w_ga, w_gb, proj_a[l].astype(BF16), proj_b[l].astype(BF16),
                        tl["merge_tm"], tl["merge_tn"])
        wr = _pad_cols(jnp.concatenate([router_g[l], router_e[l]], axis=1), LANES)
        wr_hi = wr.astype(BF16)
        wr_lo = (wr - wr_hi.astype(F32)).astype(BF16)
        rbias = jnp.pad(jnp.concatenate([router_g_b[l], router_e_b[l]]), (0, LANES - N_GROUPS - N_EXPERTS))
        x1, h2, route = _outproj(merged, xf, w_out[l].astype(BF16), norm2_g[l], wr_hi, wr_lo,
                                 rbias.reshape(1, LANES), tl["out_tm"])

        rows = tl["moe_rows"]
        row_tok, row_dst, block_e, n_used, nvalid = _routing_tables(route, rows)
        y = _moe(h2, row_tok, row_dst, block_e, n_used, nvalid, wg_e, wu_e, wd_e, rows)
        xf = _combine(x1, y, route, tl["comb_tm"])
    return xf.reshape(batch, seq, d)
```

```python
import functools
import math

import jax
import jax.numpy as jnp
from jax import lax
from jax.experimental import pallas as pl
from jax.experimental.pallas import tpu as pltpu

DA_HEADS = 8
DA_HEAD_DIM = 64
DA_V_DIM = 2 * DA_HEAD_DIM
ROT_DIM = DA_HEAD_DIM // 4
ROPE_THETA = 500000.0
SUBLN_EPS = 1e-5
RW_HEADS = 16
RW_HEAD_DIM = 64
RW_WIDTH = RW_HEADS * RW_HEAD_DIM
DECAY_LORA = 96
AAA_LORA = 96
GATE_LORA = 256
GN_EPS = 64e-5
N_GROUPS = 4
EXPERTS_PER_GROUP = 8
N_EXPERTS = N_GROUPS * EXPERTS_PER_GROUP
TOP_K = 2
NORM_EPS = 1e-6

LANES = 128
SUBLANES = 8
VMEM_LIMIT_BYTES = 56 * 1024 * 1024

CHUNK = 64
CAST_BLOCK_BYTES = 2 * 1024 * 1024

F32 = jnp.float32
BF16 = jnp.bfloat16

LOG2_E = math.log2(math.e)

NT_DIMS = (((1,), (1,)), ((), ()))
TN_DIMS = (((0,), (0,)), ((), ()))


def _cparams(*sem):
    return pltpu.CompilerParams(dimension_semantics=tuple(sem), vmem_limit_bytes=VMEM_LIMIT_BYTES)


def _dot(a, b, dims=None, precision=None):
    if dims is None:
        return jnp.dot(a, b, preferred_element_type=F32, precision=precision)
    return lax.dot_general(a, b, dims, preferred_element_type=F32, precision=precision)


def _split3(x):
    h = x.astype(BF16)
    r = x - h.astype(F32)
    m = r.astype(BF16)
    l = (r - m.astype(F32)).astype(BF16)
    return h, m, l


def _group_sum64(x, bd, terms):
    parts = _split3(x)[:terms]
    outs = []
    for s in range(x.shape[1] // LANES):
        sl = slice(s * LANES, (s + 1) * LANES)
        acc = _dot(parts[0][:, sl], bd)
        for part in parts[1:]:
            acc = acc + _dot(part[:, sl], bd)
        outs.append(acc)
    return outs[0] if len(outs) == 1 else jnp.concatenate(outs, axis=1)


def _block_diag_ones(n, blk, dtype=BF16):
    r = lax.broadcasted_iota(jnp.int32, (n, n), 0) // blk
    c = lax.broadcasted_iota(jnp.int32, (n, n), 1) // blk
    return jnp.where(r == c, 1.0, 0.0).astype(dtype)


def _rmsnorm_kernel(x_ref, g_ref, o_ref, *, eps):
    x = x_ref[...]
    ms = jnp.mean(x * x, axis=-1, keepdims=True)
    o_ref[...] = (x * lax.rsqrt(ms + eps) * g_ref[...]).astype(o_ref.dtype)


def _rmsnorm(x, g, eps, tm):
    t, d = x.shape
    return pl.pallas_call(
        functools.partial(_rmsnorm_kernel, eps=eps),
        out_shape=jax.ShapeDtypeStruct((t, d), BF16),
        grid=(t // tm,),
        in_specs=[pl.BlockSpec((tm, d), lambda i: (i, 0)),
                  pl.BlockSpec((1, d), lambda i: (0, 0))],
        out_specs=pl.BlockSpec((tm, d), lambda i: (i, 0)),
        compiler_params=_cparams("parallel"),
        name="rmsnorm",
    )(x, g.reshape(1, d))


def _matmul_kernel(a_ref, w_ref, o_ref):
    o_ref[...] = _dot(a_ref[...], w_ref[...]).astype(o_ref.dtype)


def _matmul(a, w, out_dtype, tm, tn, name):
    t, k = a.shape
    n = w.shape[1]
    return pl.pallas_call(
        _matmul_kernel,
        out_shape=jax.ShapeDtypeStruct((t, n), out_dtype),
        grid=(t // tm, n // tn),
        in_specs=[pl.BlockSpec((tm, k), lambda i, j: (i, 0)),
                  pl.BlockSpec((k, tn), lambda i, j: (0, j))],
        out_specs=pl.BlockSpec((tm, tn), lambda i, j: (i, j)),
        compiler_params=_cparams("parallel", "parallel"),
        name=name,
    )(a, w)


def _qk_proj_kernel(a_ref, w_ref, gain_ref, cos_ref, sinm_ref, sinp_ref, o_ref, *, tn):
    acc = _dot(a_ref[...], w_ref[...])
    bd = _block_diag_ones(LANES, DA_HEAD_DIM)
    ms = _group_sum64(acc * acc, bd, 1) * (1.0 / DA_HEAD_DIM)
    xn = acc * lax.rsqrt(ms + NORM_EPS) * gain_ref[...]
    reps = tn // LANES
    cos = jnp.tile(cos_ref[...], (1, reps))
    sinm = jnp.tile(sinm_ref[...], (1, reps))
    sinp = jnp.tile(sinp_ref[...], (1, reps))
    half = ROT_DIM // 2
    hi = pltpu.roll(xn, tn - half, 1)
    lo = pltpu.roll(xn, half, 1)
    o_ref[...] = (xn * cos + hi * sinm + lo * sinp).astype(o_ref.dtype)


def _qk_proj(h, w_qk, gain_row, cos_t, sinm_t, sinp_t, seq, tm, tn):
    t, k = h.shape
    n = w_qk.shape[1]
    nseq = seq // tm
    return pl.pallas_call(
        functools.partial(_qk_proj_kernel, tn=tn),
        out_shape=jax.ShapeDtypeStruct((t, n), BF16),
        grid=(t // tm, n // tn),
        in_specs=[pl.BlockSpec((tm, k), lambda i, j: (i, 0)),
                  pl.BlockSpec((k, tn), lambda i, j: (0, j)),
                  pl.BlockSpec((1, tn), lambda i, j: (0, j)),
                  pl.BlockSpec((tm, LANES), lambda i, j: (i % nseq, 0)),
                  pl.BlockSpec((tm, LANES), lambda i, j: (i % nseq, 0)),
                  pl.BlockSpec((tm, LANES), lambda i, j: (i % nseq, 0))],
        out_specs=pl.BlockSpec((tm, tn), lambda i, j: (i, j)),
        compiler_params=_cparams("parallel", "parallel"),
        name="qk_proj",
    )(h, w_qk, gain_row, cos_t, sinm_t, sinp_t)


def _rope_tables(seq):
    half = ROT_DIM // 2
    inv_freq = ROPE_THETA ** (-jnp.arange(0, ROT_DIM, 2, dtype=F32) / ROT_DIM)
    ang = jnp.arange(seq, dtype=F32)[:, None] * inv_freq[None, :]
    cos, sin = jnp.cos(ang), jnp.sin(ang)
    ones = jnp.ones((seq, DA_HEAD_DIM - ROT_DIM), F32)
    zeros = jnp.zeros((seq, DA_HEAD_DIM - ROT_DIM), F32)
    zh = jnp.zeros((seq, half), F32)
    cos64 = jnp.concatenate([cos, cos, ones], axis=1)
    sinm64 = jnp.concatenate([-sin, zh, zeros], axis=1)
    sinp64 = jnp.concatenate([zh, sin, zeros], axis=1)
    return tuple(jnp.concatenate([a, a], axis=1) for a in (cos64, sinm64, sinp64))


def _diff_attn_kernel(lam_ref, q_ref, k_ref, v_ref, g_ref, o_ref, m_ref, l_ref, a_ref, *, tq, tk, lam_init, nh):
    i = pl.program_id(2)
    lane = lax.broadcasted_iota(jnp.int32, (1, DA_V_DIM), 1)
    qs = []
    for h in range(nh):
        q = q_ref[:, h * DA_V_DIM:(h + 1) * DA_V_DIM]
        zero = jnp.zeros_like(q)
        qs.append(jnp.where(lane < DA_HEAD_DIM, q, zero))
        qs.append(jnp.where(lane >= DA_HEAD_DIM, q, zero))
    nc = 2 * nh
    m_ref[...] = jnp.full(m_ref.shape, -jnp.inf, F32)
    l_ref[...] = jnp.zeros(l_ref.shape, F32)
    a_ref[...] = jnp.zeros(a_ref.shape, F32)

    def step(j, mask):
        off = pl.multiple_of(j * tk, tk)
        ks = [k_ref[pl.ds(off, tk), h * DA_V_DIM:(h + 1) * DA_V_DIM] for h in range(nh)]
        vs = [v_ref[pl.ds(off, tk), h * DA_V_DIM:(h + 1) * DA_V_DIM] for h in range(nh)]
        s = [_dot(qs[c], ks[c // 2], NT_DIMS) for c in range(nc)]
        for c in range(nc):
            sc = s[c] if mask is None else jnp.where(mask, s[c], -jnp.inf)
            m_old = m_ref[c]
            m_new = jnp.maximum(m_old, jnp.max(sc, axis=-1, keepdims=True))
            alpha = jnp.exp2(m_old - m_new)
            p = jnp.exp2(sc - m_new)
            part = p[:, :LANES]
            for kk in range(1, tk // LANES):
                part = part + p[:, kk * LANES:(kk + 1) * LANES]
            l_ref[c] = alpha * l_ref[c] + part
            a_ref[c] = alpha * a_ref[c] + _dot(p.astype(BF16), vs[c // 2])
            m_ref[c] = m_new

    n_full = (i * tq) // tk

    def full_body(j, c):
        step(j, None)
        return c

    lax.fori_loop(0, n_full, full_body, 0)

    row = i * tq + lax.broadcasted_iota(jnp.int32, (tq, tk), 0)
    for d in range(max(1, tq // tk)):
        j = n_full + d
        col = j * tk + lax.broadcasted_iota(jnp.int32, (tq, tk), 1)
        step(j, col <= row)

    lq1, lk1, lq2, lk2 = (lam_ref[r:r + 1, :] for r in range(4))
    lam = (jnp.exp(jnp.sum(lq1 * lk1, axis=-1, keepdims=True))
           - jnp.exp(jnp.sum(lq2 * lk2, axis=-1, keepdims=True)) + lam_init)
    for h in range(nh):
        l1 = jnp.sum(l_ref[2 * h], axis=-1, keepdims=True)
        l2 = jnp.sum(l_ref[2 * h + 1], axis=-1, keepdims=True)
        o = a_ref[2 * h] / l1 - lam * (a_ref[2 * h + 1] / l2)
        ms = jnp.mean(o * o, axis=-1, keepdims=True)
        o = o * lax.rsqrt(ms + SUBLN_EPS) * (g_ref[...] * (1.0 - lam_init))
        o_ref[:, h * DA_V_DIM:(h + 1) * DA_V_DIM] = o.astype(o_ref.dtype)


def _diff_attention(qk, v, lam_params, subln_g, batch, seq, lam_init, tq, tk, nh):
    t = qk.shape[0]
    nq = seq // tq
    w = nh * DA_V_DIM
    hg = DA_HEADS // nh
    kern = functools.partial(_diff_attn_kernel, tq=tq, tk=tk, lam_init=lam_init, nh=nh)
    return pl.pallas_call(
        kern,
        out_shape=jax.ShapeDtypeStruct((t, DA_HEADS * DA_V_DIM), BF16),
        grid=(batch, hg, nq),
        in_specs=[pl.BlockSpec((4, DA_HEAD_DIM), lambda b, h, i: (0, 0)),
                  pl.BlockSpec((tq, w), lambda b, h, i: (b * nq + i, h)),
                  pl.BlockSpec((seq, w), lambda b, h, i: (b, hg + h)),
                  pl.BlockSpec((seq, w), lambda b, h, i: (b, h)),
                  pl.BlockSpec((1, DA_V_DIM), lambda b, h, i: (0, 0))],
        out_specs=pl.BlockSpec((tq, w), lambda b, h, i: (b * nq + i, h)),
        scratch_shapes=[pltpu.VMEM((2 * nh, tq, 1), F32), pltpu.VMEM((2 * nh, tq, LANES), F32),
                        pltpu.VMEM((2 * nh, tq, DA_V_DIM), F32)],
        compiler_params=_cparams("parallel", "parallel", "parallel"),
        name="diff_attention",
    )(lam_params, qk, qk, v, subln_g.reshape(1, DA_V_DIM))


DECAY_SCALE = math.exp(-0.5)


def _rwkv_prep_kernel(h_ref, w_ref, wlo_ref, mu_ref, mulo_ref, par_ref, wup_ref, aup_ref, gup_ref,
                      at_ref, rt_ref, bt_ref, kt_ref, bg_ref, kg_ref, vb_ref, bonus_ref, g_ref, gam_ref,
                      last_ref, lastlo_ref, *, tm, nseq):
    i = pl.program_id(0)

    @pl.when(i == 0)
    def _():
        last_ref[...] = jnp.zeros(last_ref.shape, F32)
        lastlo_ref[...] = jnp.zeros(lastlo_ref.shape, F32)

    seq_start = (i % nseq) == 0
    row0 = lax.broadcasted_iota(jnp.int32, (tm, 1), 0) == 0
    h = h_ref[...]

    def shifted_proj(w_r, mu_r, carry_ref):
        x = _dot(h, w_r[...])
        last = jnp.where(seq_start, 0.0, carry_ref[...])
        prev = jnp.where(row0, last, pltpu.roll(x, 1, 0))
        carry_ref[...] = x[tm - 1:tm, :]
        return x + (prev - x) * mu_r[...]

    z = shifted_proj(w_ref, mu_ref, last_ref)
    lo = shifted_proj(wlo_ref, mulo_ref, lastlo_ref)
    r, k, v = z[:, :RW_WIDTH], z[:, RW_WIDTH:2 * RW_WIDTH], z[:, 2 * RW_WIDTH:]
    dw, da, dg = lo[:, :LANES], lo[:, LANES:2 * LANES], lo[:, 2 * LANES:]
    w0, a0, k_k, k_a, r_k = (par_ref[j:j + 1, :] for j in range(5))

    u = w0 + _dot(jnp.tanh(dw).astype(BF16), wup_ref[...])
    ld = -DECAY_SCALE * jax.nn.sigmoid(u)
    a = jax.nn.sigmoid(a0 + _dot(da.astype(BF16), aup_ref[...]))
    g = _dot(jax.nn.sigmoid(dg).astype(BF16), gup_ref[...])

    bd = _block_diag_ones(LANES, RW_HEAD_DIM)
    kk = k * k_k
    kk = kk / jnp.maximum(jnp.sqrt(_group_sum64(kk * kk, bd, 1)), 1e-12)
    k2 = k * (1.0 + (a - 1.0) * k_a)
    bonus = _group_sum64(r * k2 * r_k, bd, 1) * v

    t_i = lax.broadcasted_iota(jnp.int32, (tm, tm), 0)
    s_i = lax.broadcasted_iota(jnp.int32, (tm, tm), 1)
    same = (t_i // CHUNK) == (s_i // CHUNK)
    tri = jnp.where(same & (s_i <= t_i), 1.0, 0.0).astype(BF16)
    rest = jnp.where(same & (s_i > t_i), 1.0, 0.0).astype(BF16)
    c_i = lax.broadcasted_iota(jnp.int32, (tm // CHUNK, tm), 0)
    cs_i = lax.broadcasted_iota(jnp.int32, (tm // CHUNK, tm), 1)
    whole = jnp.where(cs_i // CHUNK == c_i, 1.0, 0.0).astype(BF16)

    ldh, ldm = _split2(ld)

    def sel(m):
        return _dot(m, ldh) + _dot(m, ldm)

    cum = sel(tri)
    e_neg = jnp.exp(-cum)
    e_rem = jnp.exp(sel(rest))
    b = kk * a
    at_ref[...] = (-kk * jnp.exp(cum - ld)).astype(BF16)
    rt_ref[...] = (r * jnp.exp(cum)).astype(BF16)
    bt_ref[...] = (b * e_neg).astype(BF16)
    kt_ref[...] = (k2 * e_neg).astype(BF16)
    bg_ref[...] = (b * e_rem).astype(BF16)
    kg_ref[...] = (k2 * e_rem).astype(BF16)
    vb_ref[...] = v.astype(BF16)
    bonus_ref[...] = bonus.astype(BF16)
    g_ref[...] = g.astype(BF16)
    gam_ref[0] = jnp.exp(sel(whole))


def _rwkv_prep(h, w_rkv, w_lo, mu_rkv, mu_lo, params, wup, aup, gup, seq, tm):
    t, d = h.shape
    n, nlo = w_rkv.shape[1], w_lo.shape[1]
    nseq = seq // tm
    whole = lambda shape: pl.BlockSpec(shape, lambda i: (0, 0), pipeline_mode=pl.Buffered(1))
    in_specs = [pl.BlockSpec((tm, d), lambda i: (i, 0)), whole((d, n)), whole((d, nlo)), whole((1, n)), whole((1, nlo)),
                whole((SUBLANES, RW_WIDTH)), whole((LANES, RW_WIDTH)), whole((LANES, RW_WIDTH)),
                whole((2 * LANES, RW_WIDTH))]
    out_blk = pl.BlockSpec((tm, RW_WIDTH), lambda i: (i, 0))
    outs = [jax.ShapeDtypeStruct((t, RW_WIDTH), BF16)] * 9
    outs.append(jax.ShapeDtypeStruct((t // tm, tm // CHUNK, RW_WIDTH), F32))
    out_specs = [out_blk] * 9 + [pl.BlockSpec((1, tm // CHUNK, RW_WIDTH), lambda i: (i, 0, 0))]
    return pl.pallas_call(
        functools.partial(_rwkv_prep_kernel, tm=tm, nseq=nseq),
        out_shape=outs,
        grid=(t // tm,),
        in_specs=in_specs,
        out_specs=out_specs,
        scratch_shapes=[pltpu.VMEM((1, n), F32), pltpu.VMEM((1, nlo), F32)],
        compiler_params=_cparams("arbitrary"),
        name="rwkv_prep",
    )(h, w_rkv, w_lo, mu_rkv, mu_lo, params, wup, aup, gup)


PAIR = 2 * RW_HEAD_DIM


def _dot_bf16(a, b, dims=None):
    return _dot(a.astype(BF16), b.astype(BF16), dims)


def _split2(x):
    h = x.astype(BF16)
    return h, (x - h.astype(F32)).astype(BF16)


def _dot_split_lhs(a, b, dims=None):
    ah, al = _split2(a)
    bh = b.astype(BF16)
    return _dot(ah, bh, dims) + _dot(al, bh, dims)


def _rwkv_chunk_kernel(at_ref, rt_ref, bt_ref, kt_ref, bg_ref, kg_ref, v_ref, bonus_ref, g_ref,
                       gam_ref, lng_ref, lnb_ref, *rest, n_cast, cps):
    cast_in, o_ref, cast_out, s_ref = rest[:n_cast], rest[n_cast], rest[n_cast + 1:2 * n_cast + 1], rest[-1]
    c = pl.program_id(1)

    for src, dst in zip(cast_in, cast_out):
        dst[...] = src[...].astype(dst.dtype)

    @pl.when(c == 0)
    def _():
        s_ref[...] = jnp.zeros(s_ref.shape, F32)

    lane = lax.broadcasted_iota(jnp.int32, (1, PAIR), 1)
    first = lane < RW_HEAD_DIM
    rho = lax.broadcasted_iota(jnp.int32, (PAIR, PAIR), 0)
    sig = lax.broadcasted_iota(jnp.int32, (PAIR, PAIR), 1)
    strict, incl, eye = sig < rho, sig <= rho, sig == rho
    own = (rho // RW_HEAD_DIM) == (sig // RW_HEAD_DIM)

    def stacked(x):
        z = jnp.zeros_like(x)
        return jnp.concatenate([jnp.where(first, x, z), jnp.where(first, z, x)], axis=0)

    n_pairs = RW_HEADS // 2
    units = [(ci, p) for ci in range(cps) for p in range(n_pairs)]
    rws = [slice(ci * CHUNK, (ci + 1) * CHUNK) for ci in range(cps)]
    sls = [slice(p * PAIR, (p + 1) * PAIR) for p in range(n_pairs)]
    zero = jnp.zeros((PAIR, PAIR), F32)
    st = [[stacked(ref[rws[ci], sls[p]]) for ref in (at_ref, rt_ref, bt_ref, kt_ref, bg_ref, kg_ref, v_ref)]
          for ci, p in units]
    un = range(len(units))
    prods = [_dot(jnp.concatenate([q[0], q[1]], axis=0), jnp.concatenate([q[2], q[3]], axis=0), NT_DIMS) for q in st]
    lmat = [jnp.where(strict, pr_[:PAIR, :PAIR], zero) for pr_ in prods]
    sak = [jnp.where(strict, pr_[:PAIR, PAIR:], zero) for pr_ in prods]
    lrbk = [jnp.concatenate([jnp.where(incl, pr_[PAIR:, :PAIR], zero), jnp.where(incl, pr_[PAIR:, PAIR:], zero)],
                            axis=1).astype(BF16) for pr_ in prods]
    akv = [_dot(sak[u].astype(BF16), st[u][6]) for u in un]
    ident = jnp.where(eye, 1.0, 0.0)
    li = [_dot_bf16(lmat[u], lmat[u]) for u in un]
    tmat = [ident + lmat[u] for u in un]
    n_steps = CHUNK.bit_length() - 1
    for it in range(1, n_steps):
        if it + 1 < n_steps:
            res = [_dot_bf16(li[u], jnp.concatenate([li[u], tmat[u]], axis=1)) for u in un]
            tmat = [tmat[u] + res[u][:, PAIR:] for u in un]
            li = [res[u][:, :PAIR] for u in un]
        else:
            tmat = [tmat[u] + _dot_bf16(li[u], tmat[u]) for u in un]
    x = [_dot_bf16(tmat[u], jnp.concatenate([st[u][0].astype(F32), akv[u]], axis=1)) for u in un]
    gmat = [jnp.concatenate([x[u], jnp.concatenate([zero, st[u][6].astype(F32)], axis=1)], axis=0).astype(BF16)
            for u in un]
    out1 = [_dot(lrbk[u], gmat[u]) for u in un]
    out2 = [_dot(gmat[u], jnp.concatenate([st[u][4], st[u][5]], axis=0), TN_DIMS) for u in un]
    qe = [out1[u][:, :PAIR] + st[u][1].astype(F32) for u in un]

    state = [s_ref[p] for p in range(n_pairs)]
    for ci in range(cps):
        base = ci * n_pairs
        y = [_dot_bf16(qe[base + p], state[p], NT_DIMS) + out1[base + p][:, PAIR:] for p in range(n_pairs)]
        nxt = []
        for p in range(n_pairs):
            mmat = out2[base + p][:PAIR] + jnp.where(eye, gam_ref[ci][:, sls[p]], 0.0)
            nxt.append(_dot_split_lhs(state[p], mmat) + out2[base + p][PAIR:])
        state = nxt
        for p in range(n_pairs):
            sl = sls[p]
            mean = jnp.sum(y[p], axis=-1, keepdims=True) * (1.0 / RW_HEAD_DIM)
            d = jnp.where(own, y[p] - mean, 0.0)
            var = jnp.sum(d * d, axis=-1, keepdims=True) * (1.0 / RW_HEAD_DIM)
            yn = d * lax.rsqrt(var + GN_EPS)
            yn = yn[:CHUNK] + yn[CHUNK:]
            out = ((yn * lng_ref[:, sl] + lnb_ref[:, sl] + bonus_ref[rws[ci], sl].astype(F32))
                   * g_ref[rws[ci], sl].astype(F32))
            o_ref[rws[ci], sl] = out.astype(o_ref.dtype)
    for p in range(n_pairs):
        s_ref[p] = state[p]


def _rwkv_chunks(prep, lnx_g, lnx_b, batch, seq, cps, casts=()):
    at, rt, bt, kt, bg, kg, vb, bonus, g, gam = prep
    t = at.shape[0]
    nc = seq // (CHUNK * cps)
    steps = batch * nc
    gam = gam.reshape(t // CHUNK, 1, RW_WIDTH)
    blk = pl.BlockSpec((cps * CHUNK, RW_WIDTH), lambda b, c: (b * nc + c, 0))
    rowspec = pl.BlockSpec((1, RW_WIDTH), lambda b, c: (0, 0))
    cast_specs = [pl.BlockSpec((w.shape[0] // steps, w.shape[1]), lambda b, c: (b * nc + c, 0)) for w in casts]
    outs = pl.pallas_call(
        functools.partial(_rwkv_chunk_kernel, n_cast=len(casts), cps=cps),
        out_shape=[jax.ShapeDtypeStruct((t, RW_WIDTH), BF16)] + [jax.ShapeDtypeStruct(w.shape, BF16) for w in casts],
        grid=(batch, nc),
        in_specs=([blk] * 9 + [pl.BlockSpec((cps, 1, RW_WIDTH), lambda b, c: (b * nc + c, 0, 0)), rowspec, rowspec]
                  + cast_specs),
        out_specs=[blk] + cast_specs,
        scratch_shapes=[pltpu.VMEM((RW_HEADS // 2, PAIR, PAIR), F32)],
        compiler_params=_cparams("arbitrary", "arbitrary"),
        name="rwkv_chunks",
    )(at, rt, bt, kt, bg, kg, vb, bonus, g, gam, lnx_g.reshape(1, RW_WIDTH), lnx_b.reshape(1, RW_WIDTH), *casts)
    return outs[0], outs[1:]


def _cast_kernel(x_ref, o_ref):
    o_ref[...] = x_ref[...].astype(o_ref.dtype)


def _cast_bf16(w, rows):
    n, m = w.shape
    return pl.pallas_call(
        _cast_kernel,
        out_shape=jax.ShapeDtypeStruct((n, m), BF16),
        grid=(n // rows,),
        in_specs=[pl.BlockSpec((rows, m), lambda i: (i, 0))],
        out_specs=pl.BlockSpec((rows, m), lambda i: (i, 0)),
        compiler_params=_cparams("parallel"),
        name="cast_bf16",
    )(w)


def _merge_kernel(h_ref, ya_ref, yb_ref, wga_ref, wgb_ref, pa_ref, pb_ref, o_ref):
    h = h_ref[...]
    ga = jax.nn.sigmoid(_dot(h, wga_ref[...]))
    gb = jax.nn.sigmoid(_dot(h, wgb_ref[...]))
    o_ref[...] = (ga * _dot(ya_ref[...], pa_ref[...]) + gb * _dot(yb_ref[...], pb_ref[...])).astype(o_ref.dtype)


def _merge(h, ya, yb, wga, wgb, pa, pb, tm, tn):
    t, d = h.shape
    n = wga.shape[1]
    ka, kb = ya.shape[1], yb.shape[1]
    return pl.pallas_call(
        _merge_kernel,
        out_shape=jax.ShapeDtypeStruct((t, n), BF16),
        grid=(t // tm, n // tn),
        in_specs=[pl.BlockSpec((tm, d), lambda i, j: (i, 0)),
                  pl.BlockSpec((tm, ka), lambda i, j: (i, 0)),
                  pl.BlockSpec((tm, kb), lambda i, j: (i, 0)),
                  pl.BlockSpec((d, tn), lambda i, j: (0, j)),
                  pl.BlockSpec((d, tn), lambda i, j: (0, j)),
                  pl.BlockSpec((ka, tn), lambda i, j: (0, j)),
                  pl.BlockSpec((kb, tn), lambda i, j: (0, j))],
        out_specs=pl.BlockSpec((tm, tn), lambda i, j: (i, j)),
        compiler_params=_cparams("parallel", "parallel"),
        name="gated_merge",
    )(h, ya, yb, wga, wgb, pa, pb)


def _first_lane_where(cond, lane):
    return jnp.min(jnp.where(cond, lane, LANES), axis=-1, keepdims=True)


def _outproj_kernel(m_ref, x_ref, wo_ref, g2_ref, wrh_ref, wrl_ref, rb_ref, x1_ref, h2_ref, route_ref):
    x1 = x_ref[...] + _dot(m_ref[...], wo_ref[...])
    x1_ref[...] = x1
    ms = jnp.mean(x1 * x1, axis=-1, keepdims=True)
    h2 = x1 * lax.rsqrt(ms + NORM_EPS) * g2_ref[...]
    h2_ref[...] = h2
    hh = h2.astype(BF16)
    hl = (h2 - hh.astype(F32)).astype(BF16)
    wrh = wrh_ref[...]
    lg = _dot(hh, wrh) + _dot(hl, wrh) + _dot(hh, wrl_ref[...]) + rb_ref[...]

    lane = lax.broadcasted_iota(jnp.int32, lg.shape, 1)
    neg = -jnp.inf
    is_g = lane < N_GROUPS
    mg = jnp.max(jnp.where(is_g, lg, neg), axis=-1, keepdims=True)
    eg = jnp.where(is_g, jnp.exp(lg - mg), 0.0)
    pg = eg / jnp.sum(eg, axis=-1, keepdims=True)
    p_g_top = jnp.max(pg, axis=-1, keepdims=True)
    g_idx = _first_lane_where(is_g & (pg == p_g_top), lane)
    lo = N_GROUPS + g_idx * EXPERTS_PER_GROUP
    sel = (lane >= lo) & (lane < lo + EXPERTS_PER_GROUP)
    me = jnp.max(jnp.where(sel, lg, neg), axis=-1, keepdims=True)
    ee = jnp.where(sel, jnp.exp(lg - me), 0.0)
    pe = ee / jnp.sum(ee, axis=-1, keepdims=True)
    pe = jnp.where(sel, pe, -1.0)
    v1 = jnp.max(pe, axis=-1, keepdims=True)
    i1 = _first_lane_where(pe == v1, lane)
    pe2 = jnp.where(lane == i1, -1.0, pe)
    v2 = jnp.max(pe2, axis=-1, keepdims=True)
    i2 = _first_lane_where(pe2 == v2, lane)
    den = v1 + v2
    route = jnp.where(lane == 0, p_g_top * v1 / den,
                      jnp.where(lane == 1, p_g_top * v2 / den,
                                jnp.where(lane == 2, (i1 - N_GROUPS).astype(F32),
                                          jnp.where(lane == 3, (i2 - N_GROUPS).astype(F32), 0.0))))
    route_ref[...] = route


def _outproj(merged, x, wo, g2, wr_hi, wr_lo, rbias, tm):
    t, d = x.shape
    nr = wr_hi.shape[1]
    return pl.pallas_call(
        _outproj_kernel,
        out_shape=[jax.ShapeDtypeStruct((t, d), F32), jax.ShapeDtypeStruct((t, d), F32),
                   jax.ShapeDtypeStruct((t, nr), F32)],
        grid=(t // tm,),
        in_specs=[pl.BlockSpec((tm, d), lambda i: (i, 0)),
                  pl.BlockSpec((tm, d), lambda i: (i, 0)),
                  pl.BlockSpec((d, d), lambda i: (0, 0)),
                  pl.BlockSpec((1, d), lambda i: (0, 0)),
                  pl.BlockSpec((d, nr), lambda i: (0, 0)),
                  pl.BlockSpec((d, nr), lambda i: (0, 0)),
                  pl.BlockSpec((1, nr), lambda i: (0, 0))],
        out_specs=[pl.BlockSpec((tm, d), lambda i: (i, 0)),
                   pl.BlockSpec((tm, d), lambda i: (i, 0)),
                   pl.BlockSpec((tm, nr), lambda i: (i, 0))],
        compiler_params=_cparams("parallel"),
        name="outproj_norm_router",
    )(merged, x, wo, g2.reshape(1, d), wr_hi, wr_lo, rbias)


IDX_SLOTS = 3


def _moe_kernel(be_ref, nused_ref, nvalid_ref, tok_hbm, dst_hbm, h_hbm, wg_ref, wu_ref, wd_ref, y_hbm,
                xbuf, obuf, tok_s, dst_s, idx_sem, g_sem, s_sem, *, rows):
    b = pl.program_id(0)
    n_used = nused_ref[0]
    last = n_used - 1

    def idx_copies(blk, sl):
        return (pltpu.make_async_copy(tok_hbm.at[blk], tok_s.at[sl], idx_sem.at[sl, 0]),
                pltpu.make_async_copy(dst_hbm.at[blk], dst_s.at[sl], idx_sem.at[sl, 1]))

    def gather_row(r, isl, xsl):
        return pltpu.make_async_copy(h_hbm.at[tok_s[isl, r]], xbuf.at[xsl, r], g_sem.at[xsl])

    def gather_all(xsl):
        return pltpu.make_async_copy(h_hbm.at[pl.ds(0, rows)], xbuf.at[xsl], g_sem.at[xsl])

    def scatter_row(r, isl, osl):
        return pltpu.make_async_copy(obuf.at[osl, r], y_hbm.at[dst_s[isl, r]], s_sem.at[osl])

    def wait_scatter(blk, osl):
        n = nvalid_ref[blk]

        @pl.when(n == rows)
        def _():
            pltpu.make_async_copy(obuf.at[osl], y_hbm.at[pl.ds(0, rows)], s_sem.at[osl]).wait()

        @pl.when(n < rows)
        def _():
            def one(r, c):
                pltpu.make_async_copy(obuf.at[osl, 0], y_hbm.at[0], s_sem.at[osl]).wait()
                return c

            lax.fori_loop(0, n, one, 0)

    @pl.when(b == 0)
    def _():
        for cp in idx_copies(0, 0):
            cp.start()
        for cp in idx_copies(0, 0):
            cp.wait()
        for cp in idx_copies(jnp.minimum(1, last), 1):
            cp.start()

        def issue(r, c):
            gather_row(r, 0, 0).start()
            return c

        lax.fori_loop(0, rows, issue, 0, unroll=8)

    @pl.when((b > 1) & (b < n_used))
    def _():
        wait_scatter(b - 2, b % 2)

    for par in range(2):
        @pl.when((b < n_used) & (b % 2 == par))
        def _(par=par):
            nxt_i, ld_i = (b + 1) % IDX_SLOTS, (b + 2) % IDX_SLOTS
            for cp in idx_copies(0, nxt_i):
                cp.wait()
            for r in range(rows):
                gather_row(r, nxt_i, 1 - par).start()
            for cp in idx_copies(jnp.minimum(b + 2, last), ld_i):
                cp.start()

            gather_all(par).wait()
            x = xbuf[par].astype(BF16)
            gate = _dot(x, wg_ref[...])
            up = _dot(x, wu_ref[...])
            mid = (gate * jax.nn.sigmoid(gate) * up).astype(BF16)
            obuf[par] = _dot(mid, wd_ref[...])

        @pl.when((b < n_used) & (b % 2 == par) & (nvalid_ref[b] == rows))
        def _(par=par):
            for r in range(rows):
                scatter_row(r, b % IDX_SLOTS, par).start()

    @pl.when((b < n_used) & (nvalid_ref[b] < rows))
    def _():
        def issue(r, c):
            scatter_row(r, b % IDX_SLOTS, b % 2).start()
            return c

        lax.fori_loop(0, nvalid_ref[b], issue, 0)

    @pl.when(b == last)
    def _():
        @pl.when(b > 0)
        def _():
            wait_scatter(b - 1, (b - 1) % 2)

        wait_scatter(b, b % 2)
        gather_all((b + 1) % 2).wait()
        for cp in idx_copies(0, (b + 2) % IDX_SLOTS):
            cp.wait()


def _moe(h2, row_tok, row_dst, block_e, n_used, nvalid, wg, wu, wd, rows):
    t, d = h2.shape
    n_blocks = row_tok.shape[0]
    a = t * TOP_K
    f = wg.shape[2]
    kern = functools.partial(_moe_kernel, rows=rows)
    grid_spec = pltpu.PrefetchScalarGridSpec(
        num_scalar_prefetch=3,
        grid=(n_blocks,),
        in_specs=[pl.BlockSpec(memory_space=pl.ANY),
                  pl.BlockSpec(memory_space=pl.ANY),
                  pl.BlockSpec(memory_space=pl.ANY),
                  pl.BlockSpec((None, d, f), lambda b, be, nu, nv: (be[b], 0, 0)),
                  pl.BlockSpec((None, d, f), lambda b, be, nu, nv: (be[b], 0, 0)),
                  pl.BlockSpec((None, f, d), lambda b, be, nu, nv: (be[b], 0, 0))],
        out_specs=pl.BlockSpec(memory_space=pl.ANY),
        scratch_shapes=[pltpu.VMEM((2, rows, d), F32),
                        pltpu.VMEM((2, rows, d), F32),
                        pltpu.SMEM((IDX_SLOTS, rows), jnp.int32),
                        pltpu.SMEM((IDX_SLOTS, rows), jnp.int32),
                        pltpu.SemaphoreType.DMA((IDX_SLOTS, 2)),
                        pltpu.SemaphoreType.DMA((2,)),
                        pltpu.SemaphoreType.DMA((2,))],
    )
    return pl.pallas_call(
        kern,
        out_shape=jax.ShapeDtypeStruct((a, d), F32),
        grid_spec=grid_spec,
        compiler_params=_cparams("arbitrary"),
        name="moe_experts",
    )(block_e, n_used, nvalid, row_tok, row_dst, h2, wg, wu, wd)


def _combine_kernel(x1_ref, y0_ref, y1_ref, route_ref, o_ref):
    gts = route_ref[...]
    o_ref[...] = x1_ref[...] + gts[:, 0:1] * y0_ref[...] + gts[:, 1:2] * y1_ref[...]


def _combine(x1, y, route, tm):
    t, d = x1.shape
    nb = t // tm
    return pl.pallas_call(
        _combine_kernel,
        out_shape=jax.ShapeDtypeStruct((t, d), F32),
        grid=(nb,),
        in_specs=[pl.BlockSpec((tm, d), lambda i: (i, 0)),
                  pl.BlockSpec((tm, d), lambda i: (i, 0)),
                  pl.BlockSpec((tm, d), lambda i: (nb + i, 0)),
                  pl.BlockSpec((tm, LANES), lambda i: (i, 0))],
        out_specs=pl.BlockSpec((tm, d), lambda i: (i, 0)),
        compiler_params=_cparams("parallel"),
        name="moe_combine",
    )(x1, y, y, route)


def _routing_tables(route, rows):
    t = route.shape[0]
    a = t * TOP_K
    expert = route[:, 2:2 + TOP_K].astype(jnp.int32)
    flat_e = expert.reshape(a)
    ids = jnp.arange(a, dtype=jnp.int32)
    order = jnp.sort(flat_e * a + ids) % a
    counts = jnp.sum((flat_e[:, None] == jnp.arange(N_EXPERTS, dtype=jnp.int32)[None, :]).astype(jnp.int32), axis=0)
    padded = (counts + rows - 1) // rows * rows
    start = jnp.cumsum(counts) - counts
    pend = jnp.cumsum(padded)
    pstart = pend - padded
    n_rows = a + N_EXPERTS * rows
    n_blocks = n_rows // rows
    blk_row0 = jnp.arange(n_blocks, dtype=jnp.int32) * rows
    block_e = jnp.minimum(jnp.sum((pend[None, :] <= blk_row0[:, None]).astype(jnp.int32), axis=1), N_EXPERTS - 1)
    off0 = blk_row0 - pstart[block_e]
    cnt_b = counts[block_e]
    off = off0[:, None] + jnp.arange(rows, dtype=jnp.int32)[None, :]
    valid = (off >= 0) & (off < cnt_b[:, None])
    src = order[jnp.clip(start[block_e][:, None] + off, 0, a - 1)]
    row_tok = jnp.where(valid, src // TOP_K, 0)
    row_dst = jnp.where(valid, (src % TOP_K) * t + src // TOP_K, 0)
    n_used = (pend[-1] // rows).astype(jnp.int32).reshape(1)
    nvalid = jnp.clip(cnt_b - off0, 0, rows).astype(jnp.int32)
    return row_tok, row_dst, block_e.astype(jnp.int32), n_used, nvalid


def _tiles(t, seq):
    return dict(
        norm_tm=min(512, t),
        proj_tm=min(1024, seq), proj_tn=512,
        attn_tq=min(256, seq), attn_tk=min(1024, seq), attn_heads=4,
        prep_tm=min(256, seq), rwkv_cps=2,
        merge_tm=min(1024, t), merge_tn=512,
        out_tm=min(512, t),
        moe_rows=256,
        comb_tm=min(512, t),
    )


def _pad_rows(w, n):
    return jnp.pad(w, ((0, n - w.shape[0]), (0, 0)))


def _pad_cols(w, n):
    return jnp.pad(w, ((0, 0), (0, n - w.shape[1])))


def kernel(x, norm1_g, w_in, q_norm_g, k_norm_g, lam_q1, lam_k1, lam_q2, lam_k2, subln_g, shift_mu, w0, w_up, a0, a_up, g_up, k_k, k_a, r_k, lnx_g, lnx_b, proj_a, proj_b, w_out, norm2_g, router_g, router_g_b, router_e, router_e_b, w_gate_e, w_up_e, w_down_e):
    batch, seq, d = x.shape
    t = batch * seq
    depth = norm1_g.shape[0]
    tl = _tiles(t, seq)
    qkw = DA_HEADS * 2 * DA_HEAD_DIM
    vw = DA_HEADS * DA_V_DIM
    c_q, c_k, c_v = 0, qkw, 2 * qkw
    c_rw = c_v + vw
    c_dw = c_rw + 3 * RW_WIDTH
    c_da = c_dw + DECAY_LORA
    c_dg = c_da + AAA_LORA
    c_ga = c_dg + GATE_LORA
    c_gb = c_ga + d
    cos_t, sinm_t, sinp_t = _rope_tables(seq)
    xf = x.reshape(t, d)

    for l in range(depth):
        lam_init = 0.8 - 0.6 * math.exp(-0.3 * l)
        wl = w_in[l]
        w_qk = wl[:, c_q:c_v].astype(BF16)
        w_v = wl[:, c_v:c_rw].astype(BF16)
        w_rkv = wl[:, c_rw:c_dw].astype(BF16)
        w_lo = jnp.concatenate([_pad_cols(wl[:, c_dw:c_da], LANES), _pad_cols(wl[:, c_da:c_dg], LANES),
                                wl[:, c_dg:c_ga]], axis=1).astype(BF16)
        w_ga = wl[:, c_ga:c_gb].astype(BF16)
        w_gb = wl[:, c_gb:].astype(BF16)
        mu = shift_mu[l]
        o_dw = 3 * RW_WIDTH
        mu_rkv = mu[:o_dw].reshape(1, -1)
        mu_lo = jnp.concatenate([jnp.pad(mu[o_dw:o_dw + DECAY_LORA], (0, LANES - DECAY_LORA)),
                                 jnp.pad(mu[o_dw + DECAY_LORA:o_dw + DECAY_LORA + AAA_LORA], (0, LANES - AAA_LORA)),
                                 mu[o_dw + DECAY_LORA + AAA_LORA:]]).reshape(1, -1)
        gain_row = jnp.concatenate([jnp.tile(q_norm_g[l], 2 * DA_HEADS) * (DA_HEAD_DIM ** -0.5 * LOG2_E),
                                    jnp.tile(k_norm_g[l], 2 * DA_HEADS)]).reshape(1, 2 * qkw)
        lam_params = jnp.stack([lam_q1[l], lam_k1[l], lam_q2[l], lam_k2[l]])
        zrow = jnp.zeros((RW_WIDTH,), F32)
        rw_params = jnp.stack([w0[l], a0[l], k_k[l], k_a[l], r_k[l].reshape(-1), zrow, zrow, zrow])

        h = _rmsnorm(xf, norm1_g[l], NORM_EPS, tl["norm_tm"])
        qk = _qk_proj(h, w_qk, gain_row, cos_t, sinm_t, sinp_t, seq, tl["proj_tm"], tl["proj_tn"])
        v = _matmul(h, w_v, BF16, tl["proj_tm"], tl["proj_tn"], "v_proj")
        ya = _diff_attention(qk, v, lam_params, subln_g[l], batch, seq, lam_init, tl["attn_tq"], tl["attn_tk"],
                             tl["attn_heads"])

        prep = _rwkv_prep(h, w_rkv, w_lo, mu_rkv, mu_lo, rw_params, _pad_rows(w_up[l], LANES).astype(BF16),
                          _pad_rows(a_up[l], LANES).astype(BF16), g_up[l].astype(BF16), seq, tl["prep_tm"])
        ew = (w_gate_e[l], w_up_e[l], w_down_e[l])
        ew2d = tuple(w.reshape(-1, w.shape[-1]) for w in ew)
        cps = tl["rwkv_cps"]
        steps = batch * (seq // (CHUNK * cps))
        ride = all(w.shape[0] % (steps * 2 * SUBLANES) == 0 and w.size * 4 // steps <= CAST_BLOCK_BYTES for w in ew2d)
        yb, ew_bf16 = _rwkv_chunks(prep, lnx_g[l], lnx_b[l], batch, seq, cps, ew2d if ride else ())
        if not ride:
            ew_bf16 = tuple(_cast_bf16(w, CAST_BLOCK_BYTES // (4 * w.shape[1])) for w in ew2d)
        wg_e, wu_e, wd_e = (c.reshape(w.shape) for c, w in zip(ew_bf16, ew))

        merged = _merge(h, ya, yb, w_ga, w_gb, proj_a[l].astype(BF16), proj_b[l].astype(BF16),
                        tl["merge_tm"], tl["merge_tn"])
        wr = _pad_cols(jnp.concatenate([router_g[l], router_e[l]], axis=1), LANES)
        wr_hi = wr.astype(BF16)
        wr_lo = (wr - wr_hi.astype(F32)).astype(BF16)
        rbias = jnp.pad(jnp.concatenate([router_g_b[l], router_e_b[l]]), (0, LANES - N_GROUPS - N_EXPERTS))
        x1, h2, route = _outproj(merged, xf, w_out[l].astype(BF16), norm2_g[l], wr_hi, wr_lo,
                                 rbias.reshape(1, LANES), tl["out_tm"])

        rows = tl["moe_rows"]
        row_tok, row_dst, block_e, n_used, nvalid = _routing_tables(route, rows)
        y = _moe(h2, row_tok, row_dst, block_e, n_used, nvalid, wg_e, wu_e, wd_e, rows)
        xf = _combine(x1, y, route, tl["comb_tm"])
    return xf.reshape(batch, seq, d)
```

```python
import functools
import math

import jax
import jax.numpy as jnp
from jax import lax
from jax.experimental import pallas as pl
from jax.experimental.pallas import tpu as pltpu

DA_HEADS = 8
DA_HEAD_DIM = 64
DA_V_DIM = 2 * DA_HEAD_DIM
ROT_DIM = DA_HEAD_DIM // 4
ROPE_THETA = 500000.0
SUBLN_EPS = 1e-5
RW_HEADS = 16
RW_HEAD_DIM = 64
RW_WIDTH = RW_HEADS * RW_HEAD_DIM
DECAY_LORA = 96
AAA_LORA = 96
GATE_LORA = 256
GN_EPS = 64e-5
N_GROUPS = 4
EXPERTS_PER_GROUP = 8
N_EXPERTS = N_GROUPS * EXPERTS_PER_GROUP
TOP_K = 2
NORM_EPS = 1e-6

LANES = 128
SUBLANES = 8
VMEM_LIMIT_BYTES = 56 * 1024 * 1024

CHUNK = 64
CAST_BLOCK_BYTES = 2 * 1024 * 1024

F32 = jnp.float32
BF16 = jnp.bfloat16

LOG2_E = math.log2(math.e)

NT_DIMS = (((1,), (1,)), ((), ()))
TN_DIMS = (((0,), (0,)), ((), ()))


def _cparams(*sem):
    return pltpu.CompilerParams(dimension_semantics=tuple(sem), vmem_limit_bytes=VMEM_LIMIT_BYTES)


def _dot(a, b, dims=None, precision=None):
    if dims is None:
        return jnp.dot(a, b, preferred_element_type=F32, precision=precision)
    return lax.dot_general(a, b, dims, preferred_element_type=F32, precision=precision)


def _split3(x):
    h = x.astype(BF16)
    r = x - h.astype(F32)
    m = r.astype(BF16)
    l = (r - m.astype(F32)).astype(BF16)
    return h, m, l


def _group_sum64(x, bd, terms):
    parts = _split3(x)[:terms]
    outs = []
    for s in range(x.shape[1] // LANES):
        sl = slice(s * LANES, (s + 1) * LANES)
        acc = _dot(parts[0][:, sl], bd)
        for part in parts[1:]:
            acc = acc + _dot(part[:, sl], bd)
        outs.append(acc)
    return outs[0] if len(outs) == 1 else jnp.concatenate(outs, axis=1)


def _block_diag_ones(n, blk, dtype=BF16):
    r = lax.broadcasted_iota(jnp.int32, (n, n), 0) // blk
    c = lax.broadcasted_iota(jnp.int32, (n, n), 1) // blk
    return jnp.where(r == c, 1.0, 0.0).astype(dtype)


def _rmsnorm_kernel(x_ref, g_ref, o_ref, *, eps):
    x = x_ref[...]
    ms = jnp.mean(x * x, axis=-1, keepdims=True)
    o_ref[...] = (x * lax.rsqrt(ms + eps) * g_ref[...]).astype(o_ref.dtype)


def _rmsnorm(x, g, eps, tm):
    t, d = x.shape
    return pl.pallas_call(
        functools.partial(_rmsnorm_kernel, eps=eps),
        out_shape=jax.ShapeDtypeStruct((t, d), BF16),
        grid=(t // tm,),
        in_specs=[pl.BlockSpec((tm, d), lambda i: (i, 0)),
                  pl.BlockSpec((1, d), lambda i: (0, 0))],
        out_specs=pl.BlockSpec((tm, d), lambda i: (i, 0)),
        compiler_params=_cparams("parallel"),
        name="rmsnorm",
    )(x, g.reshape(1, d))


def _matmul_kernel(a_ref, w_ref, o_ref):
    o_ref[...] = _dot(a_ref[...], w_ref[...]).astype(o_ref.dtype)


def _matmul(a, w, out_dtype, tm, tn, name):
    t, k = a.shape
    n = w.shape[1]
    return pl.pallas_call(
        _matmul_kernel,
        out_shape=jax.ShapeDtypeStruct((t, n), out_dtype),
        grid=(t // tm, n // tn),
        in_specs=[pl.BlockSpec((tm, k), lambda i, j: (i, 0)),
                  pl.BlockSpec((k, tn), lambda i, j: (0, j))],
        out_specs=pl.BlockSpec((tm, tn), lambda i, j: (i, j)),
        compiler_params=_cparams("parallel", "parallel"),
        name=name,
    )(a, w)


def _qk_proj_kernel(a_ref, w_ref, gain_ref, cos_ref, sinm_ref, sinp_ref, o_ref, *, tn):
    acc = _dot(a_ref[...], w_ref[...])
    bd = _block_diag_ones(LANES, DA_HEAD_DIM)
    ms = _group_sum64(acc * acc, bd, 1) * (1.0 / DA_HEAD_DIM)
    xn = acc * lax.rsqrt(ms + NORM_EPS) * gain_ref[...]
    reps = tn // LANES
    cos = jnp.tile(cos_ref[...], (1, reps))
    sinm = jnp.tile(sinm_ref[...], (1, reps))
    sinp = jnp.tile(sinp_ref[...], (1, reps))
    half = ROT_DIM // 2
    hi = pltpu.roll(xn, tn - half, 1)
    lo = pltpu.roll(xn, half, 1)
    o_ref[...] = (xn * cos + hi * sinm + lo * sinp).astype(o_ref.dtype)


def _qk_proj(h, w_qk, gain_row, cos_t, sinm_t, sinp_t, seq, tm, tn):
    t, k = h.shape
    n = w_qk.shape[1]
    nseq = seq // tm
    return pl.pallas_call(
        functools.partial(_qk_proj_kernel, tn=tn),
        out_shape=jax.ShapeDtypeStruct((t, n), BF16),
        grid=(t // tm, n // tn),
        in_specs=[pl.BlockSpec((tm, k), lambda i, j: (i, 0)),
                  pl.BlockSpec((k, tn), lambda i, j: (0, j)),
                  pl.BlockSpec((1, tn), lambda i, j: (0, j)),
                  pl.BlockSpec((tm, LANES), lambda i, j: (i % nseq, 0)),
                  pl.BlockSpec((tm, LANES), lambda i, j: (i % nseq, 0)),
                  pl.BlockSpec((tm, LANES), lambda i, j: (i % nseq, 0))],
        out_specs=pl.BlockSpec((tm, tn), lambda i, j: (i, j)),
        compiler_params=_cparams("parallel", "parallel"),
        name="qk_proj",
    )(h, w_qk, gain_row, cos_t, sinm_t, sinp_t)


def _rope_tables(seq):
    half = ROT_DIM // 2
    inv_freq = ROPE_THETA ** (-jnp.arange(0, ROT_DIM, 2, dtype=F32) / ROT_DIM)
    ang = jnp.arange(seq, dtype=F32)[:, None] * inv_freq[None, :]
    cos, sin = jnp.cos(ang), jnp.sin(ang)
    ones = jnp.ones((seq, DA_HEAD_DIM - ROT_DIM), F32)
    zeros = jnp.zeros((seq, DA_HEAD_DIM - ROT_DIM), F32)
    zh = jnp.zeros((seq, half), F32)
    cos64 = jnp.concatenate([cos, cos, ones], axis=1)
    sinm64 = jnp.concatenate([-sin, zh, zeros], axis=1)
    sinp64 = jnp.concatenate([zh, sin, zeros], axis=1)
    return tuple(jnp.concatenate([a, a], axis=1) for a in (cos64, sinm64, sinp64))


def _diff_attn_kernel(lam_ref, q_ref, k_ref, v_ref, g_ref, o_ref, m_ref, l_ref, a_ref, *, tq, tk, lam_init, nh):
    i = pl.program_id(2)
    lane = lax.broadcasted_iota(jnp.int32, (1, DA_V_DIM), 1)
    qs = []
    for h in range(nh):
        q = q_ref[:, h * DA_V_DIM:(h + 1) * DA_V_DIM]
        zero = jnp.zeros_like(q)
        qs.append(jnp.where(lane < DA_HEAD_DIM, q, zero))
        qs.append(jnp.where(lane >= DA_HEAD_DIM, q, zero))
    nc = 2 * nh
    qb = [jnp.concatenate([qs[2 * h], qs[2 * h + 1]], axis=0) for h in range(nh)]
    m_ref[...] = jnp.full(m_ref.shape, -jnp.inf, F32)
    l_ref[...] = jnp.zeros(l_ref.shape, F32)
    a_ref[...] = jnp.zeros(a_ref.shape, F32)

    def step(off, width, mask):
        reps = width // LANES
        ks = [k_ref[pl.ds(off, width), h * DA_V_DIM:(h + 1) * DA_V_DIM] for h in range(nh)]
        vs = [v_ref[pl.ds(off, width), h * DA_V_DIM:(h + 1) * DA_V_DIM] for h in range(nh)]
        sb = [_dot(qb[h], ks[h], NT_DIMS) for h in range(nh)]
        s = [sb[c // 2][(c % 2) * tq:(c % 2 + 1) * tq] for c in range(nc)]
        ps = []
        for c in range(nc):
            sc = s[c] if mask is None else jnp.where(mask, s[c], -jnp.inf)
            m_old = m_ref[c]
            m_new = jnp.maximum(m_old, jnp.broadcast_to(jnp.max(sc, axis=-1, keepdims=True), m_old.shape))
            alpha = jnp.exp2(m_old - m_new)
            p = jnp.exp2(sc - jnp.tile(m_new, (1, reps)))
            part = p[:, :LANES]
            for kk in range(1, reps):
                part = part + p[:, kk * LANES:(kk + 1) * LANES]
            l_ref[c] = alpha * l_ref[c] + part
            a_ref[c] = alpha * a_ref[c]
            m_ref[c] = m_new
            ps.append(p.astype(BF16))
        for h in range(nh):
            pv = _dot(jnp.concatenate([ps[2 * h], ps[2 * h + 1]], axis=0), vs[h])
            a_ref[2 * h] = a_ref[2 * h] + pv[:tq]
            a_ref[2 * h + 1] = a_ref[2 * h + 1] + pv[tq:]

    n_full = (i * tq) // tk

    def full_body(j, c):
        step(pl.multiple_of(j * tk, tk), tk, None)
        return c

    lax.fori_loop(0, n_full, full_body, 0)

    sub = tk // tq
    rem = i % sub
    base = pl.multiple_of(n_full * tk, tk)
    for v in range(sub):
        @pl.when(rem == v)
        def _(v=v):
            width = (v + 1) * tq
            row = v * tq + lax.broadcasted_iota(jnp.int32, (tq, width), 0)
            col = lax.broadcasted_iota(jnp.int32, (tq, width), 1)
            step(base, width, col <= row)

    lq1, lk1, lq2, lk2 = (lam_ref[r:r + 1, :] for r in range(4))
    lam = (jnp.exp(jnp.sum(lq1 * lk1, axis=-1, keepdims=True))
           - jnp.exp(jnp.sum(lq2 * lk2, axis=-1, keepdims=True)) + lam_init)
    for h in range(nh):
        l1 = jnp.sum(l_ref[2 * h], axis=-1, keepdims=True)
        l2 = jnp.sum(l_ref[2 * h + 1], axis=-1, keepdims=True)
        o = a_ref[2 * h] / l1 - lam * (a_ref[2 * h + 1] / l2)
        ms = jnp.mean(o * o, axis=-1, keepdims=True)
        o = o * lax.rsqrt(ms + SUBLN_EPS) * (g_ref[...] * (1.0 - lam_init))
        o_ref[:, h * DA_V_DIM:(h + 1) * DA_V_DIM] = o.astype(o_ref.dtype)


def _diff_attention(qk, v, lam_params, subln_g, batch, seq, lam_init, tq, tk, nh):
    assert tk % tq == 0 and seq % tk == 0 and DA_HEADS % nh == 0
    t = qk.shape[0]
    nq = seq // tq
    w = nh * DA_V_DIM
    hg = DA_HEADS // nh
    kern = functools.partial(_diff_attn_kernel, tq=tq, tk=tk, lam_init=lam_init, nh=nh)
    return pl.pallas_call(
        kern,
        out_shape=jax.ShapeDtypeStruct((t, DA_HEADS * DA_V_DIM), BF16),
        grid=(batch, hg, nq),
        in_specs=[pl.BlockSpec((4, DA_HEAD_DIM), lambda b, h, i: (0, 0)),
                  pl.BlockSpec((tq, w), lambda b, h, i: (b * nq + i, h)),
                  pl.BlockSpec((seq, w), lambda b, h, i: (b, hg + h)),
                  pl.BlockSpec((seq, w), lambda b, h, i: (b, h)),
                  pl.BlockSpec((1, DA_V_DIM), lambda b, h, i: (0, 0))],
        out_specs=pl.BlockSpec((tq, w), lambda b, h, i: (b * nq + i, h)),
        scratch_shapes=[pltpu.VMEM((2 * nh, tq, LANES), F32), pltpu.VMEM((2 * nh, tq, LANES), F32),
                        pltpu.VMEM((2 * nh, tq, DA_V_DIM), F32)],
        compiler_params=_cparams("parallel", "parallel", "parallel"),
        name="diff_attention",
    )(lam_params, qk, qk, v, subln_g.reshape(1, DA_V_DIM))


DECAY_SCALE = math.exp(-0.5)


def _rwkv_prep_kernel(h_ref, w_ref, wlo_ref, mu_ref, mulo_ref, par_ref, wup_ref, aup_ref, gup_ref,
                      at_ref, rt_ref, bt_ref, kt_ref, bg_ref, kg_ref, vb_ref, bonus_ref, g_ref, gam_ref,
                      last_ref, lastlo_ref, *, tm, nseq):
    i = pl.program_id(0)

    @pl.when(i == 0)
    def _():
        last_ref[...] = jnp.zeros(last_ref.shape, F32)
        lastlo_ref[...] = jnp.zeros(lastlo_ref.shape, F32)

    seq_start = (i % nseq) == 0
    row0 = lax.broadcasted_iota(jnp.int32, (tm, 1), 0) == 0
    h = h_ref[...]

    def shifted_proj(w_r, mu_r, carry_ref):
        x = _dot(h, w_r[...])
        last = jnp.where(seq_start, 0.0, carry_ref[...])
        prev = jnp.where(row0, last, pltpu.roll(x, 1, 0))
        carry_ref[...] = x[tm - 1:tm, :]
        return x + (prev - x) * mu_r[...]

    z = shifted_proj(w_ref, mu_ref, last_ref)
    lo = shifted_proj(wlo_ref, mulo_ref, lastlo_ref)
    r, k, v = z[:, :RW_WIDTH], z[:, RW_WIDTH:2 * RW_WIDTH], z[:, 2 * RW_WIDTH:]
    dw, da, dg = lo[:, :LANES], lo[:, LANES:2 * LANES], lo[:, 2 * LANES:]
    w0, a0, k_k, k_a, r_k = (par_ref[j:j + 1, :] for j in range(5))

    u = w0 + _dot(jnp.tanh(dw).astype(BF16), wup_ref[...])
    ld = -DECAY_SCALE * jax.nn.sigmoid(u)
    a = jax.nn.sigmoid(a0 + _dot(da.astype(BF16), aup_ref[...]))
    g = _dot(jax.nn.sigmoid(dg).astype(BF16), gup_ref[...])

    bd = _block_diag_ones(LANES, RW_HEAD_DIM)
    kk = k * k_k
    kk = kk / jnp.maximum(jnp.sqrt(_group_sum64(kk * kk, bd, 1)), 1e-12)
    k2 = k * (1.0 + (a - 1.0) * k_a)
    bonus = _group_sum64(r * k2 * r_k, bd, 1) * v

    t_i = lax.broadcasted_iota(jnp.int32, (tm, tm), 0)
    s_i = lax.broadcasted_iota(jnp.int32, (tm, tm), 1)
    same = (t_i // CHUNK) == (s_i // CHUNK)
    tri = jnp.where(same & (s_i <= t_i), 1.0, 0.0).astype(BF16)
    rest = jnp.where(same & (s_i > t_i), 1.0, 0.0).astype(BF16)
    c_i = lax.broadcasted_iota(jnp.int32, (tm // CHUNK, tm), 0)
    cs_i = lax.broadcasted_iota(jnp.int32, (tm // CHUNK, tm), 1)
    whole = jnp.where(cs_i // CHUNK == c_i, 1.0, 0.0).astype(BF16)

    ldh, ldm = _split2(ld)

    def sel(m):
        return _dot(m, ldh) + _dot(m, ldm)

    cum = sel(tri)
    e_neg = jnp.exp(-cum)
    e_rem = jnp.exp(sel(rest))
    b = kk * a
    at_ref[...] = (-kk * jnp.exp(cum - ld)).astype(BF16)
    rt_ref[...] = (r * jnp.exp(cum)).astype(BF16)
    bt_ref[...] = (b * e_neg).astype(BF16)
    kt_ref[...] = (k2 * e_neg).astype(BF16)
    bg_ref[...] = (b * e_rem).astype(BF16)
    kg_ref[...] = (k2 * e_rem).astype(BF16)
    vb_ref[...] = v.astype(BF16)
    bonus_ref[...] = bonus.astype(BF16)
    g_ref[...] = g.astype(BF16)
    gam_ref[0] = jnp.exp(sel(whole))


def _rwkv_prep(h, w_rkv, w_lo, mu_rkv, mu_lo, params, wup, aup, gup, seq, tm):
    t, d = h.shape
    n, nlo = w_rkv.shape[1], w_lo.shape[1]
    nseq = seq // tm
    whole = lambda shape: pl.BlockSpec(shape, lambda i: (0, 0), pipeline_mode=pl.Buffered(1))
    in_specs = [pl.BlockSpec((tm, d), lambda i: (i, 0)), whole((d, n)), whole((d, nlo)), whole((1, n)), whole((1, nlo)),
                whole((SUBLANES, RW_WIDTH)), whole((LANES, RW_WIDTH)), whole((LANES, RW_WIDTH)),
                whole((2 * LANES, RW_WIDTH))]
    out_blk = pl.BlockSpec((tm, RW_WIDTH), lambda i: (i, 0))
    outs = [jax.ShapeDtypeStruct((t, RW_WIDTH), BF16)] * 9
    outs.append(jax.ShapeDtypeStruct((t // tm, tm // CHUNK, RW_WIDTH), F32))
    out_specs = [out_blk] * 9 + [pl.BlockSpec((1, tm // CHUNK, RW_WIDTH), lambda i: (i, 0, 0))]
    return pl.pallas_call(
        functools.partial(_rwkv_prep_kernel, tm=tm, nseq=nseq),
        out_shape=outs,
        grid=(t // tm,),
        in_specs=in_specs,
        out_specs=out_specs,
        scratch_shapes=[pltpu.VMEM((1, n), F32), pltpu.VMEM((1, nlo), F32)],
        compiler_params=_cparams("arbitrary"),
        name="rwkv_prep",
    )(h, w_rkv, w_lo, mu_rkv, mu_lo, params, wup, aup, gup)


PAIR = 2 * RW_HEAD_DIM


def _dot_bf16(a, b, dims=None):
    return _dot(a.astype(BF16), b.astype(BF16), dims)


def _split2(x):
    h = x.astype(BF16)
    return h, (x - h.astype(F32)).astype(BF16)


def _dot_split_lhs(a, b, dims=None):
    ah, al = _split2(a)
    bh = b.astype(BF16)
    return _dot(ah, bh, dims) + _dot(al, bh, dims)


def _rwkv_chunk_kernel(at_ref, rt_ref, bt_ref, kt_ref, bg_ref, kg_ref, v_ref, bonus_ref, g_ref,
                       gam_ref, lng_ref, lnb_ref, *rest, n_cast, cps):
    cast_in, o_ref, cast_out, s_ref = rest[:n_cast], rest[n_cast], rest[n_cast + 1:2 * n_cast + 1], rest[-1]
    c = pl.program_id(1)

    for src, dst in zip(cast_in, cast_out):
        dst[...] = src[...].astype(dst.dtype)

    @pl.when(c == 0)
    def _():
        s_ref[...] = jnp.zeros(s_ref.shape, F32)

    lane = lax.broadcasted_iota(jnp.int32, (1, PAIR), 1)
    first = lane < RW_HEAD_DIM
    rho = lax.broadcasted_iota(jnp.int32, (PAIR, PAIR), 0)
    sig = lax.broadcasted_iota(jnp.int32, (PAIR, PAIR), 1)
    strict, incl, eye = sig < rho, sig <= rho, sig == rho
    own = (rho // RW_HEAD_DIM) == (sig // RW_HEAD_DIM)

    def stacked(x):
        z = jnp.zeros_like(x)
        return jnp.concatenate([jnp.where(first, x, z), jnp.where(first, z, x)], axis=0)

    n_pairs = RW_HEADS // 2
    units = [(ci, p) for ci in range(cps) for p in range(n_pairs)]
    rws = [slice(ci * CHUNK, (ci + 1) * CHUNK) for ci in range(cps)]
    sls = [slice(p * PAIR, (p + 1) * PAIR) for p in range(n_pairs)]
    zero = jnp.zeros((PAIR, PAIR), F32)
    st = [[stacked(ref[rws[ci], sls[p]]) for ref in (at_ref, rt_ref, bt_ref, kt_ref, bg_ref, kg_ref, v_ref)]
          for ci, p in units]
    un = range(len(units))
    prods = [_dot(jnp.concatenate([q[0], q[1]], axis=0), jnp.concatenate([q[2], q[3]], axis=0), NT_DIMS) for q in st]
    lmat = [jnp.where(strict, pr_[:PAIR, :PAIR], zero) for pr_ in prods]
    sak = [jnp.where(strict, pr_[:PAIR, PAIR:], zero) for pr_ in prods]
    lrbk = [jnp.concatenate([jnp.where(incl, pr_[PAIR:, :PAIR], zero), jnp.where(incl, pr_[PAIR:, PAIR:], zero)],
                            axis=1).astype(BF16) for pr_ in prods]
    akv = [_dot(sak[u].astype(BF16), st[u][6]) for u in un]
    ident = jnp.where(eye, 1.0, 0.0)
    li = [_dot_bf16(lmat[u], lmat[u]) for u in un]
    tmat = [ident + lmat[u] for u in un]
    n_steps = CHUNK.bit_length() - 1
    for it in range(1, n_steps):
        if it + 1 < n_steps:
            res = [_dot_bf16(li[u], jnp.concatenate([li[u], tmat[u]], axis=1)) for u in un]
            tmat = [tmat[u] + res[u][:, PAIR:] for u in un]
            li = [res[u][:, :PAIR] for u in un]
        else:
            tmat = [tmat[u] + _dot_bf16(li[u], tmat[u]) for u in un]
    x = [_dot_bf16(tmat[u], jnp.concatenate([st[u][0].astype(F32), akv[u]], axis=1)) for u in un]
    gmat = [jnp.concatenate([x[u], jnp.concatenate([zero, st[u][6].astype(F32)], axis=1)], axis=0).astype(BF16)
            for u in un]
    out1 = [_dot(lrbk[u], gmat[u]) for u in un]
    out2 = [_dot(gmat[u], jnp.concatenate([st[u][4], st[u][5]], axis=0), TN_DIMS) for u in un]
    qe = [out1[u][:, :PAIR] + st[u][1].astype(F32) for u in un]

    state = [s_ref[p] for p in range(n_pairs)]
    for ci in range(cps):
        base = ci * n_pairs
        y = [_dot_bf16(qe[base + p], state[p], NT_DIMS) + out1[base + p][:, PAIR:] for p in range(n_pairs)]
        nxt = []
        for p in range(n_pairs):
            mmat = out2[base + p][:PAIR] + jnp.where(eye, gam_ref[ci][:, sls[p]], 0.0)
            nxt.append(_dot_split_lhs(state[p], mmat) + out2[base + p][PAIR:])
        state = nxt
        for p in range(n_pairs):
            sl = sls[p]
            mean = jnp.sum(y[p], axis=-1, keepdims=True) * (1.0 / RW_HEAD_DIM)
            d = jnp.where(own, y[p] - mean, 0.0)
            var = jnp.sum(d * d, axis=-1, keepdims=True) * (1.0 / RW_HEAD_DIM)
            yn = d * lax.rsqrt(var + GN_EPS)
            yn = yn[:CHUNK] + yn[CHUNK:]
            out = ((yn * lng_ref[:, sl] + lnb_ref[:, sl] + bonus_ref[rws[ci], sl].astype(F32))
                   * g_ref[rws[ci], sl].astype(F32))
            o_ref[rws[ci], sl] = out.astype(o_ref.dtype)
    for p in range(n_pairs):
        s_ref[p] = state[p]


def _rwkv_chunks(prep, lnx_g, lnx_b, batch, seq, cps, casts=()):
    at, rt, bt, kt, bg, kg, vb, bonus, g, gam = prep
    t = at.shape[0]
    nc = seq // (CHUNK * cps)
    steps = batch * nc
    gam = gam.reshape(t // CHUNK, 1, RW_WIDTH)
    blk = pl.BlockSpec((cps * CHUNK, RW_WIDTH), lambda b, c: (b * nc + c, 0))
    rowspec = pl.BlockSpec((1, RW_WIDTH), lambda b, c: (0, 0))
    cast_specs = [pl.BlockSpec((w.shape[0] // steps, w.shape[1]), lambda b, c: (b * nc + c, 0)) for w in casts]
    outs = pl.pallas_call(
        functools.partial(_rwkv_chunk_kernel, n_cast=len(casts), cps=cps),
        out_shape=[jax.ShapeDtypeStruct((t, RW_WIDTH), BF16)] + [jax.ShapeDtypeStruct(w.shape, BF16) for w in casts],
        grid=(batch, nc),
        in_specs=([blk] * 9 + [pl.BlockSpec((cps, 1, RW_WIDTH), lambda b, c: (b * nc + c, 0, 0)), rowspec, rowspec]
                  + cast_specs),
        out_specs=[blk] + cast_specs,
        scratch_shapes=[pltpu.VMEM((RW_HEADS // 2, PAIR, PAIR), F32)],
        compiler_params=_cparams("arbitrary", "arbitrary"),
        name="rwkv_chunks",
    )(at, rt, bt, kt, bg, kg, vb, bonus, g, gam, lnx_g.reshape(1, RW_WIDTH), lnx_b.reshape(1, RW_WIDTH), *casts)
    return outs[0], outs[1:]


def _cast_kernel(x_ref, o_ref):
    o_ref[...] = x_ref[...].astype(o_ref.dtype)


def _cast_bf16(w, rows):
    n, m = w.shape
    return pl.pallas_call(
        _cast_kernel,
        out_shape=jax.ShapeDtypeStruct((n, m), BF16),
        grid=(n // rows,),
        in_specs=[pl.BlockSpec((rows, m), lambda i: (i, 0))],
        out_specs=pl.BlockSpec((rows, m), lambda i: (i, 0)),
        compiler_params=_cparams("parallel"),
        name="cast_bf16",
    )(w)


def _merge_kernel(h_ref, ya_ref, yb_ref, wga_ref, wgb_ref, pa_ref, pb_ref, o_ref):
    h = h_ref[...]
    ga = jax.nn.sigmoid(_dot(h, wga_ref[...]))
    gb = jax.nn.sigmoid(_dot(h, wgb_ref[...]))
    o_ref[...] = (ga * _dot(ya_ref[...], pa_ref[...]) + gb * _dot(yb_ref[...], pb_ref[...])).astype(o_ref.dtype)


def _merge(h, ya, yb, wga, wgb, pa, pb, tm, tn):
    t, d = h.shape
    n = wga.shape[1]
    ka, kb = ya.shape[1], yb.shape[1]
    return pl.pallas_call(
        _merge_kernel,
        out_shape=jax.ShapeDtypeStruct((t, n), BF16),
        grid=(t // tm, n // tn),
        in_specs=[pl.BlockSpec((tm, d), lambda i, j: (i, 0)),
                  pl.BlockSpec((tm, ka), lambda i, j: (i, 0)),
                  pl.BlockSpec((tm, kb), lambda i, j: (i, 0)),
                  pl.BlockSpec((d, tn), lambda i, j: (0, j)),
                  pl.BlockSpec((d, tn), lambda i, j: (0, j)),
                  pl.BlockSpec((ka, tn), lambda i, j: (0, j)),
                  pl.BlockSpec((kb, tn), lambda i, j: (0, j))],
        out_specs=pl.BlockSpec((tm, tn), lambda i, j: (i, j)),
        compiler_params=_cparams("parallel", "parallel"),
        name="gated_merge",
    )(h, ya, yb, wga, wgb, pa, pb)


def _first_lane_where(cond, lane):
    return jnp.min(jnp.where(cond, lane, LANES), axis=-1, keepdims=True)


def _outproj_kernel(m_ref, x_ref, wo_ref, g2_ref, wrh_ref, wrl_ref, rb_ref, x1_ref, h2_ref, route_ref):
    x1 = x_ref[...] + _dot(m_ref[...], wo_ref[...])
    x1_ref[...] = x1
    ms = jnp.mean(x1 * x1, axis=-1, keepdims=True)
    h2 = x1 * lax.rsqrt(ms + NORM_EPS) * g2_ref[...]
    h2_ref[...] = h2
    hh = h2.astype(BF16)
    hl = (h2 - hh.astype(F32)).astype(BF16)
    wrh = wrh_ref[...]
    lg = _dot(hh, wrh) + _dot(hl, wrh) + _dot(hh, wrl_ref[...]) + rb_ref[...]

    lane = lax.broadcasted_iota(jnp.int32, lg.shape, 1)
    neg = -jnp.inf
    is_g = lane < N_GROUPS
    mg = jnp.max(jnp.where(is_g, lg, neg), axis=-1, keepdims=True)
    eg = jnp.where(is_g, jnp.exp(lg - mg), 0.0)
    pg = eg / jnp.sum(eg, axis=-1, keepdims=True)
    p_g_top = jnp.max(pg, axis=-1, keepdims=True)
    g_idx = _first_lane_where(is_g & (pg == p_g_top), lane)
    lo = N_GROUPS + g_idx * EXPERTS_PER_GROUP
    sel = (lane >= lo) & (lane < lo + EXPERTS_PER_GROUP)
    me = jnp.max(jnp.where(sel, lg, neg), axis=-1, keepdims=True)
    ee = jnp.where(sel, jnp.exp(lg - me), 0.0)
    pe = ee / jnp.sum(ee, axis=-1, keepdims=True)
    pe = jnp.where(sel, pe, -1.0)
    v1 = jnp.max(pe, axis=-1, keepdims=True)
    i1 = _first_lane_where(pe == v1, lane)
    pe2 = jnp.where(lane == i1, -1.0, pe)
    v2 = jnp.max(pe2, axis=-1, keepdims=True)
    i2 = _first_lane_where(pe2 == v2, lane)
    den = v1 + v2
    route = jnp.where(lane == 0, p_g_top * v1 / den,
                      jnp.where(lane == 1, p_g_top * v2 / den,
                                jnp.where(lane == 2, (i1 - N_GROUPS).astype(F32),
                                          jnp.where(lane == 3, (i2 - N_GROUPS).astype(F32), 0.0))))
    route_ref[...] = route


def _outproj(merged, x, wo, g2, wr_hi, wr_lo, rbias, tm):
    t, d = x.shape
    nr = wr_hi.shape[1]
    return pl.pallas_call(
        _outproj_kernel,
        out_shape=[jax.ShapeDtypeStruct((t, d), F32), jax.ShapeDtypeStruct((t, d), F32),
                   jax.ShapeDtypeStruct((t, nr), F32)],
        grid=(t // tm,),
        in_specs=[pl.BlockSpec((tm, d), lambda i: (i, 0)),
                  pl.BlockSpec((tm, d), lambda i: (i, 0)),
                  pl.BlockSpec((d, d), lambda i: (0, 0)),
                  pl.BlockSpec((1, d), lambda i: (0, 0)),
                  pl.BlockSpec((d, nr), lambda i: (0, 0)),
                  pl.BlockSpec((d, nr), lambda i: (0, 0)),
                  pl.BlockSpec((1, nr), lambda i: (0, 0))],
        out_specs=[pl.BlockSpec((tm, d), lambda i: (i, 0)),
                   pl.BlockSpec((tm, d), lambda i: (i, 0)),
                   pl.BlockSpec((tm, nr), lambda i: (i, 0))],
        compiler_params=_cparams("parallel"),
        name="outproj_norm_router",
    )(merged, x, wo, g2.reshape(1, d), wr_hi, wr_lo, rbias)


IDX_SLOTS = 3


def _moe_kernel(be_ref, nused_ref, nvalid_ref, tok_hbm, dst_hbm, h_hbm, wg_ref, wu_ref, wd_ref, y_hbm,
                xbuf, obuf, tok_s, dst_s, idx_sem, g_sem, s_sem, *, rows):
    b = pl.program_id(0)
    n_used = nused_ref[0]
    last = n_used - 1

    def idx_copies(blk, sl):
        return (pltpu.make_async_copy(tok_hbm.at[blk], tok_s.at[sl], idx_sem.at[sl, 0]),
                pltpu.make_async_copy(dst_hbm.at[blk], dst_s.at[sl], idx_sem.at[sl, 1]))

    def gather_row(r, isl, xsl):
        return pltpu.make_async_copy(h_hbm.at[tok_s[isl, r]], xbuf.at[xsl, r], g_sem.at[xsl])

    def gather_all(xsl):
        return pltpu.make_async_copy(h_hbm.at[pl.ds(0, rows)], xbuf.at[xsl], g_sem.at[xsl])

    def scatter_row(r, isl, osl):
        return pltpu.make_async_copy(obuf.at[osl, r], y_hbm.at[dst_s[isl, r]], s_sem.at[osl])

    def wait_scatter(blk, osl):
        n = nvalid_ref[blk]

        @pl.when(n == rows)
        def _():
            pltpu.make_async_copy(obuf.at[osl], y_hbm.at[pl.ds(0, rows)], s_sem.at[osl]).wait()

        @pl.when(n < rows)
        def _():
            def one(r, c):
                pltpu.make_async_copy(obuf.at[osl, 0], y_hbm.at[0], s_sem.at[osl]).wait()
                return c

            lax.fori_loop(0, n, one, 0)

    @pl.when(b == 0)
    def _():
        for cp in idx_copies(0, 0):
            cp.start()
        for cp in idx_copies(0, 0):
            cp.wait()
        for cp in idx_copies(jnp.minimum(1, last), 1):
            cp.start()

        def issue(r, c):
            gather_row(r, 0, 0).start()
            return c

        lax.fori_loop(0, rows, issue, 0, unroll=8)

    @pl.when((b > 1) & (b < n_used))
    def _():
        wait_scatter(b - 2, b % 2)

    for par in range(2):
        @pl.when((b < n_used) & (b % 2 == par))
        def _(par=par):
            nxt_i, ld_i = (b + 1) % IDX_SLOTS, (b + 2) % IDX_SLOTS
            for cp in idx_copies(0, nxt_i):
                cp.wait()
            for r in range(rows):
                gather_row(r, nxt_i, 1 - par).start()
            for cp in idx_copies(jnp.minimum(b + 2, last), ld_i):
                cp.start()

            gather_all(par).wait()
            x = xbuf[par].astype(BF16)
            gate = _dot(x, wg_ref[...])
            up = _dot(x, wu_ref[...])
            mid = (gate * jax.nn.sigmoid(gate) * up).astype(BF16)
            obuf[par] = _dot(mid, wd_ref[...])

        @pl.when((b < n_used) & (b % 2 == par) & (nvalid_ref[b] == rows))
        def _(par=par):
            for r in range(rows):
                scatter_row(r, b % IDX_SLOTS, par).start()

    @pl.when((b < n_used) & (nvalid_ref[b] < rows))
    def _():
        def issue(r, c):
            scatter_row(r, b % IDX_SLOTS, b % 2).start()
            return c

        lax.fori_loop(0, nvalid_ref[b], issue, 0)

    @pl.when(b == last)
    def _():
        @pl.when(b > 0)
        def _():
            wait_scatter(b - 1, (b - 1) % 2)

        wait_scatter(b, b % 2)
        gather_all((b + 1) % 2).wait()
        for cp in idx_copies(0, (b + 2) % IDX_SLOTS):
            cp.wait()


def _moe(h2, row_tok, row_dst, block_e, n_used, nvalid, wg, wu, wd, rows):
    t, d = h2.shape
    n_blocks = row_tok.shape[0]
    a = t * TOP_K
    f = wg.shape[2]
    kern = functools.partial(_moe_kernel, rows=rows)
    grid_spec = pltpu.PrefetchScalarGridSpec(
        num_scalar_prefetch=3,
        grid=(n_blocks,),
        in_specs=[pl.BlockSpec(memory_space=pl.ANY),
                  pl.BlockSpec(memory_space=pl.ANY),
                  pl.BlockSpec(memory_space=pl.ANY),
                  pl.BlockSpec((None, d, f), lambda b, be, nu, nv: (be[b], 0, 0)),
                  pl.BlockSpec((None, d, f), lambda b, be, nu, nv: (be[b], 0, 0)),
                  pl.BlockSpec((None, f, d), lambda b, be, nu, nv: (be[b], 0, 0))],
        out_specs=pl.BlockSpec(memory_space=pl.ANY),
        scratch_shapes=[pltpu.VMEM((2, rows, d), F32),
                        pltpu.VMEM((2, rows, d), F32),
                        pltpu.SMEM((IDX_SLOTS, rows), jnp.int32),
                        pltpu.SMEM((IDX_SLOTS, rows), jnp.int32),
                        pltpu.SemaphoreType.DMA((IDX_SLOTS, 2)),
                        pltpu.SemaphoreType.DMA((2,)),
                        pltpu.SemaphoreType.DMA((2,))],
    )
    return pl.pallas_call(
        kern,
        out_shape=jax.ShapeDtypeStruct((a, d), F32),
        grid_spec=grid_spec,
        compiler_params=_cparams("arbitrary"),
        name="moe_experts",
    )(block_e, n_used, nvalid, row_tok, row_dst, h2, wg, wu, wd)


def _combine_kernel(x1_ref, y0_ref, y1_ref, route_ref, o_ref):
    gts = route_ref[...]
    o_ref[...] = x1_ref[...] + gts[:, 0:1] * y0_ref[...] + gts[:, 1:2] * y1_ref[...]


def _combine(x1, y, route, tm):
    t, d = x1.shape
    nb = t // tm
    return pl.pallas_call(
        _combine_kernel,
        out_shape=jax.ShapeDtypeStruct((t, d), F32),
        grid=(nb,),
        in_specs=[pl.BlockSpec((tm, d), lambda i: (i, 0)),
                  pl.BlockSpec((tm, d), lambda i: (i, 0)),
                  pl.BlockSpec((tm, d), lambda i: (nb + i, 0)),
                  pl.BlockSpec((tm, LANES), lambda i: (i, 0))],
        out_specs=pl.BlockSpec((tm, d), lambda i: (i, 0)),
        compiler_params=_cparams("parallel"),
        name="moe_combine",
    )(x1, y, y, route)


def _routing_tables(route, rows):
    t = route.shape[0]
    a = t * TOP_K
    expert = route[:, 2:2 + TOP_K].astype(jnp.int32)
    flat_e = expert.reshape(a)
    ids = jnp.arange(a, dtype=jnp.int32)
    order = jnp.sort(flat_e * a + ids) % a
    counts = jnp.sum((flat_e[:, None] == jnp.arange(N_EXPERTS, dtype=jnp.int32)[None, :]).astype(jnp.int32), axis=0)
    padded = (counts + rows - 1) // rows * rows
    start = jnp.cumsum(counts) - counts
    pend = jnp.cumsum(padded)
    pstart = pend - padded
    n_rows = a + N_EXPERTS * rows
    n_blocks = n_rows // rows
    blk_row0 = jnp.arange(n_blocks, dtype=jnp.int32) * rows
    block_e = jnp.minimum(jnp.sum((pend[None, :] <= blk_row0[:, None]).astype(jnp.int32), axis=1), N_EXPERTS - 1)
    off0 = blk_row0 - pstart[block_e]
    cnt_b = counts[block_e]
    off = off0[:, None] + jnp.arange(rows, dtype=jnp.int32)[None, :]
    valid = (off >= 0) & (off < cnt_b[:, None])
    src = order[jnp.clip(start[block_e][:, None] + off, 0, a - 1)]
    row_tok = jnp.where(valid, src // TOP_K, 0)
    row_dst = jnp.where(valid, (src % TOP_K) * t + src // TOP_K, 0)
    n_used = (pend[-1] // rows).astype(jnp.int32).reshape(1)
    nvalid = jnp.clip(cnt_b - off0, 0, rows).astype(jnp.int32)
    return row_tok, row_dst, block_e.astype(jnp.int32), n_used, nvalid


def _tiles(t, seq):
    return dict(
        norm_tm=min(512, t),
        proj_tm=min(1024, seq), proj_tn=512,
        attn_tq=min(256, seq), attn_tk=min(1024, seq), attn_heads=4,
        prep_tm=min(256, seq), rwkv_cps=2,
        merge_tm=min(1024, t), merge_tn=512,
        out_tm=min(512, t),
        moe_rows=256,
        comb_tm=min(512, t),
    )


def _pad_rows(w, n):
    return jnp.pad(w, ((0, n - w.shape[0]), (0, 0)))


def _pad_cols(w, n):
    return jnp.pad(w, ((0, 0), (0, n - w.shape[1])))


def kernel(x, norm1_g, w_in, q_norm_g, k_norm_g, lam_q1, lam_k1, lam_q2, lam_k2, subln_g, shift_mu, w0, w_up, a0, a_up, g_up, k_k, k_a, r_k, lnx_g, lnx_b, proj_a, proj_b, w_out, norm2_g, router_g, router_g_b, router_e, router_e_b, w_gate_e, w_up_e, w_down_e):
    batch, seq, d = x.shape
    t = batch * seq
    depth = norm1_g.shape[0]
    tl = _tiles(t, seq)
    qkw = DA_HEADS * 2 * DA_HEAD_DIM
    vw = DA_HEADS * DA_V_DIM
    c_q, c_k, c_v = 0, qkw, 2 * qkw
    c_rw = c_v + vw
    c_dw = c_rw + 3 * RW_WIDTH
    c_da = c_dw + DECAY_LORA
    c_dg = c_da + AAA_LORA
    c_ga = c_dg + GATE_LORA
    c_gb = c_ga + d
    cos_t, sinm_t, sinp_t = _rope_tables(seq)
    xf = x.reshape(t, d)

    for l in range(depth):
        lam_init = 0.8 - 0.6 * math.exp(-0.3 * l)
        wl = w_in[l]
        w_qk = wl[:, c_q:c_v].astype(BF16)
        w_v = wl[:, c_v:c_rw].astype(BF16)
        w_rkv = wl[:, c_rw:c_dw].astype(BF16)
        w_lo = jnp.concatenate([_pad_cols(wl[:, c_dw:c_da], LANES), _pad_cols(wl[:, c_da:c_dg], LANES),
                                wl[:, c_dg:c_ga]], axis=1).astype(BF16)
        w_ga = wl[:, c_ga:c_gb].astype(BF16)
        w_gb = wl[:, c_gb:].astype(BF16)
        mu = shift_mu[l]
        o_dw = 3 * RW_WIDTH
        mu_rkv = mu[:o_dw].reshape(1, -1)
        mu_lo = jnp.concatenate([jnp.pad(mu[o_dw:o_dw + DECAY_LORA], (0, LANES - DECAY_LORA)),
                                 jnp.pad(mu[o_dw + DECAY_LORA:o_dw + DECAY_LORA + AAA_LORA], (0, LANES - AAA_LORA)),
                                 mu[o_dw + DECAY_LORA + AAA_LORA:]]).reshape(1, -1)
        gain_row = jnp.concatenate([jnp.tile(q_norm_g[l], 2 * DA_HEADS) * (DA_HEAD_DIM ** -0.5 * LOG2_E),
                                    jnp.tile(k_norm_g[l], 2 * DA_HEADS)]).reshape(1, 2 * qkw)
        lam_params = jnp.stack([lam_q1[l], lam_k1[l], lam_q2[l], lam_k2[l]])
        zrow = jnp.zeros((RW_WIDTH,), F32)
        rw_params = jnp.stack([w0[l], a0[l], k_k[l], k_a[l], r_k[l].reshape(-1), zrow, zrow, zrow])

        h = _rmsnorm(xf, norm1_g[l], NORM_EPS, tl["norm_tm"])
        qk = _qk_proj(h, w_qk, gain_row, cos_t, sinm_t, sinp_t, seq, tl["proj_tm"], tl["proj_tn"])
        v = _matmul(h, w_v, BF16, tl["proj_tm"], tl["proj_tn"], "v_proj")
        ya = _diff_attention(qk, v, lam_params, subln_g[l], batch, seq, lam_init, tl["attn_tq"], tl["attn_tk"],
                             tl["attn_heads"])

        prep = _rwkv_prep(h, w_rkv, w_lo, mu_rkv, mu_lo, rw_params, _pad_rows(w_up[l], LANES).astype(BF16),
                          _pad_rows(a_up[l], LANES).astype(BF16), g_up[l].astype(BF16), seq, tl["prep_tm"])
        ew = (w_gate_e[l], w_up_e[l], w_down_e[l])
        ew2d = tuple(w.reshape(-1, w.shape[-1]) for w in ew)
        cps = tl["rwkv_cps"]
        steps = batch * (seq // (CHUNK * cps))
        ride = all(w.shape[0] % (steps * 2 * SUBLANES) == 0 and w.size * 4 // steps <= CAST_BLOCK_BYTES for w in ew2d)
        yb, ew_bf16 = _rwkv_chunks(prep, lnx_g[l], lnx_b[l], batch, seq, cps, ew2d if ride else ())
        if not ride:
            ew_bf16 = tuple(_cast_bf16(w, CAST_BLOCK_BYTES // (4 * w.shape[1])) for w in ew2d)
        wg_e, wu_e, wd_e = (c.reshape(w.shape) for c, w in zip(ew_bf16, ew))

        merged = _merge(h, ya, yb, w_ga, w_gb, proj_a[l].astype(BF16), proj_b[l].astype(BF16),
                        tl["merge_tm"], tl["merge_tn"])
        wr = _pad_cols(jnp.concatenate([router_g[l], router_e[l]], axis=1), LANES)
        wr_hi = wr.astype(BF16)
        wr_lo = (wr - wr_hi.astype(F32)).astype(BF16)
        rbias = jnp.pad(jnp.concatenate([router_g_b[l], router_e_b[l]]), (0, LANES - N_GROUPS - N_EXPERTS))
        x1, h2, route = _outproj(merged, xf, w_out[l].astype(BF16), norm2_g[l], wr_hi, wr_lo,
                                 rbias.reshape(1, LANES), tl["out_tm"])

        rows = tl["moe_rows"]
        row_tok, row_dst, block_e, n_used, nvalid = _routing_tables(route, rows)
        y = _moe(h2, row_tok, row_dst, block_e, n_used, nvalid, wg_e, wu_e, wd_e, rows)
        xf = _combine(x1, y, route, tl["comb_tm"])
    return xf.reshape(batch, seq, d)
```

```python
import functools
import math

import jax
import jax.numpy as jnp
from jax import lax
from jax.experimental import pallas as pl
from jax.experimental.pallas import tpu as pltpu

DA_HEADS = 8
DA_HEAD_DIM = 64
DA_V_DIM = 2 * DA_HEAD_DIM
ROT_DIM = DA_HEAD_DIM // 4
ROPE_THETA = 500000.0
SUBLN_EPS = 1e-5
RW_HEADS = 16
RW_HEAD_DIM = 64
RW_WIDTH = RW_HEADS * RW_HEAD_DIM
DECAY_LORA = 96
AAA_LORA = 96
GATE_LORA = 256
GN_EPS = 64e-5
N_GROUPS = 4
EXPERTS_PER_GROUP = 8
N_EXPERTS = N_GROUPS * EXPERTS_PER_GROUP
TOP_K = 2
NORM_EPS = 1e-6

LANES = 128
SUBLANES = 8
VMEM_LIMIT_BYTES = 56 * 1024 * 1024

CHUNK = 64
CAST_BLOCK_BYTES = 2 * 1024 * 1024

F32 = jnp.float32
BF16 = jnp.bfloat16

LOG2_E = math.log2(math.e)

NT_DIMS = (((1,), (1,)), ((), ()))
TN_DIMS = (((0,), (0,)), ((), ()))


def _cparams(*sem):
    return pltpu.CompilerParams(dimension_semantics=tuple(sem), vmem_limit_bytes=VMEM_LIMIT_BYTES)


def _dot(a, b, dims=None, precision=None):
    if dims is None:
        return jnp.dot(a, b, preferred_element_type=F32, precision=precision)
    return lax.dot_general(a, b, dims, preferred_element_type=F32, precision=precision)


def _split3(x):
    h = x.astype(BF16)
    r = x - h.astype(F32)
    m = r.astype(BF16)
    l = (r - m.astype(F32)).astype(BF16)
    return h, m, l


def _group_sum64(x, bd, terms):
    parts = _split3(x)[:terms]
    outs = []
    for s in range(x.shape[1] // LANES):
        sl = slice(s * LANES, (s + 1) * LANES)
        acc = _dot(parts[0][:, sl], bd)
        for part in parts[1:]:
            acc = acc + _dot(part[:, sl], bd)
        outs.append(acc)
    return outs[0] if len(outs) == 1 else jnp.concatenate(outs, axis=1)


def _block_diag_ones(n, blk, dtype=BF16):
    r = lax.broadcasted_iota(jnp.int32, (n, n), 0) // blk
    c = lax.broadcasted_iota(jnp.int32, (n, n), 1) // blk
    return jnp.where(r == c, 1.0, 0.0).astype(dtype)


def _rmsnorm_kernel(x_ref, g_ref, o_ref, *, eps):
    x = x_ref[...]
    ms = jnp.mean(x * x, axis=-1, keepdims=True)
    o_ref[...] = (x * lax.rsqrt(ms + eps) * g_ref[...]).astype(o_ref.dtype)


def _rmsnorm(x, g, eps, tm):
    t, d = x.shape
    return pl.pallas_call(
        functools.partial(_rmsnorm_kernel, eps=eps),
        out_shape=jax.ShapeDtypeStruct((t, d), BF16),
        grid=(t // tm,),
        in_specs=[pl.BlockSpec((tm, d), lambda i: (i, 0)),
                  pl.BlockSpec((1, d), lambda i: (0, 0))],
        out_specs=pl.BlockSpec((tm, d), lambda i: (i, 0)),
        compiler_params=_cparams("parallel"),
        name="rmsnorm",
    )(x, g.reshape(1, d))


def _matmul_kernel(a_ref, w_ref, o_ref):
    o_ref[...] = _dot(a_ref[...], w_ref[...]).astype(o_ref.dtype)


def _matmul(a, w, out_dtype, tm, tn, name):
    t, k = a.shape
    n = w.shape[1]
    return pl.pallas_call(
        _matmul_kernel,
        out_shape=jax.ShapeDtypeStruct((t, n), out_dtype),
        grid=(t // tm, n // tn),
        in_specs=[pl.BlockSpec((tm, k), lambda i, j: (i, 0)),
                  pl.BlockSpec((k, tn), lambda i, j: (0, j))],
        out_specs=pl.BlockSpec((tm, tn), lambda i, j: (i, j)),
        compiler_params=_cparams("parallel", "parallel"),
        name=name,
    )(a, w)


def _qk_proj_kernel(a_ref, w_ref, gain_ref, cos_ref, sinm_ref, sinp_ref, o_ref, *, tn):
    acc = _dot(a_ref[...], w_ref[...])
    bd = _block_diag_ones(LANES, DA_HEAD_DIM)
    ms = _group_sum64(acc * acc, bd, 1) * (1.0 / DA_HEAD_DIM)
    xn = acc * lax.rsqrt(ms + NORM_EPS) * gain_ref[...]
    reps = tn // LANES
    cos = jnp.tile(cos_ref[...], (1, reps))
    sinm = jnp.tile(sinm_ref[...], (1, reps))
    sinp = jnp.tile(sinp_ref[...], (1, reps))
    half = ROT_DIM // 2
    hi = pltpu.roll(xn, tn - half, 1)
    lo = pltpu.roll(xn, half, 1)
    o_ref[...] = (xn * cos + hi * sinm + lo * sinp).astype(o_ref.dtype)


def _qk_proj(h, w_qk, gain_row, cos_t, sinm_t, sinp_t, seq, tm, tn):
    t, k = h.shape
    n = w_qk.shape[1]
    nseq = seq // tm
    return pl.pallas_call(
        functools.partial(_qk_proj_kernel, tn=tn),
        out_shape=jax.ShapeDtypeStruct((t, n), BF16),
        grid=(t // tm, n // tn),
        in_specs=[pl.BlockSpec((tm, k), lambda i, j: (i, 0)),
                  pl.BlockSpec((k, tn), lambda i, j: (0, j)),
                  pl.BlockSpec((1, tn), lambda i, j: (0, j)),
                  pl.BlockSpec((tm, LANES), lambda i, j: (i % nseq, 0)),
                  pl.BlockSpec((tm, LANES), lambda i, j: (i % nseq, 0)),
                  pl.BlockSpec((tm, LANES), lambda i, j: (i % nseq, 0))],
        out_specs=pl.BlockSpec((tm, tn), lambda i, j: (i, j)),
        compiler_params=_cparams("parallel", "parallel"),
        name="qk_proj",
    )(h, w_qk, gain_row, cos_t, sinm_t, sinp_t)


def _rope_tables(seq):
    half = ROT_DIM // 2
    inv_freq = ROPE_THETA ** (-jnp.arange(0, ROT_DIM, 2, dtype=F32) / ROT_DIM)
    ang = jnp.arange(seq, dtype=F32)[:, None] * inv_freq[None, :]
    cos, sin = jnp.cos(ang), jnp.sin(ang)
    ones = jnp.ones((seq, DA_HEAD_DIM - ROT_DIM), F32)
    zeros = jnp.zeros((seq, DA_HEAD_DIM - ROT_DIM), F32)
    zh = jnp.zeros((seq, half), F32)
    cos64 = jnp.concatenate([cos, cos, ones], axis=1)
    sinm64 = jnp.concatenate([-sin, zh, zeros], axis=1)
    sinp64 = jnp.concatenate([zh, sin, zeros], axis=1)
    return tuple(jnp.concatenate([a, a], axis=1) for a in (cos64, sinm64, sinp64))


def _diff_attn_kernel(lam_ref, q_ref, k_ref, v_ref, g_ref, o_ref, m_ref, l_ref, a_ref, *, tq, tk, lam_init, nh):
    i = pl.program_id(2)
    lane = lax.broadcasted_iota(jnp.int32, (1, DA_V_DIM), 1)
    qs = []
    for h in range(nh):
        q = q_ref[:, h * DA_V_DIM:(h + 1) * DA_V_DIM]
        zero = jnp.zeros_like(q)
        qs.append(jnp.where(lane < DA_HEAD_DIM, q, zero))
        qs.append(jnp.where(lane >= DA_HEAD_DIM, q, zero))
    nc = 2 * nh
    qb = [jnp.concatenate([qs[2 * h], qs[2 * h + 1]], axis=0) for h in range(nh)]
    m_ref[...] = jnp.full(m_ref.shape, -jnp.inf, F32)
    l_ref[...] = jnp.zeros(l_ref.shape, F32)
    a_ref[...] = jnp.zeros(a_ref.shape, F32)

    def step(off, width, mask):
        reps = width // LANES
        ks = [k_ref[pl.ds(off, width), h * DA_V_DIM:(h + 1) * DA_V_DIM] for h in range(nh)]
        vs = [v_ref[pl.ds(off, width), h * DA_V_DIM:(h + 1) * DA_V_DIM] for h in range(nh)]
        sb = [_dot(qb[h], ks[h], NT_DIMS) for h in range(nh)]
        s = [sb[c // 2][(c % 2) * tq:(c % 2 + 1) * tq] for c in range(nc)]
        ps = []
        for c in range(nc):
            sc = s[c] if mask is None else jnp.where(mask, s[c], -jnp.inf)
            m_old = m_ref[c]
            m_new = jnp.maximum(m_old, jnp.broadcast_to(jnp.max(sc, axis=-1, keepdims=True), m_old.shape))
            alpha = jnp.exp2(m_old - m_new)
            p = jnp.exp2(sc - jnp.tile(m_new, (1, reps)))
            part = p[:, :LANES]
            for kk in range(1, reps):
                part = part + p[:, kk * LANES:(kk + 1) * LANES]
            l_ref[c] = alpha * l_ref[c] + part
            a_ref[c] = alpha * a_ref[c]
            m_ref[c] = m_new
            ps.append(p.astype(BF16))
        for h in range(nh):
            pv = _dot(jnp.concatenate([ps[2 * h], ps[2 * h + 1]], axis=0), vs[h])
            a_ref[2 * h] = a_ref[2 * h] + pv[:tq]
            a_ref[2 * h + 1] = a_ref[2 * h + 1] + pv[tq:]

    n_full = (i * tq) // tk

    def full_body(j, c):
        step(pl.multiple_of(j * tk, tk), tk, None)
        return c

    lax.fori_loop(0, n_full, full_body, 0)

    sub = tk // tq
    rem = i % sub
    base = pl.multiple_of(n_full * tk, tk)
    for v in range(sub):
        @pl.when(rem == v)
        def _(v=v):
            width = (v + 1) * tq
            row = v * tq + lax.broadcasted_iota(jnp.int32, (tq, width), 0)
            col = lax.broadcasted_iota(jnp.int32, (tq, width), 1)
            step(base, width, col <= row)

    lq1, lk1, lq2, lk2 = (lam_ref[r:r + 1, :] for r in range(4))
    lam = (jnp.exp(jnp.sum(lq1 * lk1, axis=-1, keepdims=True))
           - jnp.exp(jnp.sum(lq2 * lk2, axis=-1, keepdims=True)) + lam_init)
    for h in range(nh):
        l1 = jnp.sum(l_ref[2 * h], axis=-1, keepdims=True)
        l2 = jnp.sum(l_ref[2 * h + 1], axis=-1, keepdims=True)
        o = a_ref[2 * h] / l1 - lam * (a_ref[2 * h + 1] / l2)
        ms = jnp.mean(o * o, axis=-1, keepdims=True)
        o = o * lax.rsqrt(ms + SUBLN_EPS) * (g_ref[...] * (1.0 - lam_init))
        o_ref[:, h * DA_V_DIM:(h + 1) * DA_V_DIM] = o.astype(o_ref.dtype)


def _diff_attention(qk, v, lam_params, subln_g, batch, seq, lam_init, tq, tk, nh):
    assert tk % tq == 0 and seq % tk == 0 and DA_HEADS % nh == 0
    t = qk.shape[0]
    nq = seq // tq
    w = nh * DA_V_DIM
    hg = DA_HEADS // nh
    kern = functools.partial(_diff_attn_kernel, tq=tq, tk=tk, lam_init=lam_init, nh=nh)
    return pl.pallas_call(
        kern,
        out_shape=jax.ShapeDtypeStruct((t, DA_HEADS * DA_V_DIM), BF16),
        grid=(batch, hg, nq),
        in_specs=[pl.BlockSpec((4, DA_HEAD_DIM), lambda b, h, i: (0, 0)),
                  pl.BlockSpec((tq, w), lambda b, h, i: (b * nq + i, h)),
                  pl.BlockSpec((seq, w), lambda b, h, i: (b, hg + h)),
                  pl.BlockSpec((seq, w), lambda b, h, i: (b, h)),
                  pl.BlockSpec((1, DA_V_DIM), lambda b, h, i: (0, 0))],
        out_specs=pl.BlockSpec((tq, w), lambda b, h, i: (b * nq + i, h)),
        scratch_shapes=[pltpu.VMEM((2 * nh, tq, LANES), F32), pltpu.VMEM((2 * nh, tq, LANES), F32),
                        pltpu.VMEM((2 * nh, tq, DA_V_DIM), F32)],
        compiler_params=_cparams("parallel", "parallel", "parallel"),
        name="diff_attention",
    )(lam_params, qk, qk, v, subln_g.reshape(1, DA_V_DIM))


DECAY_SCALE = math.exp(-0.5)


def _rwkv_prep_kernel(h_ref, w_ref, wlo_ref, mu_ref, mulo_ref, par_ref, wup_ref, aup_ref, gup_ref,
                      at_ref, rt_ref, bt_ref, kt_ref, bg_ref, kg_ref, vb_ref, bonus_ref, g_ref, gam_ref,
                      last_ref, lastlo_ref, *, tm, nseq):
    i = pl.program_id(0)

    @pl.when(i == 0)
    def _():
        last_ref[...] = jnp.zeros(last_ref.shape, F32)
        lastlo_ref[...] = jnp.zeros(lastlo_ref.shape, F32)

    seq_start = (i % nseq) == 0
    row0 = lax.broadcasted_iota(jnp.int32, (tm, 1), 0) == 0
    h = h_ref[...]

    def shifted_proj(w_r, mu_r, carry_ref):
        x = _dot(h, w_r[...])
        last = jnp.where(seq_start, 0.0, carry_ref[...])
        prev = jnp.where(row0, last, pltpu.roll(x, 1, 0))
        carry_ref[...] = x[tm - 1:tm, :]
        return x + (prev - x) * mu_r[...]

    z = shifted_proj(w_ref, mu_ref, last_ref)
    lo = shifted_proj(wlo_ref, mulo_ref, lastlo_ref)
    r, k, v = z[:, :RW_WIDTH], z[:, RW_WIDTH:2 * RW_WIDTH], z[:, 2 * RW_WIDTH:]
    dw, da, dg = lo[:, :LANES], lo[:, LANES:2 * LANES], lo[:, 2 * LANES:]
    w0, a0, k_k, k_a, r_k = (par_ref[j:j + 1, :] for j in range(5))

    u = w0 + _dot(jnp.tanh(dw).astype(BF16), wup_ref[...])
    ld = -DECAY_SCALE * jax.nn.sigmoid(u)
    a = jax.nn.sigmoid(a0 + _dot(da.astype(BF16), aup_ref[...]))
    g = _dot(jax.nn.sigmoid(dg).astype(BF16), gup_ref[...])

    bd = _block_diag_ones(LANES, RW_HEAD_DIM)
    kk = k * k_k
    kk = kk / jnp.maximum(jnp.sqrt(_group_sum64(kk * kk, bd, 1)), 1e-12)
    k2 = k * (1.0 + (a - 1.0) * k_a)
    bonus = _group_sum64(r * k2 * r_k, bd, 1) * v

    t_i = lax.broadcasted_iota(jnp.int32, (tm, tm), 0)
    s_i = lax.broadcasted_iota(jnp.int32, (tm, tm), 1)
    same = (t_i // CHUNK) == (s_i // CHUNK)
    tri = jnp.where(same & (s_i <= t_i), 1.0, 0.0).astype(BF16)
    rest = jnp.where(same & (s_i > t_i), 1.0, 0.0).astype(BF16)
    c_i = lax.broadcasted_iota(jnp.int32, (tm // CHUNK, tm), 0)
    cs_i = lax.broadcasted_iota(jnp.int32, (tm // CHUNK, tm), 1)
    whole = jnp.where(cs_i // CHUNK == c_i, 1.0, 0.0).astype(BF16)

    ldh, ldm = _split2(ld)

    def sel(m):
        return _dot(m, ldh) + _dot(m, ldm)

    cum = sel(tri)
    e_neg = jnp.exp(-cum)
    e_rem = jnp.exp(sel(rest))
    b = kk * a
    at_ref[...] = (-kk * jnp.exp(cum - ld)).astype(BF16)
    rt_ref[...] = (r * jnp.exp(cum)).astype(BF16)
    bt_ref[...] = (b * e_neg).astype(BF16)
    kt_ref[...] = (k2 * e_neg).astype(BF16)
    bg_ref[...] = (b * e_rem).astype(BF16)
    kg_ref[...] = (k2 * e_rem).astype(BF16)
    vb_ref[...] = v.astype(BF16)
    bonus_ref[...] = bonus.astype(BF16)
    g_ref[...] = g.astype(BF16)
    gam_ref[0] = jnp.exp(sel(whole))


def _rwkv_prep(h, w_rkv, w_lo, mu_rkv, mu_lo, params, wup, aup, gup, seq, tm):
    t, d = h.shape
    n, nlo = w_rkv.shape[1], w_lo.shape[1]
    nseq = seq // tm
    whole = lambda shape: pl.BlockSpec(shape, lambda i: (0, 0), pipeline_mode=pl.Buffered(1))
    in_specs = [pl.BlockSpec((tm, d), lambda i: (i, 0)), whole((d, n)), whole((d, nlo)), whole((1, n)), whole((1, nlo)),
                whole((SUBLANES, RW_WIDTH)), whole((LANES, RW_WIDTH)), whole((LANES, RW_WIDTH)),
                whole((2 * LANES, RW_WIDTH))]
    out_blk = pl.BlockSpec((tm, RW_WIDTH), lambda i: (i, 0))
    outs = [jax.ShapeDtypeStruct((t, RW_WIDTH), BF16)] * 9
    outs.append(jax.ShapeDtypeStruct((t // tm, tm // CHUNK, RW_WIDTH), F32))
    out_specs = [out_blk] * 9 + [pl.BlockSpec((1, tm // CHUNK, RW_WIDTH), lambda i: (i, 0, 0))]
    return pl.pallas_call(
        functools.partial(_rwkv_prep_kernel, tm=tm, nseq=nseq),
        out_shape=outs,
        grid=(t // tm,),
        in_specs=in_specs,
        out_specs=out_specs,
        scratch_shapes=[pltpu.VMEM((1, n), F32), pltpu.VMEM((1, nlo), F32)],
        compiler_params=_cparams("arbitrary"),
        name="rwkv_prep",
    )(h, w_rkv, w_lo, mu_rkv, mu_lo, params, wup, aup, gup)


PAIR = 2 * RW_HEAD_DIM


def _dot_bf16(a, b, dims=None):
    return _dot(a.astype(BF16), b.astype(BF16), dims)


def _split2(x):
    h = x.astype(BF16)
    return h, (x - h.astype(F32)).astype(BF16)


def _dot_split_lhs(a, b, dims=None):
    ah, al = _split2(a)
    bh = b.astype(BF16)
    return _dot(ah, bh, dims) + _dot(al, bh, dims)


def _rwkv_chunk_kernel(at_ref, rt_ref, bt_ref, kt_ref, bg_ref, kg_ref, v_ref, bonus_ref, g_ref,
                       gam_ref, lng_ref, lnb_ref, *rest, n_cast, cps):
    cast_in, o_ref, cast_out, s_ref = rest[:n_cast], rest[n_cast], rest[n_cast + 1:2 * n_cast + 1], rest[-1]
    c = pl.program_id(1)

    for src, dst in zip(cast_in, cast_out):
        dst[...] = src[...].astype(dst.dtype)

    @pl.when(c == 0)
    def _():
        s_ref[...] = jnp.zeros(s_ref.shape, F32)

    lane = lax.broadcasted_iota(jnp.int32, (1, PAIR), 1)
    first = lane < RW_HEAD_DIM
    rho = lax.broadcasted_iota(jnp.int32, (PAIR, PAIR), 0)
    sig = lax.broadcasted_iota(jnp.int32, (PAIR, PAIR), 1)
    strict, incl, eye = sig < rho, sig <= rho, sig == rho
    own = (rho // RW_HEAD_DIM) == (sig // RW_HEAD_DIM)

    def stacked(x):
        z = jnp.zeros_like(x)
        return jnp.concatenate([jnp.where(first, x, z), jnp.where(first, z, x)], axis=0)

    n_pairs = RW_HEADS // 2
    units = [(ci, p) for ci in range(cps) for p in range(n_pairs)]
    rws = [slice(ci * CHUNK, (ci + 1) * CHUNK) for ci in range(cps)]
    sls = [slice(p * PAIR, (p + 1) * PAIR) for p in range(n_pairs)]
    zero = jnp.zeros((PAIR, PAIR), F32)
    st = [[stacked(ref[rws[ci], sls[p]]) for ref in (at_ref, rt_ref, bt_ref, kt_ref, bg_ref, kg_ref, v_ref)]
          for ci, p in units]
    un = range(len(units))
    prods = [_dot(jnp.concatenate([q[0], q[1]], axis=0), jnp.concatenate([q[2], q[3]], axis=0), NT_DIMS) for q in st]
    lmat = [jnp.where(strict, pr_[:PAIR, :PAIR], zero) for pr_ in prods]
    sak = [jnp.where(strict, pr_[:PAIR, PAIR:], zero) for pr_ in prods]
    lrbk = [jnp.concatenate([jnp.where(incl, pr_[PAIR:, :PAIR], zero), jnp.where(incl, pr_[PAIR:, PAIR:], zero)],
                            axis=1).astype(BF16) for pr_ in prods]
    akv = [_dot(sak[u].astype(BF16), st[u][6]) for u in un]
    ident = jnp.where(eye, 1.0, 0.0)
    li = [_dot_bf16(lmat[u], lmat[u]) for u in un]
    tmat = [ident + lmat[u] for u in un]
    n_steps = CHUNK.bit_length() - 1
    for it in range(1, n_steps):
        if it + 1 < n_steps:
            res = [_dot_bf16(li[u], jnp.concatenate([li[u], tmat[u]], axis=1)) for u in un]
            tmat = [tmat[u] + res[u][:, PAIR:] for u in un]
            li = [res[u][:, :PAIR] for u in un]
        else:
            tmat = [tmat[u] + _dot_bf16(li[u], tmat[u]) for u in un]
    x = [_dot_bf16(tmat[u], jnp.concatenate([st[u][0].astype(F32), akv[u]], axis=1)) for u in un]
    gmat = [jnp.concatenate([x[u], jnp.concatenate([zero, st[u][6].astype(F32)], axis=1)], axis=0).astype(BF16)
            for u in un]
    out1 = [_dot(lrbk[u], gmat[u]) for u in un]
    out2 = [_dot(gmat[u], jnp.concatenate([st[u][4], st[u][5]], axis=0), TN_DIMS) for u in un]
    qe = [out1[u][:, :PAIR] + st[u][1].astype(F32) for u in un]

    state = [s_ref[p] for p in range(n_pairs)]
    for ci in range(cps):
        base = ci * n_pairs
        y = [_dot_bf16(qe[base + p], state[p], NT_DIMS) + out1[base + p][:, PAIR:] for p in range(n_pairs)]
        nxt = []
        for p in range(n_pairs):
            mmat = out2[base + p][:PAIR] + jnp.where(eye, gam_ref[ci][:, sls[p]], 0.0)
            nxt.append(_dot_split_lhs(state[p], mmat) + out2[base + p][PAIR:])
        state = nxt
        for p in range(n_pairs):
            sl = sls[p]
            mean = jnp.sum(y[p], axis=-1, keepdims=True) * (1.0 / RW_HEAD_DIM)
            d = jnp.where(own, y[p] - mean, 0.0)
            var = jnp.sum(d * d, axis=-1, keepdims=True) * (1.0 / RW_HEAD_DIM)
            yn = d * lax.rsqrt(var + GN_EPS)
            yn = yn[:CHUNK] + yn[CHUNK:]
            out = ((yn * lng_ref[:, sl] + lnb_ref[:, sl] + bonus_ref[rws[ci], sl].astype(F32))
                   * g_ref[rws[ci], sl].astype(F32))
            o_ref[rws[ci], sl] = out.astype(o_ref.dtype)
    for p in range(n_pairs):
        s_ref[p] = state[p]


def _rwkv_chunks(prep, lnx_g, lnx_b, batch, seq, cps, casts=()):
    at, rt, bt, kt, bg, kg, vb, bonus, g, gam = prep
    t = at.shape[0]
    nc = seq // (CHUNK * cps)
    steps = batch * nc
    gam = gam.reshape(t // CHUNK, 1, RW_WIDTH)
    blk = pl.BlockSpec((cps * CHUNK, RW_WIDTH), lambda b, c: (b * nc + c, 0))
    rowspec = pl.BlockSpec((1, RW_WIDTH), lambda b, c: (0, 0))
    cast_specs = [pl.BlockSpec((w.shape[0] // steps, w.shape[1]), lambda b, c: (b * nc + c, 0)) for w in casts]
    outs = pl.pallas_call(
        functools.partial(_rwkv_chunk_kernel, n_cast=len(casts), cps=cps),
        out_shape=[jax.ShapeDtypeStruct((t, RW_WIDTH), BF16)] + [jax.ShapeDtypeStruct(w.shape, BF16) for w in casts],
        grid=(batch, nc),
        in_specs=([blk] * 9 + [pl.BlockSpec((cps, 1, RW_WIDTH), lambda b, c: (b * nc + c, 0, 0)), rowspec, rowspec]
                  + cast_specs),
        out_specs=[blk] + cast_specs,
        scratch_shapes=[pltpu.VMEM((RW_HEADS // 2, PAIR, PAIR), F32)],
        compiler_params=_cparams("arbitrary", "arbitrary"),
        name="rwkv_chunks",
    )(at, rt, bt, kt, bg, kg, vb, bonus, g, gam, lnx_g.reshape(1, RW_WIDTH), lnx_b.reshape(1, RW_WIDTH), *casts)
    return outs[0], outs[1:]


def _cast_kernel(x_ref, o_ref):
    o_ref[...] = x_ref[...].astype(o_ref.dtype)


def _cast_bf16(w, rows):
    n, m = w.shape
    return pl.pallas_call(
        _cast_kernel,
        out_shape=jax.ShapeDtypeStruct((n, m), BF16),
        grid=(n // rows,),
        in_specs=[pl.BlockSpec((rows, m), lambda i: (i, 0))],
        out_specs=pl.BlockSpec((rows, m), lambda i: (i, 0)),
        compiler_params=_cparams("parallel"),
        name="cast_bf16",
    )(w)


def _merge_kernel(h_ref, ya_ref, yb_ref, wga_ref, wgb_ref, pa_ref, pb_ref, o_ref):
    h = h_ref[...]
    ga = jax.nn.sigmoid(_dot(h, wga_ref[...]))
    gb = jax.nn.sigmoid(_dot(h, wgb_ref[...]))
    o_ref[...] = (ga * _dot(ya_ref[...], pa_ref[...]) + gb * _dot(yb_ref[...], pb_ref[...])).astype(o_ref.dtype)


def _merge(h, ya, yb, wga, wgb, pa, pb, tm, tn):
    t, d = h.shape
    n = wga.shape[1]
    ka, kb = ya.shape[1], yb.shape[1]
    return pl.pallas_call(
        _merge_kernel,
        out_shape=jax.ShapeDtypeStruct((t, n), BF16),
        grid=(t // tm, n // tn),
        in_specs=[pl.BlockSpec((tm, d), lambda i, j: (i, 0)),
                  pl.BlockSpec((tm, ka), lambda i, j: (i, 0)),
                  pl.BlockSpec((tm, kb), lambda i, j: (i, 0)),
                  pl.BlockSpec((d, tn), lambda i, j: (0, j)),
                  pl.BlockSpec((d, tn), lambda i, j: (0, j)),
                  pl.BlockSpec((ka, tn), lambda i, j: (0, j)),
                  pl.BlockSpec((kb, tn), lambda i, j: (0, j))],
        out_specs=pl.BlockSpec((tm, tn), lambda i, j: (i, j)),
        compiler_params=_cparams("parallel", "parallel"),
        name="gated_merge",
    )(h, ya, yb, wga, wgb, pa, pb)


def _first_lane_where(cond, lane):
    return jnp.min(jnp.where(cond, lane, LANES), axis=-1, keepdims=True)


def _outproj_kernel(m_ref, x_ref, wo_ref, g2_ref, wrh_ref, wrl_ref, rb_ref, x1_ref, h2_ref, route_ref):
    x1 = x_ref[...] + _dot(m_ref[...], wo_ref[...])
    x1_ref[...] = x1
    ms = jnp.mean(x1 * x1, axis=-1, keepdims=True)
    h2 = x1 * lax.rsqrt(ms + NORM_EPS) * g2_ref[...]
    h2_ref[...] = h2
    hh = h2.astype(BF16)
    hl = (h2 - hh.astype(F32)).astype(BF16)
    wrh = wrh_ref[...]
    lg = _dot(hh, wrh) + _dot(hl, wrh) + _dot(hh, wrl_ref[...]) + rb_ref[...]

    lane = lax.broadcasted_iota(jnp.int32, lg.shape, 1)
    neg = -jnp.inf
    is_g = lane < N_GROUPS
    mg = jnp.max(jnp.where(is_g, lg, neg), axis=-1, keepdims=True)
    eg = jnp.where(is_g, jnp.exp(lg - mg), 0.0)
    pg = eg / jnp.sum(eg, axis=-1, keepdims=True)
    p_g_top = jnp.max(pg, axis=-1, keepdims=True)
    g_idx = _first_lane_where(is_g & (pg == p_g_top), lane)
    lo = N_GROUPS + g_idx * EXPERTS_PER_GROUP
    sel = (lane >= lo) & (lane < lo + EXPERTS_PER_GROUP)
    me = jnp.max(jnp.where(sel, lg, neg), axis=-1, keepdims=True)
    ee = jnp.where(sel, jnp.exp(lg - me), 0.0)
    pe = ee / jnp.sum(ee, axis=-1, keepdims=True)
    pe = jnp.where(sel, pe, -1.0)
    v1 = jnp.max(pe, axis=-1, keepdims=True)
    i1 = _first_lane_where(pe == v1, lane)
    pe2 = jnp.where(lane == i1, -1.0, pe)
    v2 = jnp.max(pe2, axis=-1, keepdims=True)
    i2 = _first_lane_where(pe2 == v2, lane)
    den = v1 + v2
    route = jnp.where(lane == 0, p_g_top * v1 / den,
                      jnp.where(lane == 1, p_g_top * v2 / den,
                                jnp.where(lane == 2, (i1 - N_GROUPS).astype(F32),
                                          jnp.where(lane == 3, (i2 - N_GROUPS).astype(F32), 0.0))))
    route_ref[...] = route


def _outproj(merged, x, wo, g2, wr_hi, wr_lo, rbias, tm):
    t, d = x.shape
    nr = wr_hi.shape[1]
    return pl.pallas_call(
        _outproj_kernel,
        out_shape=[jax.ShapeDtypeStruct((t, d), F32), jax.ShapeDtypeStruct((t, d), F32),
                   jax.ShapeDtypeStruct((t, nr), F32)],
        grid=(t // tm,),
        in_specs=[pl.BlockSpec((tm, d), lambda i: (i, 0)),
                  pl.BlockSpec((tm, d), lambda i: (i, 0)),
                  pl.BlockSpec((d, d), lambda i: (0, 0)),
                  pl.BlockSpec((1, d), lambda i: (0, 0)),
                  pl.BlockSpec((d, nr), lambda i: (0, 0)),
                  pl.BlockSpec((d, nr), lambda i: (0, 0)),
                  pl.BlockSpec((1, nr), lambda i: (0, 0))],
        out_specs=[pl.BlockSpec((tm, d), lambda i: (i, 0)),
                   pl.BlockSpec((tm, d), lambda i: (i, 0)),
                   pl.BlockSpec((tm, nr), lambda i: (i, 0))],
        compiler_params=_cparams("parallel"),
        name="outproj_norm_router",
    )(merged, x, wo, g2.reshape(1, d), wr_hi, wr_lo, rbias)


IDX_SLOTS = 3


def _moe_kernel(be_ref, nused_ref, tok_hbm, dst_hbm, h_hbm, wg_ref, wu_ref, wd_ref, y_hbm,
                xbuf, obuf, tok_s, dst_s, idx_sem, g_sem, s_sem, *, rows, n_real):
    b = pl.program_id(0)
    n_used = nused_ref[0]
    last = n_used - 1

    def idx_copies(blk, sl):
        return (pltpu.make_async_copy(tok_hbm.at[blk], tok_s.at[sl], idx_sem.at[sl, 0]),
                pltpu.make_async_copy(dst_hbm.at[blk], dst_s.at[sl], idx_sem.at[sl, 1]))

    def gather_row(r, isl, xsl):
        return pltpu.make_async_copy(h_hbm.at[tok_s[isl, r]], xbuf.at[xsl, r], g_sem.at[xsl])

    def gather_all(xsl):
        return pltpu.make_async_copy(h_hbm.at[pl.ds(0, rows)], xbuf.at[xsl], g_sem.at[xsl])

    def scatter_row(r, isl, osl):
        return pltpu.make_async_copy(obuf.at[osl, r], y_hbm.at[dst_s[isl, r]], s_sem.at[osl])

    def scatter_all(osl):
        return pltpu.make_async_copy(obuf.at[osl], y_hbm.at[pl.ds(0, rows)], s_sem.at[osl])

    def spare_fill(par):
        return pltpu.make_async_copy(obuf.at[1], y_hbm.at[pl.ds(n_real + par * rows, rows)], s_sem.at[1])

    @pl.when(b == 0)
    def _():
        obuf[1] = jnp.zeros(obuf.shape[1:], obuf.dtype)
        for par in range(2):
            spare_fill(par).start()
        for cp in idx_copies(0, 0):
            cp.start()
        for cp in idx_copies(0, 0):
            cp.wait()
        for cp in idx_copies(jnp.minimum(1, last), 1):
            cp.start()

        def issue(r, c):
            gather_row(r, 0, 0).start()
            return c

        lax.fori_loop(0, rows, issue, 0, unroll=8)
        for par in range(2):
            spare_fill(par).wait()

    @pl.when((b > 1) & (b < n_used))
    def _():
        scatter_all(b % 2).wait()

    for par in range(2):
        @pl.when((b < n_used) & (b % 2 == par))
        def _(par=par):
            cur_i, nxt_i, ld_i = b % IDX_SLOTS, (b + 1) % IDX_SLOTS, (b + 2) % IDX_SLOTS
            for cp in idx_copies(0, nxt_i):
                cp.wait()
            for r in range(rows):
                gather_row(r, nxt_i, 1 - par).start()
            for cp in idx_copies(jnp.minimum(b + 2, last), ld_i):
                cp.start()

            gather_all(par).wait()
            x = xbuf[par].astype(BF16)
            gate = _dot(x, wg_ref[...])
            up = _dot(x, wu_ref[...])
            mid = (gate * jax.nn.sigmoid(gate) * up).astype(BF16)
            obuf[par] = _dot(mid, wd_ref[...])
            for r in range(rows):
                scatter_row(r, cur_i, par).start()

    @pl.when(b == last)
    def _():
        @pl.when(b > 0)
        def _():
            scatter_all((b - 1) % 2).wait()

        scatter_all(b % 2).wait()
        gather_all((b + 1) % 2).wait()
        for cp in idx_copies(0, (b + 2) % IDX_SLOTS):
            cp.wait()


def _moe(h2, row_tok, row_dst, block_e, n_used, wg, wu, wd, rows):
    t, d = h2.shape
    n_blocks = row_tok.shape[0]
    a = t * TOP_K
    f = wg.shape[2]
    kern = functools.partial(_moe_kernel, rows=rows, n_real=a)
    grid_spec = pltpu.PrefetchScalarGridSpec(
        num_scalar_prefetch=2,
        grid=(n_blocks,),
        in_specs=[pl.BlockSpec(memory_space=pl.ANY),
                  pl.BlockSpec(memory_space=pl.ANY),
                  pl.BlockSpec(memory_space=pl.ANY),
                  pl.BlockSpec((None, d, f), lambda b, be, nu: (be[b], 0, 0)),
                  pl.BlockSpec((None, d, f), lambda b, be, nu: (be[b], 0, 0)),
                  pl.BlockSpec((None, f, d), lambda b, be, nu: (be[b], 0, 0))],
        out_specs=pl.BlockSpec(memory_space=pl.ANY),
        scratch_shapes=[pltpu.VMEM((2, rows, d), F32),
                        pltpu.VMEM((2, rows, d), F32),
                        pltpu.SMEM((IDX_SLOTS, rows), jnp.int32),
                        pltpu.SMEM((IDX_SLOTS, rows), jnp.int32),
                        pltpu.SemaphoreType.DMA((IDX_SLOTS, 2)),
                        pltpu.SemaphoreType.DMA((2,)),
                        pltpu.SemaphoreType.DMA((2,))],
    )
    return pl.pallas_call(
        kern,
        out_shape=jax.ShapeDtypeStruct((a + 2 * rows, d), F32),
        grid_spec=grid_spec,
        compiler_params=_cparams("arbitrary"),
        name="moe_experts",
    )(block_e, n_used, row_tok, row_dst, h2, wg, wu, wd)


def _combine_kernel(x1_ref, y0_ref, y1_ref, route_ref, o_ref):
    gts = route_ref[...]
    o_ref[...] = x1_ref[...] + gts[:, 0:1] * y0_ref[...] + gts[:, 1:2] * y1_ref[...]


def _combine(x1, y, route, tm):
    t, d = x1.shape
    nb = t // tm
    return pl.pallas_call(
        _combine_kernel,
        out_shape=jax.ShapeDtypeStruct((t, d), F32),
        grid=(nb,),
        in_specs=[pl.BlockSpec((tm, d), lambda i: (i, 0)),
                  pl.BlockSpec((tm, d), lambda i: (i, 0)),
                  pl.BlockSpec((tm, d), lambda i: (nb + i, 0)),
                  pl.BlockSpec((tm, LANES), lambda i: (i, 0))],
        out_specs=pl.BlockSpec((tm, d), lambda i: (i, 0)),
        compiler_params=_cparams("parallel"),
        name="moe_combine",
    )(x1, y, y, route)


def _routing_tables(route, rows):
    t = route.shape[0]
    a = t * TOP_K
    expert = route[:, 2:2 + TOP_K].astype(jnp.int32)
    flat_e = expert.reshape(a)
    ids = jnp.arange(a, dtype=jnp.int32)
    order = jnp.sort(flat_e * a + ids) % a
    counts = jnp.sum((flat_e[:, None] == jnp.arange(N_EXPERTS, dtype=jnp.int32)[None, :]).astype(jnp.int32), axis=0)
    padded = (counts + rows - 1) // rows * rows
    start = jnp.cumsum(counts) - counts
    pend = jnp.cumsum(padded)
    pstart = pend - padded
    n_rows = a + N_EXPERTS * rows
    n_blocks = n_rows // rows
    blk_row0 = jnp.arange(n_blocks, dtype=jnp.int32) * rows
    block_e = jnp.minimum(jnp.sum((pend[None, :] <= blk_row0[:, None]).astype(jnp.int32), axis=1), N_EXPERTS - 1)
    off0 = blk_row0 - pstart[block_e]
    cnt_b = counts[block_e]
    off = off0[:, None] + jnp.arange(rows, dtype=jnp.int32)[None, :]
    valid = (off >= 0) & (off < cnt_b[:, None])
    src = order[jnp.clip(start[block_e][:, None] + off, 0, a - 1)]
    row_tok = jnp.where(valid, src // TOP_K, 0)
    spare = a + (jnp.arange(n_blocks, dtype=jnp.int32)[:, None] % 2) * rows + jnp.arange(rows, dtype=jnp.int32)[None, :]
    row_dst = jnp.where(valid, (src % TOP_K) * t + src // TOP_K, spare)
    n_used = (pend[-1] // rows).astype(jnp.int32).reshape(1)
    return row_tok, row_dst, block_e.astype(jnp.int32), n_used


def _tiles(t, seq):
    return dict(
        norm_tm=min(512, t),
        proj_tm=min(1024, seq), proj_tn=512,
        attn_tq=min(256, seq), attn_tk=min(1024, seq), attn_heads=4,
        prep_tm=min(256, seq), rwkv_cps=2,
        merge_tm=min(1024, t), merge_tn=512,
        out_tm=min(512, t),
        moe_rows=256,
        comb_tm=min(512, t),
    )


def _pad_rows(w, n):
    return jnp.pad(w, ((0, n - w.shape[0]), (0, 0)))


def _pad_cols(w, n):
    return jnp.pad(w, ((0, 0), (0, n - w.shape[1])))


def kernel(x, norm1_g, w_in, q_norm_g, k_norm_g, lam_q1, lam_k1, lam_q2, lam_k2, subln_g, shift_mu, w0, w_up, a0, a_up, g_up, k_k, k_a, r_k, lnx_g, lnx_b, proj_a, proj_b, w_out, norm2_g, router_g, router_g_b, router_e, router_e_b, w_gate_e, w_up_e, w_down_e):
    batch, seq, d = x.shape
    t = batch * seq
    depth = norm1_g.shape[0]
    tl = _tiles(t, seq)
    qkw = DA_HEADS * 2 * DA_HEAD_DIM
    vw = DA_HEADS * DA_V_DIM
    c_q, c_k, c_v = 0, qkw, 2 * qkw
    c_rw = c_v + vw
    c_dw = c_rw + 3 * RW_WIDTH
    c_da = c_dw + DECAY_LORA
    c_dg = c_da + AAA_LORA
    c_ga = c_dg + GATE_LORA
    c_gb = c_ga + d
    cos_t, sinm_t, sinp_t = _rope_tables(seq)
    xf = x.reshape(t, d)

    for l in range(depth):
        lam_init = 0.8 - 0.6 * math.exp(-0.3 * l)
        wl = w_in[l]
        w_qk = wl[:, c_q:c_v].astype(BF16)
        w_v = wl[:, c_v:c_rw].astype(BF16)
        w_rkv = wl[:, c_rw:c_dw].astype(BF16)
        w_lo = jnp.concatenate([_pad_cols(wl[:, c_dw:c_da], LANES), _pad_cols(wl[:, c_da:c_dg], LANES),
                                wl[:, c_dg:c_ga]], axis=1).astype(BF16)
        w_ga = wl[:, c_ga:c_gb].astype(BF16)
        w_gb = wl[:, c_gb:].astype(BF16)
        mu = shift_mu[l]
        o_dw = 3 * RW_WIDTH
        mu_rkv = mu[:o_dw].reshape(1, -1)
        mu_lo = jnp.concatenate([jnp.pad(mu[o_dw:o_dw + DECAY_LORA], (0, LANES - DECAY_LORA)),
                                 jnp.pad(mu[o_dw + DECAY_LORA:o_dw + DECAY_LORA + AAA_LORA], (0, LANES - AAA_LORA)),
                                 mu[o_dw + DECAY_LORA + AAA_LORA:]]).reshape(1, -1)
        gain_row = jnp.concatenate([jnp.tile(q_norm_g[l], 2 * DA_HEADS) * (DA_HEAD_DIM ** -0.5 * LOG2_E),
                                    jnp.tile(k_norm_g[l], 2 * DA_HEADS)]).reshape(1, 2 * qkw)
        lam_params = jnp.stack([lam_q1[l], lam_k1[l], lam_q2[l], lam_k2[l]])
        zrow = jnp.zeros((RW_WIDTH,), F32)
        rw_params = jnp.stack([w0[l], a0[l], k_k[l], k_a[l], r_k[l].reshape(-1), zrow, zrow, zrow])

        h = _rmsnorm(xf, norm1_g[l], NORM_EPS, tl["norm_tm"])
        qk = _qk_proj(h, w_qk, gain_row, cos_t, sinm_t, sinp_t, seq, tl["proj_tm"], tl["proj_tn"])
        v = _matmul(h, w_v, BF16, tl["proj_tm"], tl["proj_tn"], "v_proj")
        ya = _diff_attention(qk, v, lam_params, subln_g[l], batch, seq, lam_init, tl["attn_tq"], tl["attn_tk"],
                             tl["attn_heads"])

        prep = _rwkv_prep(h, w_rkv, w_lo, mu_rkv, mu_lo, rw_params, _pad_rows(w_up[l], LANES).astype(BF16),
                          _pad_rows(a_up[l], LANES).astype(BF16), g_up[l].astype(BF16), seq, tl["prep_tm"])
        ew = (w_gate_e[l], w_up_e[l], w_down_e[l])
        ew2d = tuple(w.reshape(-1, w.shape[-1]) for w in ew)
        cps = tl["rwkv_cps"]
        steps = batch * (seq // (CHUNK * cps))
        ride = all(w.shape[0] % (steps * 2 * SUBLANES) == 0 and w.size * 4 // steps <= CAST_BLOCK_BYTES for w in ew2d)
        yb, ew_bf16 = _rwkv_chunks(prep, lnx_g[l], lnx_b[l], batch, seq, cps, ew2d if ride else ())
        if not ride:
            ew_bf16 = tuple(_cast_bf16(w, CAST_BLOCK_BYTES // (4 * w.shape[1])) for w in ew2d)
        wg_e, wu_e, wd_e = (c.reshape(w.shape) for c, w in zip(ew_bf16, ew))

        merged = _merge(h, ya, yb, w_ga, w_gb, proj_a[l].astype(BF16), proj_b[l].astype(BF16),
                        tl["merge_tm"], tl["merge_tn"])
        wr = _pad_cols(jnp.concatenate([router_g[l], router_e[l]], axis=1), LANES)
        wr_hi = wr.astype(BF16)
        wr_lo = (wr - wr_hi.astype(F32)).astype(BF16)
        rbias = jnp.pad(jnp.concatenate([router_g_b[l], router_e_b[l]]), (0, LANES - N_GROUPS - N_EXPERTS))
        x1, h2, route = _outproj(merged, xf, w_out[l].astype(BF16), norm2_g[l], wr_hi, wr_lo,
                                 rbias.reshape(1, LANES), tl["out_tm"])

        rows = tl["moe_rows"]
        row_tok, row_dst, block_e, n_used = _routing_tables(route, rows)
        y = _moe(h2, row_tok, row_dst, block_e, n_used, wg_e, wu_e, wd_e, rows)
        xf = _combine(x1, y, route, tl["comb_tm"])
    return xf.reshape(batch, seq, d)
```

```python
import functools
import math

import jax
import jax.numpy as jnp
from jax import lax
from jax.experimental import pallas as pl
from jax.experimental.pallas import tpu as pltpu

DA_HEADS = 8
DA_HEAD_DIM = 64
DA_V_DIM = 2 * DA_HEAD_DIM
ROT_DIM = DA_HEAD_DIM // 4
ROPE_THETA = 500000.0
SUBLN_EPS = 1e-5
RW_HEADS = 16
RW_HEAD_DIM = 64
RW_WIDTH = RW_HEADS * RW_HEAD_DIM
DECAY_LORA = 96
AAA_LORA = 96
GATE_LORA = 256
GN_EPS = 64e-5
N_GROUPS = 4
EXPERTS_PER_GROUP = 8
N_EXPERTS = N_GROUPS * EXPERTS_PER_GROUP
TOP_K = 2
NORM_EPS = 1e-6

LANES = 128
SUBLANES = 8
VMEM_LIMIT_BYTES = 56 * 1024 * 1024

CHUNK = 64
CAST_BLOCK_BYTES = 2 * 1024 * 1024

F32 = jnp.float32
BF16 = jnp.bfloat16

LOG2_E = math.log2(math.e)

NT_DIMS = (((1,), (1,)), ((), ()))
TN_DIMS = (((0,), (0,)), ((), ()))


def _cparams(*sem):
    return pltpu.CompilerParams(dimension_semantics=tuple(sem), vmem_limit_bytes=VMEM_LIMIT_BYTES)


def _dot(a, b, dims=None, precision=None):
    if dims is None:
        return jnp.dot(a, b, preferred_element_type=F32, precision=precision)
    return lax.dot_general(a, b, dims, preferred_element_type=F32, precision=precision)


def _split3(x):
    h = x.astype(BF16)
    r = x - h.astype(F32)
    m = r.astype(BF16)
    l = (r - m.astype(F32)).astype(BF16)
    return h, m, l


def _group_sum64(x, bd, terms):
    parts = _split3(x)[:terms]
    outs = []
    for s in range(x.shape[1] // LANES):
        sl = slice(s * LANES, (s + 1) * LANES)
        acc = _dot(parts[0][:, sl], bd)
        for part in parts[1:]:
            acc = acc + _dot(part[:, sl], bd)
        outs.append(acc)
    return outs[0] if len(outs) == 1 else jnp.concatenate(outs, axis=1)


def _block_diag_ones(n, blk, dtype=BF16):
    r = lax.broadcasted_iota(jnp.int32, (n, n), 0) // blk
    c = lax.broadcasted_iota(jnp.int32, (n, n), 1) // blk
    return jnp.where(r == c, 1.0, 0.0).astype(dtype)


def _rmsnorm_kernel(x_ref, g_ref, o_ref, *, eps):
    x = x_ref[...]
    ms = jnp.mean(x * x, axis=-1, keepdims=True)
    o_ref[...] = (x * lax.rsqrt(ms + eps) * g_ref[...]).astype(o_ref.dtype)


def _rmsnorm(x, g, eps, tm):
    t, d = x.shape
    return pl.pallas_call(
        functools.partial(_rmsnorm_kernel, eps=eps),
        out_shape=jax.ShapeDtypeStruct((t, d), BF16),
        grid=(t // tm,),
        in_specs=[pl.BlockSpec((tm, d), lambda i: (i, 0)),
                  pl.BlockSpec((1, d), lambda i: (0, 0))],
        out_specs=pl.BlockSpec((tm, d), lambda i: (i, 0)),
        compiler_params=_cparams("parallel"),
        name="rmsnorm",
    )(x, g.reshape(1, d))


def _matmul_kernel(a_ref, w_ref, o_ref):
    o_ref[...] = _dot(a_ref[...], w_ref[...]).astype(o_ref.dtype)


def _matmul(a, w, out_dtype, tm, tn, name):
    t, k = a.shape
    n = w.shape[1]
    return pl.pallas_call(
        _matmul_kernel,
        out_shape=jax.ShapeDtypeStruct((t, n), out_dtype),
        grid=(t // tm, n // tn),
        in_specs=[pl.BlockSpec((tm, k), lambda i, j: (i, 0)),
                  pl.BlockSpec((k, tn), lambda i, j: (0, j))],
        out_specs=pl.BlockSpec((tm, tn), lambda i, j: (i, j)),
        compiler_params=_cparams("parallel", "parallel"),
        name=name,
    )(a, w)


def _qk_proj_kernel(a_ref, w_ref, gain_ref, cos_ref, sin_ref, o_ref, *, tn):
    acc = _dot(a_ref[...], w_ref[...])
    bd = _block_diag_ones(LANES, DA_HEAD_DIM)
    ms = _group_sum64(acc * acc, bd, 1) * (1.0 / DA_HEAD_DIM)
    xn = acc * lax.rsqrt(ms + NORM_EPS) * gain_ref[...]
    reps = tn // LANES
    cos = jnp.tile(cos_ref[...], (1, reps))
    sin = jnp.tile(sin_ref[...], (1, reps))
    half = ROT_DIM // 2
    src = lax.broadcasted_iota(jnp.int32, (LANES, LANES), 0)
    dst = lax.broadcasted_iota(jnp.int32, (LANES, LANES), 1)
    d64 = dst % DA_HEAD_DIM
    perm = (jnp.where((d64 < half) & (src == dst + half), -1.0, 0.0)
            + jnp.where((d64 >= half) & (d64 < ROT_DIM) & (src == dst - half), 1.0, 0.0)).astype(BF16)
    xb = xn.astype(BF16)
    rot = jnp.concatenate([_dot(xb[:, k * LANES:(k + 1) * LANES], perm) for k in range(reps)], axis=1)
    o_ref[...] = (xn * cos + rot * sin).astype(o_ref.dtype)


def _qk_proj(h, w_qk, gain_row, cos_t, sin_t, seq, tm, tn):
    t, k = h.shape
    n = w_qk.shape[1]
    nseq = seq // tm
    return pl.pallas_call(
        functools.partial(_qk_proj_kernel, tn=tn),
        out_shape=jax.ShapeDtypeStruct((t, n), BF16),
        grid=(t // tm, n // tn),
        in_specs=[pl.BlockSpec((tm, k), lambda i, j: (i, 0)),
                  pl.BlockSpec((k, tn), lambda i, j: (0, j)),
                  pl.BlockSpec((1, tn), lambda i, j: (0, j)),
                  pl.BlockSpec((tm, LANES), lambda i, j: (i % nseq, 0)),
                  pl.BlockSpec((tm, LANES), lambda i, j: (i % nseq, 0))],
        out_specs=pl.BlockSpec((tm, tn), lambda i, j: (i, j)),
        compiler_params=_cparams("parallel", "parallel"),
        name="qk_proj",
    )(h, w_qk, gain_row, cos_t, sin_t)


def _rope_tables(seq):
    inv_freq = ROPE_THETA ** (-jnp.arange(0, ROT_DIM, 2, dtype=F32) / ROT_DIM)
    ang = jnp.arange(seq, dtype=F32)[:, None] * inv_freq[None, :]
    cos, sin = jnp.cos(ang), jnp.sin(ang)
    ones = jnp.ones((seq, DA_HEAD_DIM - ROT_DIM), F32)
    cos64 = jnp.concatenate([cos, cos, ones], axis=1)
    sin64 = jnp.concatenate([sin, sin, 0.0 * ones], axis=1)
    return tuple(jnp.concatenate([a, a], axis=1) for a in (cos64, sin64))


def _diff_attn_kernel(lam_ref, q_ref, k_ref, v_ref, g_ref, o_ref, m_ref, l_ref, a_ref, *, tq, tk, lam_init, nh):
    i = pl.program_id(2)
    lane = lax.broadcasted_iota(jnp.int32, (1, DA_V_DIM), 1)
    qs = []
    for h in range(nh):
        q = q_ref[:, h * DA_V_DIM:(h + 1) * DA_V_DIM]
        zero = jnp.zeros_like(q)
        qs.append(jnp.where(lane < DA_HEAD_DIM, q, zero))
        qs.append(jnp.where(lane >= DA_HEAD_DIM, q, zero))
    nc = 2 * nh
    qb = [jnp.concatenate([qs[2 * h], qs[2 * h + 1]], axis=0) for h in range(nh)]
    m_ref[...] = jnp.full(m_ref.shape, -jnp.inf, F32)
    l_ref[...] = jnp.zeros(l_ref.shape, F32)
    a_ref[...] = jnp.zeros(a_ref.shape, F32)

    def step(off, width, mask):
        reps = width // LANES
        ks = [k_ref[pl.ds(off, width), h * DA_V_DIM:(h + 1) * DA_V_DIM] for h in range(nh)]
        vs = [v_ref[pl.ds(off, width), h * DA_V_DIM:(h + 1) * DA_V_DIM] for h in range(nh)]
        sb = [_dot(qb[h], ks[h], NT_DIMS) for h in range(nh)]
        s = [sb[c // 2][(c % 2) * tq:(c % 2 + 1) * tq] for c in range(nc)]
        ps = []
        for c in range(nc):
            sc = s[c] if mask is None else jnp.where(mask, s[c], -jnp.inf)
            m_old = m_ref[c]
            m_new = jnp.maximum(m_old, jnp.broadcast_to(jnp.max(sc, axis=-1, keepdims=True), m_old.shape))
            alpha = jnp.exp2(m_old - m_new)
            p = jnp.exp2(sc - jnp.tile(m_new, (1, reps)))
            part = p[:, :LANES]
            for kk in range(1, reps):
                part = part + p[:, kk * LANES:(kk + 1) * LANES]
            l_ref[c] = alpha * l_ref[c] + part
            a_ref[c] = alpha * a_ref[c]
            m_ref[c] = m_new
            ps.append(p.astype(BF16))
        for h in range(nh):
            pv = _dot(jnp.concatenate([ps[2 * h], ps[2 * h + 1]], axis=0), vs[h])
            a_ref[2 * h] = a_ref[2 * h] + pv[:tq]
            a_ref[2 * h + 1] = a_ref[2 * h + 1] + pv[tq:]

    n_full = (i * tq) // tk

    def full_body(j, c):
        step(pl.multiple_of(j * tk, tk), tk, None)
        return c

    lax.fori_loop(0, n_full, full_body, 0)

    sub = tk // tq
    rem = i % sub
    base = pl.multiple_of(n_full * tk, tk)
    for v in range(sub):
        @pl.when(rem == v)
        def _(v=v):
            width = (v + 1) * tq
            row = v * tq + lax.broadcasted_iota(jnp.int32, (tq, width), 0)
            col = lax.broadcasted_iota(jnp.int32, (tq, width), 1)
            step(base, width, col <= row)

    lq1, lk1, lq2, lk2 = (lam_ref[r:r + 1, :] for r in range(4))
    lam = (jnp.exp(jnp.sum(lq1 * lk1, axis=-1, keepdims=True))
           - jnp.exp(jnp.sum(lq2 * lk2, axis=-1, keepdims=True)) + lam_init)
    for h in range(nh):
        l1 = jnp.sum(l_ref[2 * h], axis=-1, keepdims=True)
        l2 = jnp.sum(l_ref[2 * h + 1], axis=-1, keepdims=True)
        o = a_ref[2 * h] / l1 - lam * (a_ref[2 * h + 1] / l2)
        ms = jnp.mean(o * o, axis=-1, keepdims=True)
        o = o * lax.rsqrt(ms + SUBLN_EPS) * (g_ref[...] * (1.0 - lam_init))
        o_ref[:, h * DA_V_DIM:(h + 1) * DA_V_DIM] = o.astype(o_ref.dtype)


def _diff_attention(qk, v, lam_params, subln_g, batch, seq, lam_init, tq, tk, nh):
    assert tk % tq == 0 and seq % tk == 0 and DA_HEADS % nh == 0
    t = qk.shape[0]
    nq = seq // tq
    w = nh * DA_V_DIM
    hg = DA_HEADS // nh
    kern = functools.partial(_diff_attn_kernel, tq=tq, tk=tk, lam_init=lam_init, nh=nh)
    return pl.pallas_call(
        kern,
        out_shape=jax.ShapeDtypeStruct((t, DA_HEADS * DA_V_DIM), BF16),
        grid=(batch, hg, nq),
        in_specs=[pl.BlockSpec((4, DA_HEAD_DIM), lambda b, h, i: (0, 0)),
                  pl.BlockSpec((tq, w), lambda b, h, i: (b * nq + i, h)),
                  pl.BlockSpec((seq, w), lambda b, h, i: (b, hg + h)),
                  pl.BlockSpec((seq, w), lambda b, h, i: (b, h)),
                  pl.BlockSpec((1, DA_V_DIM), lambda b, h, i: (0, 0))],
        out_specs=pl.BlockSpec((tq, w), lambda b, h, i: (b * nq + i, h)),
        scratch_shapes=[pltpu.VMEM((2 * nh, tq, LANES), F32), pltpu.VMEM((2 * nh, tq, LANES), F32),
                        pltpu.VMEM((2 * nh, tq, DA_V_DIM), F32)],
        compiler_params=_cparams("parallel", "parallel", "parallel"),
        name="diff_attention",
    )(lam_params, qk, qk, v, subln_g.reshape(1, DA_V_DIM))


DECAY_SCALE = math.exp(-0.5)


def _rwkv_prep_kernel(h_ref, w_ref, wlo_ref, mu_ref, mulo_ref, par_ref, wup_ref, aup_ref, gup_ref,
                      at_ref, rt_ref, bt_ref, kt_ref, bg_ref, kg_ref, vb_ref, bonus_ref, g_ref, gam_ref,
                      last_ref, lastlo_ref, *, tm, nseq):
    i = pl.program_id(0)

    @pl.when(i == 0)
    def _():
        last_ref[...] = jnp.zeros(last_ref.shape, F32)
        lastlo_ref[...] = jnp.zeros(lastlo_ref.shape, F32)

    seq_start = (i % nseq) == 0
    row0 = lax.broadcasted_iota(jnp.int32, (tm, 1), 0) == 0
    h = h_ref[...]

    def shifted_proj(w_r, mu_r, carry_ref):
        x = _dot(h, w_r[...])
        last = jnp.where(seq_start, 0.0, carry_ref[...])
        prev = jnp.where(row0, last, pltpu.roll(x, 1, 0))
        carry_ref[...] = x[tm - 1:tm, :]
        return x + (prev - x) * mu_r[...]

    z = shifted_proj(w_ref, mu_ref, last_ref)
    lo = shifted_proj(wlo_ref, mulo_ref, lastlo_ref)
    r, k, v = z[:, :RW_WIDTH], z[:, RW_WIDTH:2 * RW_WIDTH], z[:, 2 * RW_WIDTH:]
    dw, da, dg = lo[:, :LANES], lo[:, LANES:2 * LANES], lo[:, 2 * LANES:]
    w0, a0, k_k, k_a, r_k = (par_ref[j:j + 1, :] for j in range(5))

    u = w0 + _dot(jnp.tanh(dw).astype(BF16), wup_ref[...])
    ld = -DECAY_SCALE * jax.nn.sigmoid(u)
    a = jax.nn.sigmoid(a0 + _dot(da.astype(BF16), aup_ref[...]))
    g = _dot(jax.nn.sigmoid(dg).astype(BF16), gup_ref[...])

    bd = _block_diag_ones(LANES, RW_HEAD_DIM)
    kk = k * k_k
    kk = kk / jnp.maximum(jnp.sqrt(_group_sum64(kk * kk, bd, 1)), 1e-12)
    k2 = k * (1.0 + (a - 1.0) * k_a)
    bonus = _group_sum64(r * k2 * r_k, bd, 1) * v

    t_i = lax.broadcasted_iota(jnp.int32, (tm, tm), 0)
    s_i = lax.broadcasted_iota(jnp.int32, (tm, tm), 1)
    same = (t_i // CHUNK) == (s_i // CHUNK)
    tri = jnp.where(same & (s_i <= t_i), 1.0, 0.0).astype(BF16)
    rest = jnp.where(same & (s_i > t_i), 1.0, 0.0).astype(BF16)
    c_i = lax.broadcasted_iota(jnp.int32, (tm // CHUNK, tm), 0)
    cs_i = lax.broadcasted_iota(jnp.int32, (tm // CHUNK, tm), 1)
    whole = jnp.where(cs_i // CHUNK == c_i, 1.0, 0.0).astype(BF16)

    ldh, ldm = _split2(ld)

    def sel(m):
        return _dot(m, ldh) + _dot(m, ldm)

    cum = sel(tri)
    e_neg = jnp.exp(-cum)
    e_rem = jnp.exp(sel(rest))
    b = kk * a
    at_ref[...] = (-kk * jnp.exp(cum - ld)).astype(BF16)
    rt_ref[...] = (r * jnp.exp(cum)).astype(BF16)
    bt_ref[...] = (b * e_neg).astype(BF16)
    kt_ref[...] = (k2 * e_neg).astype(BF16)
    bg_ref[...] = (b * e_rem).astype(BF16)
    kg_ref[...] = (k2 * e_rem).astype(BF16)
    vb_ref[...] = v.astype(BF16)
    bonus_ref[...] = bonus.astype(BF16)
    g_ref[...] = g.astype(BF16)
    gam_ref[0] = jnp.exp(sel(whole))


def _rwkv_prep(h, w_rkv, w_lo, mu_rkv, mu_lo, params, wup, aup, gup, seq, tm):
    t, d = h.shape
    n, nlo = w_rkv.shape[1], w_lo.shape[1]
    nseq = seq // tm
    whole = lambda shape: pl.BlockSpec(shape, lambda i: (0, 0), pipeline_mode=pl.Buffered(1))
    in_specs = [pl.BlockSpec((tm, d), lambda i: (i, 0)), whole((d, n)), whole((d, nlo)), whole((1, n)), whole((1, nlo)),
                whole((SUBLANES, RW_WIDTH)), whole((LANES, RW_WIDTH)), whole((LANES, RW_WIDTH)),
                whole((2 * LANES, RW_WIDTH))]
    out_blk = pl.BlockSpec((tm, RW_WIDTH), lambda i: (i, 0))
    outs = [jax.ShapeDtypeStruct((t, RW_WIDTH), BF16)] * 9
    outs.append(jax.ShapeDtypeStruct((t // tm, tm // CHUNK, RW_WIDTH), F32))
    out_specs = [out_blk] * 9 + [pl.BlockSpec((1, tm // CHUNK, RW_WIDTH), lambda i: (i, 0, 0))]
    return pl.pallas_call(
        functools.partial(_rwkv_prep_kernel, tm=tm, nseq=nseq),
        out_shape=outs,
        grid=(t // tm,),
        in_specs=in_specs,
        out_specs=out_specs,
        scratch_shapes=[pltpu.VMEM((1, n), F32), pltpu.VMEM((1, nlo), F32)],
        compiler_params=_cparams("arbitrary"),
        name="rwkv_prep",
    )(h, w_rkv, w_lo, mu_rkv, mu_lo, params, wup, aup, gup)


PAIR = 2 * RW_HEAD_DIM


def _dot_bf16(a, b, dims=None):
    return _dot(a.astype(BF16), b.astype(BF16), dims)


def _split2(x):
    h = x.astype(BF16)
    return h, (x - h.astype(F32)).astype(BF16)


def _dot_split_lhs(a, b, dims=None):
    ah, al = _split2(a)
    bh = b.astype(BF16)
    return _dot(ah, bh, dims) + _dot(al, bh, dims)


def _rwkv_chunk_kernel(at_ref, rt_ref, bt_ref, kt_ref, bg_ref, kg_ref, v_ref, bonus_ref, g_ref,
                       gam_ref, lng_ref, lnb_ref, *rest, n_cast, cps):
    cast_in, o_ref, cast_out, s_ref = rest[:n_cast], rest[n_cast], rest[n_cast + 1:2 * n_cast + 1], rest[-1]
    c = pl.program_id(1)

    for src, dst in zip(cast_in, cast_out):
        dst[...] = src[...].astype(dst.dtype)

    @pl.when(c == 0)
    def _():
        s_ref[...] = jnp.zeros(s_ref.shape, F32)

    lane = lax.broadcasted_iota(jnp.int32, (1, PAIR), 1)
    first = lane < RW_HEAD_DIM
    rho = lax.broadcasted_iota(jnp.int32, (PAIR, PAIR), 0)
    sig = lax.broadcasted_iota(jnp.int32, (PAIR, PAIR), 1)
    strict, incl, eye = sig < rho, sig <= rho, sig == rho
    own = (rho // RW_HEAD_DIM) == (sig // RW_HEAD_DIM)

    def stacked(x):
        z = jnp.zeros_like(x)
        return jnp.concatenate([jnp.where(first, x, z), jnp.where(first, z, x)], axis=0)

    n_pairs = RW_HEADS // 2
    units = [(ci, p) for ci in range(cps) for p in range(n_pairs)]
    rws = [slice(ci * CHUNK, (ci + 1) * CHUNK) for ci in range(cps)]
    sls = [slice(p * PAIR, (p + 1) * PAIR) for p in range(n_pairs)]
    zero = jnp.zeros((PAIR, PAIR), F32)
    st = [[stacked(ref[rws[ci], sls[p]]) for ref in (at_ref, rt_ref, bt_ref, kt_ref, bg_ref, kg_ref, v_ref)]
          for ci, p in units]
    un = range(len(units))
    prods = [_dot(jnp.concatenate([q[0], q[1]], axis=0), jnp.concatenate([q[2], q[3]], axis=0), NT_DIMS) for q in st]
    lmat = [jnp.where(strict, pr_[:PAIR, :PAIR], zero) for pr_ in prods]
    sak = [jnp.where(strict, pr_[:PAIR, PAIR:], zero) for pr_ in prods]
    lrbk = [jnp.concatenate([jnp.where(incl, pr_[PAIR:, :PAIR], zero), jnp.where(incl, pr_[PAIR:, PAIR:], zero)],
                            axis=1).astype(BF16) for pr_ in prods]
    akv = [_dot(sak[u].astype(BF16), st[u][6]) for u in un]
    ident = jnp.where(eye, 1.0, 0.0)
    li = [_dot_bf16(lmat[u], lmat[u]) for u in un]
    tmat = [ident + lmat[u] for u in un]
    n_steps = CHUNK.bit_length() - 1
    for it in range(1, n_steps):
        if it + 1 < n_steps:
            res = [_dot_bf16(li[u], jnp.concatenate([li[u], tmat[u]], axis=1)) for u in un]
            tmat = [tmat[u] + res[u][:, PAIR:] for u in un]
            li = [res[u][:, :PAIR] for u in un]
        else:
            tmat = [tmat[u] + _dot_bf16(li[u], tmat[u]) for u in un]
    x = [_dot_bf16(tmat[u], jnp.concatenate([st[u][0].astype(F32), akv[u]], axis=1)) for u in un]
    gmat = [jnp.concatenate([x[u], jnp.concatenate([zero, st[u][6].astype(F32)], axis=1)], axis=0).astype(BF16)
            for u in un]
    out1 = [_dot(lrbk[u], gmat[u]) for u in un]
    out2 = [_dot(gmat[u], jnp.concatenate([st[u][4], st[u][5]], axis=0), TN_DIMS) for u in un]
    qe = [out1[u][:, :PAIR] + st[u][1].astype(F32) for u in un]

    state = [s_ref[p] for p in range(n_pairs)]
    for ci in range(cps):
        base = ci * n_pairs
        y = [_dot_bf16(qe[base + p], state[p], NT_DIMS) + out1[base + p][:, PAIR:] for p in range(n_pairs)]
        nxt = []
        for p in range(n_pairs):
            mmat = out2[base + p][:PAIR] + jnp.where(eye, gam_ref[ci][:, sls[p]], 0.0)
            nxt.append(_dot_split_lhs(state[p], mmat) + out2[base + p][PAIR:])
        state = nxt
        for p in range(n_pairs):
            sl = sls[p]
            mean = jnp.sum(y[p], axis=-1, keepdims=True) * (1.0 / RW_HEAD_DIM)
            d = jnp.where(own, y[p] - mean, 0.0)
            var = jnp.sum(d * d, axis=-1, keepdims=True) * (1.0 / RW_HEAD_DIM)
            yn = d * lax.rsqrt(var + GN_EPS)
            yn = yn[:CHUNK] + yn[CHUNK:]
            out = ((yn * lng_ref[:, sl] + lnb_ref[:, sl] + bonus_ref[rws[ci], sl].astype(F32))
                   * g_ref[rws[ci], sl].astype(F32))
            o_ref[rws[ci], sl] = out.astype(o_ref.dtype)
    for p in range(n_pairs):
        s_ref[p] = state[p]


def _rwkv_chunks(prep, lnx_g, lnx_b, batch, seq, cps, casts=()):
    at, rt, bt, kt, bg, kg, vb, bonus, g, gam = prep
    t = at.shape[0]
    nc = seq // (CHUNK * cps)
    steps = batch * nc
    gam = gam.reshape(t // CHUNK, 1, RW_WIDTH)
    blk = pl.BlockSpec((cps * CHUNK, RW_WIDTH), lambda b, c: (b * nc + c, 0))
    rowspec = pl.BlockSpec((1, RW_WIDTH), lambda b, c: (0, 0))
    cast_specs = [pl.BlockSpec((w.shape[0] // steps, w.shape[1]), lambda b, c: (b * nc + c, 0)) for w in casts]
    outs = pl.pallas_call(
        functools.partial(_rwkv_chunk_kernel, n_cast=len(casts), cps=cps),
        out_shape=[jax.ShapeDtypeStruct((t, RW_WIDTH), BF16)] + [jax.ShapeDtypeStruct(w.shape, BF16) for w in casts],
        grid=(batch, nc),
        in_specs=([blk] * 9 + [pl.BlockSpec((cps, 1, RW_WIDTH), lambda b, c: (b * nc + c, 0, 0)), rowspec, rowspec]
                  + cast_specs),
        out_specs=[blk] + cast_specs,
        scratch_shapes=[pltpu.VMEM((RW_HEADS // 2, PAIR, PAIR), F32)],
        compiler_params=_cparams("arbitrary", "arbitrary"),
        name="rwkv_chunks",
    )(at, rt, bt, kt, bg, kg, vb, bonus, g, gam, lnx_g.reshape(1, RW_WIDTH), lnx_b.reshape(1, RW_WIDTH), *casts)
    return outs[0], outs[1:]


def _cast_kernel(x_ref, o_ref):
    o_ref[...] = x_ref[...].astype(o_ref.dtype)


def _cast_bf16(w, rows):
    n, m = w.shape
    return pl.pallas_call(
        _cast_kernel,
        out_shape=jax.ShapeDtypeStruct((n, m), BF16),
        grid=(n // rows,),
        in_specs=[pl.BlockSpec((rows, m), lambda i: (i, 0))],
        out_specs=pl.BlockSpec((rows, m), lambda i: (i, 0)),
        compiler_params=_cparams("parallel"),
        name="cast_bf16",
    )(w)


def _merge_kernel(h_ref, ya_ref, yb_ref, wga_ref, wgb_ref, pa_ref, pb_ref, o_ref):
    h = h_ref[...]
    ga = jax.nn.sigmoid(_dot(h, wga_ref[...]))
    gb = jax.nn.sigmoid(_dot(h, wgb_ref[...]))
    o_ref[...] = (ga * _dot(ya_ref[...], pa_ref[...]) + gb * _dot(yb_ref[...], pb_ref[...])).astype(o_ref.dtype)


def _merge(h, ya, yb, wga, wgb, pa, pb, tm, tn):
    t, d = h.shape
    n = wga.shape[1]
    ka, kb = ya.shape[1], yb.shape[1]
    return pl.pallas_call(
        _merge_kernel,
        out_shape=jax.ShapeDtypeStruct((t, n), BF16),
        grid=(t // tm, n // tn),
        in_specs=[pl.BlockSpec((tm, d), lambda i, j: (i, 0)),
                  pl.BlockSpec((tm, ka), lambda i, j: (i, 0)),
                  pl.BlockSpec((tm, kb), lambda i, j: (i, 0)),
                  pl.BlockSpec((d, tn), lambda i, j: (0, j)),
                  pl.BlockSpec((d, tn), lambda i, j: (0, j)),
                  pl.BlockSpec((ka, tn), lambda i, j: (0, j)),
                  pl.BlockSpec((kb, tn), lambda i, j: (0, j))],
        out_specs=pl.BlockSpec((tm, tn), lambda i, j: (i, j)),
        compiler_params=_cparams("parallel", "parallel"),
        name="gated_merge",
    )(h, ya, yb, wga, wgb, pa, pb)


def _first_lane_where(cond, lane):
    return jnp.min(jnp.where(cond, lane, LANES), axis=-1, keepdims=True)


def _outproj_kernel(m_ref, x_ref, wo_ref, g2_ref, wrh_ref, wrl_ref, rb_ref, x1_ref, h2_ref, route_ref):
    x1 = x_ref[...] + _dot(m_ref[...], wo_ref[...])
    x1_ref[...] = x1
    ms = jnp.mean(x1 * x1, axis=-1, keepdims=True)
    h2 = x1 * lax.rsqrt(ms + NORM_EPS) * g2_ref[...]
    h2_ref[...] = h2
    hh = h2.astype(BF16)
    hl = (h2 - hh.astype(F32)).astype(BF16)
    wrh = wrh_ref[...]
    lg = _dot(hh, wrh) + _dot(hl, wrh) + _dot(hh, wrl_ref[...]) + rb_ref[...]

    lane = lax.broadcasted_iota(jnp.int32, lg.shape, 1)
    neg = -jnp.inf
    is_g = lane < N_GROUPS
    mg = jnp.max(jnp.where(is_g, lg, neg), axis=-1, keepdims=True)
    eg = jnp.where(is_g, jnp.exp(lg - mg), 0.0)
    pg = eg / jnp.sum(eg, axis=-1, keepdims=True)
    p_g_top = jnp.max(pg, axis=-1, keepdims=True)
    g_idx = _first_lane_where(is_g & (pg == p_g_top), lane)
    lo = N_GROUPS + g_idx * EXPERTS_PER_GROUP
    sel = (lane >= lo) & (lane < lo + EXPERTS_PER_GROUP)
    me = jnp.max(jnp.where(sel, lg, neg), axis=-1, keepdims=True)
    ee = jnp.where(sel, jnp.exp(lg - me), 0.0)
    pe = ee / jnp.sum(ee, axis=-1, keepdims=True)
    pe = jnp.where(sel, pe, -1.0)
    v1 = jnp.max(pe, axis=-1, keepdims=True)
    i1 = _first_lane_where(pe == v1, lane)
    pe2 = jnp.where(lane == i1, -1.0, pe)
    v2 = jnp.max(pe2, axis=-1, keepdims=True)
    i2 = _first_lane_where(pe2 == v2, lane)
    den = v1 + v2
    route = jnp.where(lane == 0, p_g_top * v1 / den,
                      jnp.where(lane == 1, p_g_top * v2 / den,
                                jnp.where(lane == 2, (i1 - N_GROUPS).astype(F32),
                                          jnp.where(lane == 3, (i2 - N_GROUPS).astype(F32), 0.0))))
    route_ref[...] = route


def _outproj(merged, x, wo, g2, wr_hi, wr_lo, rbias, tm):
    t, d = x.shape
    nr = wr_hi.shape[1]
    return pl.pallas_call(
        _outproj_kernel,
        out_shape=[jax.ShapeDtypeStruct((t, d), F32), jax.ShapeDtypeStruct((t, d), F32),
                   jax.ShapeDtypeStruct((t, nr), F32)],
        grid=(t // tm,),
        in_specs=[pl.BlockSpec((tm, d), lambda i: (i, 0)),
                  pl.BlockSpec((tm, d), lambda i: (i, 0)),
                  pl.BlockSpec((d, d), lambda i: (0, 0)),
                  pl.BlockSpec((1, d), lambda i: (0, 0)),
                  pl.BlockSpec((d, nr), lambda i: (0, 0)),
                  pl.BlockSpec((d, nr), lambda i: (0, 0)),
                  pl.BlockSpec((1, nr), lambda i: (0, 0))],
        out_specs=[pl.BlockSpec((tm, d), lambda i: (i, 0)),
                   pl.BlockSpec((tm, d), lambda i: (i, 0)),
                   pl.BlockSpec((tm, nr), lambda i: (i, 0))],
        compiler_params=_cparams("parallel"),
        name="outproj_norm_router",
    )(merged, x, wo, g2.reshape(1, d), wr_hi, wr_lo, rbias)


IDX_SLOTS = 3


def _moe_kernel(be_ref, nused_ref, tok_hbm, dst_hbm, h_hbm, wg_ref, wu_ref, wd_ref, y_hbm,
                xbuf, obuf, tok_s, dst_s, idx_sem, g_sem, s_sem, *, rows, n_real):
    b = pl.program_id(0)
    n_used = nused_ref[0]
    last = n_used - 1

    def idx_copies(blk, sl):
        return (pltpu.make_async_copy(tok_hbm.at[blk], tok_s.at[sl], idx_sem.at[sl, 0]),
                pltpu.make_async_copy(dst_hbm.at[blk], dst_s.at[sl], idx_sem.at[sl, 1]))

    def gather_row(r, isl, xsl):
        return pltpu.make_async_copy(h_hbm.at[tok_s[isl, r]], xbuf.at[xsl, r], g_sem.at[xsl])

    def gather_all(xsl):
        return pltpu.make_async_copy(h_hbm.at[pl.ds(0, rows)], xbuf.at[xsl], g_sem.at[xsl])

    def scatter_row(r, isl, osl):
        return pltpu.make_async_copy(obuf.at[osl, r], y_hbm.at[dst_s[isl, r]], s_sem.at[osl])

    def scatter_all(osl):
        return pltpu.make_async_copy(obuf.at[osl], y_hbm.at[pl.ds(0, rows)], s_sem.at[osl])

    def spare_fill(par):
        return pltpu.make_async_copy(obuf.at[1], y_hbm.at[pl.ds(n_real + par * rows, rows)], s_sem.at[1])

    @pl.when(b == 0)
    def _():
        obuf[1] = jnp.zeros(obuf.shape[1:], obuf.dtype)
        for par in range(2):
            spare_fill(par).start()
        for cp in idx_copies(0, 0):
            cp.start()
        for cp in idx_copies(0, 0):
            cp.wait()
        for cp in idx_copies(jnp.minimum(1, last), 1):
            cp.start()

        def issue(r, c):
            gather_row(r, 0, 0).start()
            return c

        lax.fori_loop(0, rows, issue, 0, unroll=8)
        for par in range(2):
            spare_fill(par).wait()

    @pl.when((b > 1) & (b < n_used))
    def _():
        scatter_all(b % 2).wait()

    for par in range(2):
        @pl.when((b < n_used) & (b % 2 == par))
        def _(par=par):
            cur_i, nxt_i, ld_i = b % IDX_SLOTS, (b + 1) % IDX_SLOTS, (b + 2) % IDX_SLOTS
            for cp in idx_copies(0, nxt_i):
                cp.wait()
            for r in range(rows):
                gather_row(r, nxt_i, 1 - par).start()
            for cp in idx_copies(jnp.minimum(b + 2, last), ld_i):
                cp.start()

            gather_all(par).wait()
            x = xbuf[par].astype(BF16)
            gate = _dot(x, wg_ref[...])
            up = _dot(x, wu_ref[...])
            mid = (gate * jax.nn.sigmoid(gate) * up).astype(BF16)
            obuf[par] = _dot(mid, wd_ref[...])
            for r in range(rows):
                scatter_row(r, cur_i, par).start()

    @pl.when(b == last)
    def _():
        @pl.when(b > 0)
        def _():
            scatter_all((b - 1) % 2).wait()

        scatter_all(b % 2).wait()
        gather_all((b + 1) % 2).wait()
        for cp in idx_copies(0, (b + 2) % IDX_SLOTS):
            cp.wait()


def _moe(h2, row_tok, row_dst, block_e, n_used, wg, wu, wd, rows):
    t, d = h2.shape
    n_blocks = row_tok.shape[0]
    a = t * TOP_K
    f = wg.shape[2]
    kern = functools.partial(_moe_kernel, rows=rows, n_real=a)
    grid_spec = pltpu.PrefetchScalarGridSpec(
        num_scalar_prefetch=2,
        grid=(n_blocks,),
        in_specs=[pl.BlockSpec(memory_space=pl.ANY),
                  pl.BlockSpec(memory_space=pl.ANY),
                  pl.BlockSpec(memory_space=pl.ANY),
                  pl.BlockSpec((None, d, f), lambda b, be, nu: (be[b], 0, 0)),
                  pl.BlockSpec((None, d, f), lambda b, be, nu: (be[b], 0, 0)),
                  pl.BlockSpec((None, f, d), lambda b, be, nu: (be[b], 0, 0))],
        out_specs=pl.BlockSpec(memory_space=pl.ANY),
        scratch_shapes=[pltpu.VMEM((2, rows, d), F32),
                        pltpu.VMEM((2, rows, d), F32),
                        pltpu.SMEM((IDX_SLOTS, rows), jnp.int32),
                        pltpu.SMEM((IDX_SLOTS, rows), jnp.int32),
                        pltpu.SemaphoreType.DMA((IDX_SLOTS, 2)),
                        pltpu.SemaphoreType.DMA((2,)),
                        pltpu.SemaphoreType.DMA((2,))],
    )
    return pl.pallas_call(
        kern,
        out_shape=jax.ShapeDtypeStruct((a + 2 * rows, d), F32),
        grid_spec=grid_spec,
        compiler_params=_cparams("arbitrary"),
        name="moe_experts",
    )(block_e, n_used, row_tok, row_dst, h2, wg, wu, wd)


def _combine_kernel(x1_ref, y0_ref, y1_ref, route_ref, o_ref):
    gts = route_ref[...]
    o_ref[...] = x1_ref[...] + gts[:, 0:1] * y0_ref[...] + gts[:, 1:2] * y1_ref[...]


def _combine(x1, y, route, tm):
    t, d = x1.shape
    nb = t // tm
    return pl.pallas_call(
        _combine_kernel,
        out_shape=jax.ShapeDtypeStruct((t, d), F32),
        grid=(nb,),
        in_specs=[pl.BlockSpec((tm, d), lambda i: (i, 0)),
                  pl.BlockSpec((tm, d), lambda i: (i, 0)),
                  pl.BlockSpec((tm, d), lambda i: (nb + i, 0)),
                  pl.BlockSpec((tm, LANES), lambda i: (i, 0))],
        out_specs=pl.BlockSpec((tm, d), lambda i: (i, 0)),
        compiler_params=_cparams("parallel"),
        name="moe_combine",
    )(x1, y, y, route)


def _routing_tables(route, rows):
    t = route.shape[0]
    a = t * TOP_K
    expert = route[:, 2:2 + TOP_K].astype(jnp.int32)
    flat_e = expert.reshape(a)
    ids = jnp.arange(a, dtype=jnp.int32)
    order = jnp.sort(flat_e * a + ids) % a
    counts = jnp.sum((flat_e[:, None] == jnp.arange(N_EXPERTS, dtype=jnp.int32)[None, :]).astype(jnp.int32), axis=0)
    padded = (counts + rows - 1) // rows * rows
    start = jnp.cumsum(counts) - counts
    pend = jnp.cumsum(padded)
    pstart = pend - padded
    n_rows = a + N_EXPERTS * rows
    n_blocks = n_rows // rows
    blk_row0 = jnp.arange(n_blocks, dtype=jnp.int32) * rows
    block_e = jnp.minimum(jnp.sum((pend[None, :] <= blk_row0[:, None]).astype(jnp.int32), axis=1), N_EXPERTS - 1)
    off0 = blk_row0 - pstart[block_e]
    cnt_b = counts[block_e]
    off = off0[:, None] + jnp.arange(rows, dtype=jnp.int32)[None, :]
    valid = (off >= 0) & (off < cnt_b[:, None])
    src = order[jnp.clip(start[block_e][:, None] + off, 0, a - 1)]
    row_tok = jnp.where(valid, src // TOP_K, 0)
    spare = a + (jnp.arange(n_blocks, dtype=jnp.int32)[:, None] % 2) * rows + jnp.arange(rows, dtype=jnp.int32)[None, :]
    row_dst = jnp.where(valid, (src % TOP_K) * t + src // TOP_K, spare)
    n_used = (pend[-1] // rows).astype(jnp.int32).reshape(1)
    return row_tok, row_dst, block_e.astype(jnp.int32), n_used


def _tiles(t, seq):
    return dict(
        norm_tm=min(512, t),
        proj_tm=min(1024, seq), proj_tn=512,
        attn_tq=min(256, seq), attn_tk=min(1024, seq), attn_heads=4,
        prep_tm=min(256, seq), rwkv_cps=2,
        merge_tm=min(1024, t), merge_tn=512,
        out_tm=min(512, t),
        moe_rows=256,
        comb_tm=min(512, t),
    )


def _pad_rows(w, n):
    return jnp.pad(w, ((0, n - w.shape[0]), (0, 0)))


def _pad_cols(w, n):
    return jnp.pad(w, ((0, 0), (0, n - w.shape[1])))


def kernel(x, norm1_g, w_in, q_norm_g, k_norm_g, lam_q1, lam_k1, lam_q2, lam_k2, subln_g, shift_mu, w0, w_up, a0, a_up, g_up, k_k, k_a, r_k, lnx_g, lnx_b, proj_a, proj_b, w_out, norm2_g, router_g, router_g_b, router_e, router_e_b, w_gate_e, w_up_e, w_down_e):
    batch, seq, d = x.shape
    t = batch * seq
    depth = norm1_g.shape[0]
    tl = _tiles(t, seq)
    qkw = DA_HEADS * 2 * DA_HEAD_DIM
    vw = DA_HEADS * DA_V_DIM
    c_q, c_k, c_v = 0, qkw, 2 * qkw
    c_rw = c_v + vw
    c_dw = c_rw + 3 * RW_WIDTH
    c_da = c_dw + DECAY_LORA
    c_dg = c_da + AAA_LORA
    c_ga = c_dg + GATE_LORA
    c_gb = c_ga + d
    cos_t, sin_t = _rope_tables(seq)
    xf = x.reshape(t, d)

    for l in range(depth):
        lam_init = 0.8 - 0.6 * math.exp(-0.3 * l)
        wl = w_in[l]
        w_qk = wl[:, c_q:c_v].astype(BF16)
        w_v = wl[:, c_v:c_rw].astype(BF16)
        w_rkv = wl[:, c_rw:c_dw].astype(BF16)
        w_lo = jnp.concatenate([_pad_cols(wl[:, c_dw:c_da], LANES), _pad_cols(wl[:, c_da:c_dg], LANES),
                                wl[:, c_dg:c_ga]], axis=1).astype(BF16)
        w_ga = wl[:, c_ga:c_gb].astype(BF16)
        w_gb = wl[:, c_gb:].astype(BF16)
        mu = shift_mu[l]
        o_dw = 3 * RW_WIDTH
        mu_rkv = mu[:o_dw].reshape(1, -1)
        mu_lo = jnp.concatenate([jnp.pad(mu[o_dw:o_dw + DECAY_LORA], (0, LANES - DECAY_LORA)),
                                 jnp.pad(mu[o_dw + DECAY_LORA:o_dw + DECAY_LORA + AAA_LORA], (0, LANES - AAA_LORA)),
                                 mu[o_dw + DECAY_LORA + AAA_LORA:]]).reshape(1, -1)
        gain_row = jnp.concatenate([jnp.tile(q_norm_g[l], 2 * DA_HEADS) * (DA_HEAD_DIM ** -0.5 * LOG2_E),
                                    jnp.tile(k_norm_g[l], 2 * DA_HEADS)]).reshape(1, 2 * qkw)
        lam_params = jnp.stack([lam_q1[l], lam_k1[l], lam_q2[l], lam_k2[l]])
        zrow = jnp.zeros((RW_WIDTH,), F32)
        rw_params = jnp.stack([w0[l], a0[l], k_k[l], k_a[l], r_k[l].reshape(-1), zrow, zrow, zrow])

        h = _rmsnorm(xf, norm1_g[l], NORM_EPS, tl["norm_tm"])
        qk = _qk_proj(h, w_qk, gain_row, cos_t, sin_t, seq, tl["proj_tm"], tl["proj_tn"])
        v = _matmul(h, w_v, BF16, tl["proj_tm"], tl["proj_tn"], "v_proj")
        ya = _diff_attention(qk, v, lam_params, subln_g[l], batch, seq, lam_init, tl["attn_tq"], tl["attn_tk"],
                             tl["attn_heads"])

        prep = _rwkv_prep(h, w_rkv, w_lo, mu_rkv, mu_lo, rw_params, _pad_rows(w_up[l], LANES).astype(BF16),
                          _pad_rows(a_up[l], LANES).astype(BF16), g_up[l].astype(BF16), seq, tl["prep_tm"])
        ew = (w_gate_e[l], w_up_e[l], w_down_e[l])
        ew2d = tuple(w.reshape(-1, w.shape[-1]) for w in ew)
        cps = tl["rwkv_cps"]
        steps = batch * (seq // (CHUNK * cps))
        ride = all(w.shape[0] % (steps * 2 * SUBLANES) == 0 and w.size * 4 // steps <= CAST_BLOCK_BYTES for w in ew2d)
        yb, ew_bf16 = _rwkv_chunks(prep, lnx_g[l], lnx_b[l], batch, seq, cps, ew2d if ride else ())
        if not ride:
            ew_bf16 = tuple(_cast_bf16(w, CAST_BLOCK_BYTES // (4 * w.shape[1])) for w in ew2d)
        wg_e, wu_e, wd_e = (c.reshape(w.shape) for c, w in zip(ew_bf16, ew))

        merged = _merge(h, ya, yb, w_ga, w_gb, proj_a[l].astype(BF16), proj_b[l].astype(BF16),
                        tl["merge_tm"], tl["merge_tn"])
        wr = _pad_cols(jnp.concatenate([router_g[l], router_e[l]], axis=1), LANES)
        wr_hi = wr.astype(BF16)
        wr_lo = (wr - wr_hi.astype(F32)).astype(BF16)
        rbias = jnp.pad(jnp.concatenate([router_g_b[l], router_e_b[l]]), (0, LANES - N_GROUPS - N_EXPERTS))
        x1, h2, route = _outproj(merged, xf, w_out[l].astype(BF16), norm2_g[l], wr_hi, wr_lo,
                                 rbias.reshape(1, LANES), tl["out_tm"])

        rows = tl["moe_rows"]
        row_tok, row_dst, block_e, n_used = _routing_tables(route, rows)
        y = _moe(h2, row_tok, row_dst, block_e, n_used, wg_e, wu_e, wd_e, rows)
        xf = _combine(x1, y, route, tl["comb_tm"])
    return xf.reshape(batch, seq, d)
```

```python
import functools
import math

import jax
import jax.numpy as jnp
from jax import lax
from jax.experimental import pallas as pl
from jax.experimental.pallas import tpu as pltpu

DA_HEADS = 8
DA_HEAD_DIM = 64
DA_V_DIM = 2 * DA_HEAD_DIM
ROT_DIM = DA_HEAD_DIM // 4
ROPE_THETA = 500000.0
SUBLN_EPS = 1e-5
RW_HEADS = 16
RW_HEAD_DIM = 64
RW_WIDTH = RW_HEADS * RW_HEAD_DIM
DECAY_LORA = 96
AAA_LORA = 96
GATE_LORA = 256
GN_EPS = 64e-5
N_GROUPS = 4
EXPERTS_PER_GROUP = 8
N_EXPERTS = N_GROUPS * EXPERTS_PER_GROUP
TOP_K = 2
NORM_EPS = 1e-6

LANES = 128
SUBLANES = 8
VMEM_LIMIT_BYTES = 56 * 1024 * 1024

CHUNK = 64
CAST_BLOCK_BYTES = 2 * 1024 * 1024

F32 = jnp.float32
BF16 = jnp.bfloat16

LOG2_E = math.log2(math.e)

NT_DIMS = (((1,), (1,)), ((), ()))
TN_DIMS = (((0,), (0,)), ((), ()))


def _cparams(*sem):
    return pltpu.CompilerParams(dimension_semantics=tuple(sem), vmem_limit_bytes=VMEM_LIMIT_BYTES)


def _dot(a, b, dims=None, precision=None):
    if dims is None:
        return jnp.dot(a, b, preferred_element_type=F32, precision=precision)
    return lax.dot_general(a, b, dims, preferred_element_type=F32, precision=precision)


def _split3(x):
    h = x.astype(BF16)
    r = x - h.astype(F32)
    m = r.astype(BF16)
    l = (r - m.astype(F32)).astype(BF16)
    return h, m, l


def _group_sum64(x, bd, terms):
    parts = _split3(x)[:terms]
    outs = []
    for s in range(x.shape[1] // LANES):
        sl = slice(s * LANES, (s + 1) * LANES)
        acc = _dot(parts[0][:, sl], bd)
        for part in parts[1:]:
            acc = acc + _dot(part[:, sl], bd)
        outs.append(acc)
    return outs[0] if len(outs) == 1 else jnp.concatenate(outs, axis=1)


def _block_diag_ones(n, blk, dtype=BF16):
    r = lax.broadcasted_iota(jnp.int32, (n, n), 0) // blk
    c = lax.broadcasted_iota(jnp.int32, (n, n), 1) // blk
    return jnp.where(r == c, 1.0, 0.0).astype(dtype)


def _rmsnorm_kernel(x_ref, g_ref, o_ref, *, eps):
    x = x_ref[...]
    ms = jnp.mean(x * x, axis=-1, keepdims=True)
    o_ref[...] = (x * lax.rsqrt(ms + eps) * g_ref[...]).astype(o_ref.dtype)


def _rmsnorm(x, g, eps, tm):
    t, d = x.shape
    return pl.pallas_call(
        functools.partial(_rmsnorm_kernel, eps=eps),
        out_shape=jax.ShapeDtypeStruct((t, d), BF16),
        grid=(t // tm,),
        in_specs=[pl.BlockSpec((tm, d), lambda i: (i, 0)),
                  pl.BlockSpec((1, d), lambda i: (0, 0))],
        out_specs=pl.BlockSpec((tm, d), lambda i: (i, 0)),
        compiler_params=_cparams("parallel"),
        name="rmsnorm",
    )(x, g.reshape(1, d))


def _matmul_kernel(a_ref, w_ref, o_ref):
    o_ref[...] = _dot(a_ref[...], w_ref[...]).astype(o_ref.dtype)


def _matmul(a, w, out_dtype, tm, tn, name):
    t, k = a.shape
    n = w.shape[1]
    return pl.pallas_call(
        _matmul_kernel,
        out_shape=jax.ShapeDtypeStruct((t, n), out_dtype),
        grid=(t // tm, n // tn),
        in_specs=[pl.BlockSpec((tm, k), lambda i, j: (i, 0)),
                  pl.BlockSpec((k, tn), lambda i, j: (0, j))],
        out_specs=pl.BlockSpec((tm, tn), lambda i, j: (i, j)),
        compiler_params=_cparams("parallel", "parallel"),
        name=name,
    )(a, w)


def _qk_proj_kernel(a_ref, w_ref, gain_ref, cos_ref, sin_ref, o_ref, *, tn):
    acc = _dot(a_ref[...], w_ref[...])
    bd = _block_diag_ones(LANES, DA_HEAD_DIM)
    ms = _group_sum64(acc * acc, bd, 1) * (1.0 / DA_HEAD_DIM)
    xn = acc * lax.rsqrt(ms + NORM_EPS) * gain_ref[...]
    reps = tn // LANES
    cos = jnp.tile(cos_ref[...], (1, reps))
    sin = jnp.tile(sin_ref[...], (1, reps))
    half = ROT_DIM // 2
    src = lax.broadcasted_iota(jnp.int32, (LANES, LANES), 0)
    dst = lax.broadcasted_iota(jnp.int32, (LANES, LANES), 1)
    d64 = dst % DA_HEAD_DIM
    perm = (jnp.where((d64 < half) & (src == dst + half), -1.0, 0.0)
            + jnp.where((d64 >= half) & (d64 < ROT_DIM) & (src == dst - half), 1.0, 0.0)).astype(BF16)
    xb = xn.astype(BF16)
    rot = jnp.concatenate([_dot(xb[:, k * LANES:(k + 1) * LANES], perm) for k in range(reps)], axis=1)
    o_ref[...] = (xn * cos + rot * sin).astype(o_ref.dtype)


def _qk_proj(h, w_qk, gain_row, cos_t, sin_t, seq, tm, tn):
    t, k = h.shape
    n = w_qk.shape[1]
    nseq = seq // tm
    return pl.pallas_call(
        functools.partial(_qk_proj_kernel, tn=tn),
        out_shape=jax.ShapeDtypeStruct((t, n), BF16),
        grid=(t // tm, n // tn),
        in_specs=[pl.BlockSpec((tm, k), lambda i, j: (i, 0)),
                  pl.BlockSpec((k, tn), lambda i, j: (0, j)),
                  pl.BlockSpec((1, tn), lambda i, j: (0, j)),
                  pl.BlockSpec((tm, LANES), lambda i, j: (i % nseq, 0)),
                  pl.BlockSpec((tm, LANES), lambda i, j: (i % nseq, 0))],
        out_specs=pl.BlockSpec((tm, tn), lambda i, j: (i, j)),
        compiler_params=_cparams("parallel", "parallel"),
        name="qk_proj",
    )(h, w_qk, gain_row, cos_t, sin_t)


def _rope_tables(seq):
    inv_freq = ROPE_THETA ** (-jnp.arange(0, ROT_DIM, 2, dtype=F32) / ROT_DIM)
    ang = jnp.arange(seq, dtype=F32)[:, None] * inv_freq[None, :]
    cos, sin = jnp.cos(ang), jnp.sin(ang)
    ones = jnp.ones((seq, DA_HEAD_DIM - ROT_DIM), F32)
    cos64 = jnp.concatenate([cos, cos, ones], axis=1)
    sin64 = jnp.concatenate([sin, sin, 0.0 * ones], axis=1)
    return tuple(jnp.concatenate([a, a], axis=1) for a in (cos64, sin64))


def _diff_attn_kernel(lam_ref, q_ref, k_ref, v_ref, g_ref, o_ref, m_ref, l_ref, a_ref, *, tq, tk, lam_init, nh):
    i = pl.program_id(2)
    lane = lax.broadcasted_iota(jnp.int32, (1, DA_V_DIM), 1)
    qs = []
    for h in range(nh):
        q = q_ref[:, h * DA_V_DIM:(h + 1) * DA_V_DIM]
        zero = jnp.zeros_like(q)
        qs.append(jnp.where(lane < DA_HEAD_DIM, q, zero))
        qs.append(jnp.where(lane >= DA_HEAD_DIM, q, zero))
    nc = 2 * nh
    qb = [jnp.concatenate([qs[2 * h], qs[2 * h + 1]], axis=0) for h in range(nh)]
    m_ref[...] = jnp.full(m_ref.shape, -jnp.inf, F32)
    l_ref[...] = jnp.zeros(l_ref.shape, F32)
    a_ref[...] = jnp.zeros(a_ref.shape, F32)

    def step(off, width, mask):
        reps = width // LANES
        ks = [k_ref[pl.ds(off, width), h * DA_V_DIM:(h + 1) * DA_V_DIM] for h in range(nh)]
        vs = [v_ref[pl.ds(off, width), h * DA_V_DIM:(h + 1) * DA_V_DIM] for h in range(nh)]
        sb = [_dot(qb[h], ks[h], NT_DIMS) for h in range(nh)]
        s = [sb[c // 2][(c % 2) * tq:(c % 2 + 1) * tq] for c in range(nc)]
        ps = []
        for c in range(nc):
            sc = s[c] if mask is None else jnp.where(mask, s[c], -jnp.inf)
            m_old = m_ref[c]
            m_new = jnp.maximum(m_old, jnp.broadcast_to(jnp.max(sc, axis=-1, keepdims=True), m_old.shape))
            alpha = jnp.exp2(m_old - m_new)
            p = jnp.exp2(sc - jnp.tile(m_new, (1, reps)))
            part = p[:, :LANES]
            for kk in range(1, reps):
                part = part + p[:, kk * LANES:(kk + 1) * LANES]
            l_ref[c] = alpha * l_ref[c] + part
            a_ref[c] = alpha * a_ref[c]
            m_ref[c] = m_new
            ps.append(p.astype(BF16))
        for h in range(nh):
            pv = _dot(jnp.concatenate([ps[2 * h], ps[2 * h + 1]], axis=0), vs[h])
            a_ref[2 * h] = a_ref[2 * h] + pv[:tq]
            a_ref[2 * h + 1] = a_ref[2 * h + 1] + pv[tq:]

    n_full = (i * tq) // tk

    def full_body(j, c):
        step(pl.multiple_of(j * tk, tk), tk, None)
        return c

    lax.fori_loop(0, n_full, full_body, 0)

    sub = tk // tq
    rem = i % sub
    base = pl.multiple_of(n_full * tk, tk)
    for v in range(sub):
        @pl.when(rem == v)
        def _(v=v):
            width = (v + 1) * tq
            row = v * tq + lax.broadcasted_iota(jnp.int32, (tq, width), 0)
            col = lax.broadcasted_iota(jnp.int32, (tq, width), 1)
            step(base, width, col <= row)

    lq1, lk1, lq2, lk2 = (lam_ref[r:r + 1, :] for r in range(4))
    lam = (jnp.exp(jnp.sum(lq1 * lk1, axis=-1, keepdims=True))
           - jnp.exp(jnp.sum(lq2 * lk2, axis=-1, keepdims=True)) + lam_init)
    for h in range(nh):
        l1 = jnp.sum(l_ref[2 * h], axis=-1, keepdims=True)
        l2 = jnp.sum(l_ref[2 * h + 1], axis=-1, keepdims=True)
        o = a_ref[2 * h] / l1 - lam * (a_ref[2 * h + 1] / l2)
        ms = jnp.mean(o * o, axis=-1, keepdims=True)
        o = o * lax.rsqrt(ms + SUBLN_EPS) * (g_ref[...] * (1.0 - lam_init))
        o_ref[:, h * DA_V_DIM:(h + 1) * DA_V_DIM] = o.astype(o_ref.dtype)


def _diff_attention(qk, v, lam_params, subln_g, batch, seq, lam_init, tq, tk, nh):
    assert tk % tq == 0 and seq % tk == 0 and DA_HEADS % nh == 0
    t = qk.shape[0]
    nq = seq // tq
    w = nh * DA_V_DIM
    hg = DA_HEADS // nh
    kern = functools.partial(_diff_attn_kernel, tq=tq, tk=tk, lam_init=lam_init, nh=nh)
    return pl.pallas_call(
        kern,
        out_shape=jax.ShapeDtypeStruct((t, DA_HEADS * DA_V_DIM), BF16),
        grid=(batch, hg, nq),
        in_specs=[pl.BlockSpec((4, DA_HEAD_DIM), lambda b, h, i: (0, 0)),
                  pl.BlockSpec((tq, w), lambda b, h, i: (b * nq + i, h)),
                  pl.BlockSpec((seq, w), lambda b, h, i: (b, hg + h)),
                  pl.BlockSpec((seq, w), lambda b, h, i: (b, h)),
                  pl.BlockSpec((1, DA_V_DIM), lambda b, h, i: (0, 0))],
        out_specs=pl.BlockSpec((tq, w), lambda b, h, i: (b * nq + i, h)),
        scratch_shapes=[pltpu.VMEM((2 * nh, tq, LANES), F32), pltpu.VMEM((2 * nh, tq, LANES), F32),
                        pltpu.VMEM((2 * nh, tq, DA_V_DIM), F32)],
        compiler_params=_cparams("parallel", "parallel", "parallel"),
        name="diff_attention",
    )(lam_params, qk, qk, v, subln_g.reshape(1, DA_V_DIM))


DECAY_SCALE = math.exp(-0.5)


def _rwkv_prep_kernel(h_ref, w_ref, wlo_ref, mu_ref, mulo_ref, par_ref, wup_ref, aup_ref, gup_ref,
                      at_ref, rt_ref, bt_ref, kt_ref, bg_ref, kg_ref, vb_ref, bonus_ref, g_ref, gam_ref,
                      last_ref, lastlo_ref, *, tm, nseq):
    i = pl.program_id(0)

    @pl.when(i == 0)
    def _():
        last_ref[...] = jnp.zeros(last_ref.shape, F32)
        lastlo_ref[...] = jnp.zeros(lastlo_ref.shape, F32)

    seq_start = (i % nseq) == 0
    row0 = lax.broadcasted_iota(jnp.int32, (tm, 1), 0) == 0
    h = h_ref[...]

    def shifted_proj(w_r, mu_r, carry_ref):
        x = _dot(h, w_r[...])
        last = jnp.where(seq_start, 0.0, carry_ref[...])
        prev = jnp.where(row0, last, pltpu.roll(x, 1, 0))
        carry_ref[...] = x[tm - 1:tm, :]
        return x + (prev - x) * mu_r[...]

    z = shifted_proj(w_ref, mu_ref, last_ref)
    lo = shifted_proj(wlo_ref, mulo_ref, lastlo_ref)
    r, k, v = z[:, :RW_WIDTH], z[:, RW_WIDTH:2 * RW_WIDTH], z[:, 2 * RW_WIDTH:]
    dw, da, dg = lo[:, :LANES], lo[:, LANES:2 * LANES], lo[:, 2 * LANES:]
    w0, a0, k_k, k_a, r_k = (par_ref[j:j + 1, :] for j in range(5))

    u = w0 + _dot(jnp.tanh(dw).astype(BF16), wup_ref[...])
    ld = -DECAY_SCALE * jax.nn.sigmoid(u)
    a = jax.nn.sigmoid(a0 + _dot(da.astype(BF16), aup_ref[...]))
    g = _dot(jax.nn.sigmoid(dg).astype(BF16), gup_ref[...])

    bd = _block_diag_ones(LANES, RW_HEAD_DIM)
    kk = k * k_k
    kk = kk / jnp.maximum(jnp.sqrt(_group_sum64(kk * kk, bd, 1)), 1e-12)
    k2 = k * (1.0 + (a - 1.0) * k_a)
    bonus = _group_sum64(r * k2 * r_k, bd, 1) * v

    t_i = lax.broadcasted_iota(jnp.int32, (tm, tm), 0)
    s_i = lax.broadcasted_iota(jnp.int32, (tm, tm), 1)
    same = (t_i // CHUNK) == (s_i // CHUNK)
    tri = jnp.where(same & (s_i <= t_i), 1.0, 0.0).astype(BF16)
    rest = jnp.where(same & (s_i > t_i), 1.0, 0.0).astype(BF16)
    c_i = lax.broadcasted_iota(jnp.int32, (tm // CHUNK, tm), 0)
    cs_i = lax.broadcasted_iota(jnp.int32, (tm // CHUNK, tm), 1)
    whole = jnp.where(cs_i // CHUNK == c_i, 1.0, 0.0).astype(BF16)

    ldh, ldm = _split2(ld)

    def sel(m):
        return _dot(m, ldh) + _dot(m, ldm)

    cum = sel(tri)
    e_neg = jnp.exp(-cum)
    e_rem = jnp.exp(sel(rest))
    b = kk * a
    at_ref[...] = (-kk * jnp.exp(cum - ld)).astype(BF16)
    rt_ref[...] = (r * jnp.exp(cum)).astype(BF16)
    bt_ref[...] = (b * e_neg).astype(BF16)
    kt_ref[...] = (k2 * e_neg).astype(BF16)
    bg_ref[...] = (b * e_rem).astype(BF16)
    kg_ref[...] = (k2 * e_rem).astype(BF16)
    vb_ref[...] = v.astype(BF16)
    bonus_ref[...] = bonus.astype(BF16)
    g_ref[...] = g.astype(BF16)
    gam_ref[0] = jnp.exp(sel(whole))


def _rwkv_prep(h, w_rkv, w_lo, mu_rkv, mu_lo, params, wup, aup, gup, seq, tm):
    t, d = h.shape
    n, nlo = w_rkv.shape[1], w_lo.shape[1]
    nseq = seq // tm
    whole = lambda shape: pl.BlockSpec(shape, lambda i: (0, 0), pipeline_mode=pl.Buffered(1))
    in_specs = [pl.BlockSpec((tm, d), lambda i: (i, 0)), whole((d, n)), whole((d, nlo)), whole((1, n)), whole((1, nlo)),
                whole((SUBLANES, RW_WIDTH)), whole((LANES, RW_WIDTH)), whole((LANES, RW_WIDTH)),
                whole((2 * LANES, RW_WIDTH))]
    out_blk = pl.BlockSpec((tm, RW_WIDTH), lambda i: (i, 0))
    outs = [jax.ShapeDtypeStruct((t, RW_WIDTH), BF16)] * 9
    outs.append(jax.ShapeDtypeStruct((t // tm, tm // CHUNK, RW_WIDTH), F32))
    out_specs = [out_blk] * 9 + [pl.BlockSpec((1, tm // CHUNK, RW_WIDTH), lambda i: (i, 0, 0))]
    return pl.pallas_call(
        functools.partial(_rwkv_prep_kernel, tm=tm, nseq=nseq),
        out_shape=outs,
        grid=(t // tm,),
        in_specs=in_specs,
        out_specs=out_specs,
        scratch_shapes=[pltpu.VMEM((1, n), F32), pltpu.VMEM((1, nlo), F32)],
        compiler_params=_cparams("arbitrary"),
        name="rwkv_prep",
    )(h, w_rkv, w_lo, mu_rkv, mu_lo, params, wup, aup, gup)


PAIR = 2 * RW_HEAD_DIM


def _dot_bf16(a, b, dims=None):
    return _dot(a.astype(BF16), b.astype(BF16), dims)


def _split2(x):
    h = x.astype(BF16)
    return h, (x - h.astype(F32)).astype(BF16)


def _dot_split_lhs(a, b, dims=None):
    ah, al = _split2(a)
    bh = b.astype(BF16)
    return _dot(ah, bh, dims) + _dot(al, bh, dims)


def _rwkv_chunk_kernel(at_ref, rt_ref, bt_ref, kt_ref, bg_ref, kg_ref, v_ref, bonus_ref, g_ref,
                       gam_ref, lng_ref, lnb_ref, *rest, n_cast, cps):
    cast_in, o_ref, cast_out, s_ref = rest[:n_cast], rest[n_cast], rest[n_cast + 1:2 * n_cast + 1], rest[-1]
    c = pl.program_id(1)

    for src, dst in zip(cast_in, cast_out):
        dst[...] = src[...].astype(dst.dtype)

    @pl.when(c == 0)
    def _():
        s_ref[...] = jnp.zeros(s_ref.shape, F32)

    lane = lax.broadcasted_iota(jnp.int32, (1, PAIR), 1)
    first = lane < RW_HEAD_DIM
    rho = lax.broadcasted_iota(jnp.int32, (PAIR, PAIR), 0)
    sig = lax.broadcasted_iota(jnp.int32, (PAIR, PAIR), 1)
    strict, incl, eye = sig < rho, sig <= rho, sig == rho
    own = (rho // RW_HEAD_DIM) == (sig // RW_HEAD_DIM)

    def stacked(x):
        z = jnp.zeros_like(x)
        return jnp.concatenate([jnp.where(first, x, z), jnp.where(first, z, x)], axis=0)

    n_pairs = RW_HEADS // 2
    units = [(ci, p) for ci in range(cps) for p in range(n_pairs)]
    rws = [slice(ci * CHUNK, (ci + 1) * CHUNK) for ci in range(cps)]
    sls = [slice(p * PAIR, (p + 1) * PAIR) for p in range(n_pairs)]
    zero = jnp.zeros((PAIR, PAIR), F32)
    st = [[stacked(ref[rws[ci], sls[p]]) for ref in (at_ref, rt_ref, bt_ref, kt_ref, bg_ref, kg_ref, v_ref)]
          for ci, p in units]
    un = range(len(units))
    prods = [_dot(jnp.concatenate([q[0], q[1]], axis=0), jnp.concatenate([q[2], q[3]], axis=0), NT_DIMS) for q in st]
    lmat = [jnp.where(strict, pr_[:PAIR, :PAIR], zero) for pr_ in prods]
    sak = [jnp.where(strict, pr_[:PAIR, PAIR:], zero) for pr_ in prods]
    lrbk = [jnp.concatenate([jnp.where(incl, pr_[PAIR:, :PAIR], zero), jnp.where(incl, pr_[PAIR:, PAIR:], zero)],
                            axis=1).astype(BF16) for pr_ in prods]
    akv = [_dot(sak[u].astype(BF16), st[u][6]) for u in un]
    ident = jnp.where(eye, 1.0, 0.0)
    li = [_dot_bf16(lmat[u], lmat[u]) for u in un]
    tmat = [ident + lmat[u] for u in un]
    n_steps = CHUNK.bit_length() - 1
    for it in range(1, n_steps):
        if it + 1 < n_steps:
            res = [_dot_bf16(li[u], jnp.concatenate([li[u], tmat[u]], axis=1)) for u in un]
            tmat = [tmat[u] + res[u][:, PAIR:] for u in un]
            li = [res[u][:, :PAIR] for u in un]
        else:
            tmat = [tmat[u] + _dot_bf16(li[u], tmat[u]) for u in un]
    x = [_dot_bf16(tmat[u], jnp.concatenate([st[u][0].astype(F32), akv[u]], axis=1)) for u in un]
    gmat = [jnp.concatenate([x[u], jnp.concatenate([zero, st[u][6].astype(F32)], axis=1)], axis=0).astype(BF16)
            for u in un]
    out1 = [_dot(lrbk[u], gmat[u]) for u in un]
    out2 = [_dot(gmat[u], jnp.concatenate([st[u][4], st[u][5]], axis=0), TN_DIMS) for u in un]
    qe = [out1[u][:, :PAIR] + st[u][1].astype(F32) for u in un]

    state = [s_ref[p] for p in range(n_pairs)]
    for ci in range(cps):
        base = ci * n_pairs
        y = [_dot_bf16(qe[base + p], state[p], NT_DIMS) + out1[base + p][:, PAIR:] for p in range(n_pairs)]
        nxt = []
        for p in range(n_pairs):
            mmat = out2[base + p][:PAIR] + jnp.where(eye, gam_ref[ci][:, sls[p]], 0.0)
            nxt.append(_dot_split_lhs(state[p], mmat) + out2[base + p][PAIR:])
        state = nxt
        for p in range(n_pairs):
            sl = sls[p]
            mean = jnp.sum(y[p], axis=-1, keepdims=True) * (1.0 / RW_HEAD_DIM)
            d = jnp.where(own, y[p] - mean, 0.0)
            var = jnp.sum(d * d, axis=-1, keepdims=True) * (1.0 / RW_HEAD_DIM)
            yn = d * lax.rsqrt(var + GN_EPS)
            yn = yn[:CHUNK] + yn[CHUNK:]
            out = ((yn * lng_ref[:, sl] + lnb_ref[:, sl] + bonus_ref[rws[ci], sl].astype(F32))
                   * g_ref[rws[ci], sl].astype(F32))
            o_ref[rws[ci], sl] = out.astype(o_ref.dtype)
    for p in range(n_pairs):
        s_ref[p] = state[p]


def _rwkv_chunks(prep, lnx_g, lnx_b, batch, seq, cps, casts=()):
    at, rt, bt, kt, bg, kg, vb, bonus, g, gam = prep
    t = at.shape[0]
    nc = seq // (CHUNK * cps)
    steps = batch * nc
    gam = gam.reshape(t // CHUNK, 1, RW_WIDTH)
    blk = pl.BlockSpec((cps * CHUNK, RW_WIDTH), lambda b, c: (b * nc + c, 0))
    rowspec = pl.BlockSpec((1, RW_WIDTH), lambda b, c: (0, 0))
    cast_specs = [pl.BlockSpec((w.shape[0] // steps, w.shape[1]), lambda b, c: (b * nc + c, 0)) for w in casts]
    outs = pl.pallas_call(
        functools.partial(_rwkv_chunk_kernel, n_cast=len(casts), cps=cps),
        out_shape=[jax.ShapeDtypeStruct((t, RW_WIDTH), BF16)] + [jax.ShapeDtypeStruct(w.shape, BF16) for w in casts],
        grid=(batch, nc),
        in_specs=([blk] * 9 + [pl.BlockSpec((cps, 1, RW_WIDTH), lambda b, c: (b * nc + c, 0, 0)), rowspec, rowspec]
                  + cast_specs),
        out_specs=[blk] + cast_specs,
        scratch_shapes=[pltpu.VMEM((RW_HEADS // 2, PAIR, PAIR), F32)],
        compiler_params=_cparams("arbitrary", "arbitrary"),
        name="rwkv_chunks",
    )(at, rt, bt, kt, bg, kg, vb, bonus, g, gam, lnx_g.reshape(1, RW_WIDTH), lnx_b.reshape(1, RW_WIDTH), *casts)
    return outs[0], outs[1:]


def _cast_kernel(x_ref, o_ref):
    o_ref[...] = x_ref[...].astype(o_ref.dtype)


def _cast_bf16(w, rows):
    n, m = w.shape
    return pl.pallas_call(
        _cast_kernel,
        out_shape=jax.ShapeDtypeStruct((n, m), BF16),
        grid=(n // rows,),
        in_specs=[pl.BlockSpec((rows, m), lambda i: (i, 0))],
        out_specs=pl.BlockSpec((rows, m), lambda i: (i, 0)),
        compiler_params=_cparams("parallel"),
        name="cast_bf16",
    )(w)


def _merge_kernel(h_ref, ya_ref, yb_ref, wga_ref, wgb_ref, pa_ref, pb_ref, o_ref):
    h = h_ref[...]
    ga = jax.nn.sigmoid(_dot(h, wga_ref[...]))
    gb = jax.nn.sigmoid(_dot(h, wgb_ref[...]))
    o_ref[...] = (ga * _dot(ya_ref[...], pa_ref[...]) + gb * _dot(yb_ref[...], pb_ref[...])).astype(o_ref.dtype)


def _merge(h, ya, yb, wga, wgb, pa, pb, tm, tn):
    t, d = h.shape
    n = wga.shape[1]
    ka, kb = ya.shape[1], yb.shape[1]
    return pl.pallas_call(
        _merge_kernel,
        out_shape=jax.ShapeDtypeStruct((t, n), BF16),
        grid=(t // tm, n // tn),
        in_specs=[pl.BlockSpec((tm, d), lambda i, j: (i, 0)),
                  pl.BlockSpec((tm, ka), lambda i, j: (i, 0)),
                  pl.BlockSpec((tm, kb), lambda i, j: (i, 0)),
                  pl.BlockSpec((d, tn), lambda i, j: (0, j)),
                  pl.BlockSpec((d, tn), lambda i, j: (0, j)),
                  pl.BlockSpec((ka, tn), lambda i, j: (0, j)),
                  pl.BlockSpec((kb, tn), lambda i, j: (0, j))],
        out_specs=pl.BlockSpec((tm, tn), lambda i, j: (i, j)),
        compiler_params=_cparams("parallel", "parallel"),
        name="gated_merge",
    )(h, ya, yb, wga, wgb, pa, pb)


def _first_lane_where(cond, lane):
    return jnp.min(jnp.where(cond, lane, LANES), axis=-1, keepdims=True)


def _outproj_kernel(m_ref, x_ref, wo_ref, g2_ref, wrh_ref, wrl_ref, rb_ref, x1_ref, h2_ref, route_ref):
    x1 = x_ref[...] + _dot(m_ref[...], wo_ref[...])
    x1_ref[...] = x1
    ms = jnp.mean(x1 * x1, axis=-1, keepdims=True)
    h2 = x1 * lax.rsqrt(ms + NORM_EPS) * g2_ref[...]
    h2_ref[...] = h2
    hh = h2.astype(BF16)
    hl = (h2 - hh.astype(F32)).astype(BF16)
    wrh = wrh_ref[...]
    lg = _dot(hh, wrh) + _dot(hl, wrh) + _dot(hh, wrl_ref[...]) + rb_ref[...]

    lane = lax.broadcasted_iota(jnp.int32, lg.shape, 1)
    neg = -jnp.inf
    is_g = lane < N_GROUPS
    mg = jnp.max(jnp.where(is_g, lg, neg), axis=-1, keepdims=True)
    eg = jnp.where(is_g, jnp.exp(lg - mg), 0.0)
    pg = eg / jnp.sum(eg, axis=-1, keepdims=True)
    p_g_top = jnp.max(pg, axis=-1, keepdims=True)
    g_idx = _first_lane_where(is_g & (pg == p_g_top), lane)
    lo = N_GROUPS + g_idx * EXPERTS_PER_GROUP
    sel = (lane >= lo) & (lane < lo + EXPERTS_PER_GROUP)
    me = jnp.max(jnp.where(sel, lg, neg), axis=-1, keepdims=True)
    ee = jnp.where(sel, jnp.exp(lg - me), 0.0)
    pe = ee / jnp.sum(ee, axis=-1, keepdims=True)
    pe = jnp.where(sel, pe, -1.0)
    v1 = jnp.max(pe, axis=-1, keepdims=True)
    i1 = _first_lane_where(pe == v1, lane)
    pe2 = jnp.where(lane == i1, -1.0, pe)
    v2 = jnp.max(pe2, axis=-1, keepdims=True)
    i2 = _first_lane_where(pe2 == v2, lane)
    den = v1 + v2
    route = jnp.where(lane == 0, p_g_top * v1 / den,
                      jnp.where(lane == 1, p_g_top * v2 / den,
                                jnp.where(lane == 2, (i1 - N_GROUPS).astype(F32),
                                          jnp.where(lane == 3, (i2 - N_GROUPS).astype(F32), 0.0))))
    route_ref[...] = route


def _outproj(merged, x, wo, g2, wr_hi, wr_lo, rbias, tm):
    t, d = x.shape
    nr = wr_hi.shape[1]
    return pl.pallas_call(
        _outproj_kernel,
        out_shape=[jax.ShapeDtypeStruct((t, d), F32), jax.ShapeDtypeStruct((t, d), F32),
                   jax.ShapeDtypeStruct((t, nr), F32)],
        grid=(t // tm,),
        in_specs=[pl.BlockSpec((tm, d), lambda i: (i, 0)),
                  pl.BlockSpec((tm, d), lambda i: (i, 0)),
                  pl.BlockSpec((d, d), lambda i: (0, 0)),
                  pl.BlockSpec((1, d), lambda i: (0, 0)),
                  pl.BlockSpec((d, nr), lambda i: (0, 0)),
                  pl.BlockSpec((d, nr), lambda i: (0, 0)),
                  pl.BlockSpec((1, nr), lambda i: (0, 0))],
        out_specs=[pl.BlockSpec((tm, d), lambda i: (i, 0)),
                   pl.BlockSpec((tm, d), lambda i: (i, 0)),
                   pl.BlockSpec((tm, nr), lambda i: (i, 0))],
        compiler_params=_cparams("parallel"),
        name="outproj_norm_router",
    )(merged, x, wo, g2.reshape(1, d), wr_hi, wr_lo, rbias)


IDX_SLOTS = 3


def _moe_kernel(be_ref, nused_ref, tok_hbm, dst_hbm, h_hbm, wg_ref, wu_ref, wd_ref, y_hbm,
                xbuf, obuf, tok_s, dst_s, idx_sem, g_sem, s_sem, *, rows, n_real):
    b = pl.program_id(0)
    n_used = nused_ref[0]
    last = n_used - 1

    def idx_copies(blk, sl):
        return (pltpu.make_async_copy(tok_hbm.at[blk], tok_s.at[sl], idx_sem.at[sl, 0]),
                pltpu.make_async_copy(dst_hbm.at[blk], dst_s.at[sl], idx_sem.at[sl, 1]))

    def gather_row(r, isl, xsl):
        return pltpu.make_async_copy(h_hbm.at[tok_s[isl, r]], xbuf.at[xsl, r], g_sem.at[xsl])

    def gather_all(xsl):
        return pltpu.make_async_copy(h_hbm.at[pl.ds(0, rows)], xbuf.at[xsl], g_sem.at[xsl])

    def scatter_row(r, isl, osl):
        return pltpu.make_async_copy(obuf.at[osl, r], y_hbm.at[dst_s[isl, r]], s_sem.at[osl])

    def scatter_all(osl):
        return pltpu.make_async_copy(obuf.at[osl], y_hbm.at[pl.ds(0, rows)], s_sem.at[osl])

    def spare_fill(par):
        return pltpu.make_async_copy(obuf.at[1], y_hbm.at[pl.ds(n_real + par * rows, rows)], s_sem.at[1])

    @pl.when(b == 0)
    def _():
        obuf[1] = jnp.zeros(obuf.shape[1:], obuf.dtype)
        for par in range(2):
            spare_fill(par).start()
        for cp in idx_copies(0, 0):
            cp.start()
        for cp in idx_copies(0, 0):
            cp.wait()
        for cp in idx_copies(jnp.minimum(1, last), 1):
            cp.start()

        def issue(r, c):
            gather_row(r, 0, 0).start()
            return c

        lax.fori_loop(0, rows, issue, 0, unroll=8)
        for par in range(2):
            spare_fill(par).wait()

    @pl.when((b > 1) & (b < n_used))
    def _():
        scatter_all(b % 2).wait()

    for par in range(2):
        @pl.when((b < n_used) & (b % 2 == par))
        def _(par=par):
            cur_i, nxt_i, ld_i = b % IDX_SLOTS, (b + 1) % IDX_SLOTS, (b + 2) % IDX_SLOTS
            for cp in idx_copies(0, nxt_i):
                cp.wait()
            for r in range(rows):
                gather_row(r, nxt_i, 1 - par).start()
            for cp in idx_copies(jnp.minimum(b + 2, last), ld_i):
                cp.start()

            gather_all(par).wait()
            x = xbuf[par].astype(BF16)
            gate = _dot(x, wg_ref[...])
            up = _dot(x, wu_ref[...])
            mid = (gate * jax.nn.sigmoid(gate) * up).astype(BF16)
            obuf[par] = _dot(mid, wd_ref[...])
            for r in range(rows):
                scatter_row(r, cur_i, par).start(priority=r % 2)

    @pl.when(b == last)
    def _():
        @pl.when(b > 0)
        def _():
            scatter_all((b - 1) % 2).wait()

        scatter_all(b % 2).wait()
        gather_all((b + 1) % 2).wait()
        for cp in idx_copies(0, (b + 2) % IDX_SLOTS):
            cp.wait()


def _moe(h2, row_tok, row_dst, block_e, n_used, wg, wu, wd, rows):
    t, d = h2.shape
    n_blocks = row_tok.shape[0]
    a = t * TOP_K
    f = wg.shape[2]
    kern = functools.partial(_moe_kernel, rows=rows, n_real=a)
    grid_spec = pltpu.PrefetchScalarGridSpec(
        num_scalar_prefetch=2,
        grid=(n_blocks,),
        in_specs=[pl.BlockSpec(memory_space=pl.ANY),
                  pl.BlockSpec(memory_space=pl.ANY),
                  pl.BlockSpec(memory_space=pl.ANY),
                  pl.BlockSpec((None, d, f), lambda b, be, nu: (be[b], 0, 0)),
                  pl.BlockSpec((None, d, f), lambda b, be, nu: (be[b], 0, 0)),
                  pl.BlockSpec((None, f, d), lambda b, be, nu: (be[b], 0, 0))],
        out_specs=pl.BlockSpec(memory_space=pl.ANY),
        scratch_shapes=[pltpu.VMEM((2, rows, d), F32),
                        pltpu.VMEM((2, rows, d), F32),
                        pltpu.SMEM((IDX_SLOTS, rows), jnp.int32),
                        pltpu.SMEM((IDX_SLOTS, rows), jnp.int32),
                        pltpu.SemaphoreType.DMA((IDX_SLOTS, 2)),
                        pltpu.SemaphoreType.DMA((2,)),
                        pltpu.SemaphoreType.DMA((2,))],
    )
    return pl.pallas_call(
        kern,
        out_shape=jax.ShapeDtypeStruct((a + 2 * rows, d), F32),
        grid_spec=grid_spec,
        compiler_params=_cparams("arbitrary"),
        name="moe_experts",
    )(block_e, n_used, row_tok, row_dst, h2, wg, wu, wd)


def _combine_kernel(x1_ref, y0_ref, y1_ref, route_ref, o_ref):
    gts = route_ref[...]
    o_ref[...] = x1_ref[...] + gts[:, 0:1] * y0_ref[...] + gts[:, 1:2] * y1_ref[...]


def _combine(x1, y, route, tm):
    t, d = x1.shape
    nb = t // tm
    return pl.pallas_call(
        _combine_kernel,
        out_shape=jax.ShapeDtypeStruct((t, d), F32),
        grid=(nb,),
        in_specs=[pl.BlockSpec((tm, d), lambda i: (i, 0)),
                  pl.BlockSpec((tm, d), lambda i: (i, 0)),
                  pl.BlockSpec((tm, d), lambda i: (nb + i, 0)),
                  pl.BlockSpec((tm, LANES), lambda i: (i, 0))],
        out_specs=pl.BlockSpec((tm, d), lambda i: (i, 0)),
        compiler_params=_cparams("parallel"),
        name="moe_combine",
    )(x1, y, y, route)


def _routing_tables(route, rows):
    t = route.shape[0]
    a = t * TOP_K
    expert = route[:, 2:2 + TOP_K].astype(jnp.int32)
    flat_e = expert.reshape(a)
    ids = jnp.arange(a, dtype=jnp.int32)
    order = jnp.sort(flat_e * a + ids) % a
    counts = jnp.sum((flat_e[:, None] == jnp.arange(N_EXPERTS, dtype=jnp.int32)[None, :]).astype(jnp.int32), axis=0)
    padded = (counts + rows - 1) // rows * rows
    start = jnp.cumsum(counts) - counts
    pend = jnp.cumsum(padded)
    pstart = pend - padded
    n_rows = a + N_EXPERTS * rows
    n_blocks = n_rows // rows
    blk_row0 = jnp.arange(n_blocks, dtype=jnp.int32) * rows
    block_e = jnp.minimum(jnp.sum((pend[None, :] <= blk_row0[:, None]).astype(jnp.int32), axis=1), N_EXPERTS - 1)
    off0 = blk_row0 - pstart[block_e]
    cnt_b = counts[block_e]
    off = off0[:, None] + jnp.arange(rows, dtype=jnp.int32)[None, :]
    valid = (off >= 0) & (off < cnt_b[:, None])
    src = order[jnp.clip(start[block_e][:, None] + off, 0, a - 1)]
    row_tok = jnp.where(valid, src // TOP_K, 0)
    spare = a + (jnp.arange(n_blocks, dtype=jnp.int32)[:, None] % 2) * rows + jnp.arange(rows, dtype=jnp.int32)[None, :]
    row_dst = jnp.where(valid, (src % TOP_K) * t + src // TOP_K, spare)
    n_used = (pend[-1] // rows).astype(jnp.int32).reshape(1)
    return row_tok, row_dst, block_e.astype(jnp.int32), n_used


def _tiles(t, seq):
    return dict(
        norm_tm=min(512, t),
        proj_tm=min(1024, seq), proj_tn=512,
        attn_tq=min(256, seq), attn_tk=min(1024, seq), attn_heads=4,
        prep_tm=min(256, seq), rwkv_cps=2,
        merge_tm=min(1024, t), merge_tn=512,
        out_tm=min(512, t),
        moe_rows=256,
        comb_tm=min(512, t),
    )


def _pad_rows(w, n):
    return jnp.pad(w, ((0, n - w.shape[0]), (0, 0)))


def _pad_cols(w, n):
    return jnp.pad(w, ((0, 0), (0, n - w.shape[1])))


def kernel(x, norm1_g, w_in, q_norm_g, k_norm_g, lam_q1, lam_k1, lam_q2, lam_k2, subln_g, shift_mu, w0, w_up, a0, a_up, g_up, k_k, k_a, r_k, lnx_g, lnx_b, proj_a, proj_b, w_out, norm2_g, router_g, router_g_b, router_e, router_e_b, w_gate_e, w_up_e, w_down_e):
    batch, seq, d = x.shape
    t = batch * seq
    depth = norm1_g.shape[0]
    tl = _tiles(t, seq)
    qkw = DA_HEADS * 2 * DA_HEAD_DIM
    vw = DA_HEADS * DA_V_DIM
    c_q, c_k, c_v = 0, qkw, 2 * qkw
    c_rw = c_v + vw
    c_dw = c_rw + 3 * RW_WIDTH
    c_da = c_dw + DECAY_LORA
    c_dg = c_da + AAA_LORA
    c_ga = c_dg + GATE_LORA
    c_gb = c_ga + d
    cos_t, sin_t = _rope_tables(seq)
    xf = x.reshape(t, d)

    for l in range(depth):
        lam_init = 0.8 - 0.6 * math.exp(-0.3 * l)
        wl = w_in[l]
        w_qk = wl[:, c_q:c_v].astype(BF16)
        w_v = wl[:, c_v:c_rw].astype(BF16)
        w_rkv = wl[:, c_rw:c_dw].astype(BF16)
        w_lo = jnp.concatenate([_pad_cols(wl[:, c_dw:c_da], LANES), _pad_cols(wl[:, c_da:c_dg], LANES),
                                wl[:, c_dg:c_ga]], axis=1).astype(BF16)
        w_ga = wl[:, c_ga:c_gb].astype(BF16)
        w_gb = wl[:, c_gb:].astype(BF16)
        mu = shift_mu[l]
        o_dw = 3 * RW_WIDTH
        mu_rkv = mu[:o_dw].reshape(1, -1)
        mu_lo = jnp.concatenate([jnp.pad(mu[o_dw:o_dw + DECAY_LORA], (0, LANES - DECAY_LORA)),
                                 jnp.pad(mu[o_dw + DECAY_LORA:o_dw + DECAY_LORA + AAA_LORA], (0, LANES - AAA_LORA)),
                                 mu[o_dw + DECAY_LORA + AAA_LORA:]]).reshape(1, -1)
        gain_row = jnp.concatenate([jnp.tile(q_norm_g[l], 2 * DA_HEADS) * (DA_HEAD_DIM ** -0.5 * LOG2_E),
                                    jnp.tile(k_norm_g[l], 2 * DA_HEADS)]).reshape(1, 2 * qkw)
        lam_params = jnp.stack([lam_q1[l], lam_k1[l], lam_q2[l], lam_k2[l]])
        zrow = jnp.zeros((RW_WIDTH,), F32)
        rw_params = jnp.stack([w0[l], a0[l], k_k[l], k_a[l], r_k[l].reshape(-1), zrow, zrow, zrow])

        h = _rmsnorm(xf, norm1_g[l], NORM_EPS, tl["norm_tm"])
        qk = _qk_proj(h, w_qk, gain_row, cos_t, sin_t, seq, tl["proj_tm"], tl["proj_tn"])
        v = _matmul(h, w_v, BF16, tl["proj_tm"], tl["proj_tn"], "v_proj")
        ya = _diff_attention(qk, v, lam_params, subln_g[l], batch, seq, lam_init, tl["attn_tq"], tl["attn_tk"],
                             tl["attn_heads"])

        prep = _rwkv_prep(h, w_rkv, w_lo, mu_rkv, mu_lo, rw_params, _pad_rows(w_up[l], LANES).astype(BF16),
                          _pad_rows(a_up[l], LANES).astype(BF16), g_up[l].astype(BF16), seq, tl["prep_tm"])
        ew = (w_gate_e[l], w_up_e[l], w_down_e[l])
        ew2d = tuple(w.reshape(-1, w.shape[-1]) for w in ew)
        cps = tl["rwkv_cps"]
        steps = batch * (seq // (CHUNK * cps))
        ride = all(w.shape[0] % (steps * 2 * SUBLANES) == 0 and w.size * 4 // steps <= CAST_BLOCK_BYTES for w in ew2d)
        yb, ew_bf16 = _rwkv_chunks(prep, lnx_g[l], lnx_b[l], batch, seq, cps, ew2d if ride else ())
        if not ride:
            ew_bf16 = tuple(_cast_bf16(w, CAST_BLOCK_BYTES // (4 * w.shape[1])) for w in ew2d)
        wg_e, wu_e, wd_e = (c.reshape(w.shape) for c, w in zip(ew_bf16, ew))

        merged = _merge(h, ya, yb, w_ga, w_gb, proj_a[l].astype(BF16), proj_b[l].astype(BF16),
                        tl["merge_tm"], tl["merge_tn"])
        wr = _pad_cols(jnp.concatenate([router_g[l], router_e[l]], axis=1), LANES)
        wr_hi = wr.astype(BF16)
        wr_lo = (wr - wr_hi.astype(F32)).astype(BF16)
        rbias = jnp.pad(jnp.concatenate([router_g_b[l], router_e_b[l]]), (0, LANES - N_GROUPS - N_EXPERTS))
        x1, h2, route = _outproj(merged, xf, w_out[l].astype(BF16), norm2_g[l], wr_hi, wr_lo,
                                 rbias.reshape(1, LANES), tl["out_tm"])

        rows = tl["moe_rows"]
        row_tok, row_dst, block_e, n_used = _routing_tables(route, rows)
        y = _moe(h2, row_tok, row_dst, block_e, n_used, wg_e, wu_e, wd_e, rows)
        xf = _combine(x1, y, route, tl["comb_tm"])
    return xf.reshape(batch, seq, d)
```
